```python
import math
import jax, jax.numpy as jnp
from jax import lax
import numpy as np

D_MODEL = 1024
BATCH = 8
SEQ = 4096
DEPTH = 2

N_MIXERS = 2
RET_HEADS = 4
RET_QK_DIM = D_MODEL // RET_HEADS
RET_V_DIM = 2 * RET_QK_DIM
RET_QK_WIDTH = RET_HEADS * RET_QK_DIM
RET_V_WIDTH = RET_HEADS * RET_V_DIM
RET_IN_WIDTH = 2 * RET_QK_WIDTH + 2 * RET_V_WIDTH
RET_CHUNK = 128
ROPE_BASE = 10000.0
FNET_GROUPS = 4
FNET_GROUP_DIM = D_MODEL // FNET_GROUPS
D_FF_DENSE = ((8 * D_MODEL // 3 + 255) // 256) * 256
N_EXPERTS = 8
TOP_K = 2
D_FF_EXPERT = 7 * D_MODEL // 2
EPS = 1e-6

kernel_name = "hybrid_retention_fnet_moe_encoder"


def rms_norm(x, g):
    xf = x.astype(jnp.float32)
    y = xf * lax.rsqrt(jnp.mean(xf * xf, axis=-1, keepdims=True) + EPS)
    return (y * g.astype(jnp.float32)).astype(x.dtype)


def rotary(t):
    s, dh = t.shape[1], t.shape[-1]
    half = dh // 2
    inv = ROPE_BASE ** (-jnp.arange(half, dtype=jnp.float32) / half)
    ang = jnp.arange(s, dtype=jnp.float32)[:, None] * inv[None, :]
    cos = jnp.cos(ang)[None, :, None, :]
    sin = jnp.sin(ang)[None, :, None, :]
    t1 = t[..., :half].astype(jnp.float32)
    t2 = t[..., half:].astype(jnp.float32)
    return jnp.concatenate([t1 * cos - t2 * sin, t1 * sin + t2 * cos], axis=-1).astype(t.dtype)


def chunk_retention(q, k, v, log_gamma, inclusive):
    b, h, s, dk = q.shape
    dv = v.shape[-1]
    c = RET_CHUNK
    n = s // c
    dt = q.dtype
    qc = q.reshape(b, h, n, c, dk)
    kc = k.reshape(b, h, n, c, dk)
    vc = v.reshape(b, h, n, c, dv)
    idx = jnp.arange(c, dtype=jnp.float32)
    diff = idx[:, None] - idx[None, :]
    mask = (diff >= 0) if inclusive else (diff > 0)
    lg = log_gamma[:, None, None]
    decay = jnp.where(mask[None], jnp.exp(jnp.where(mask[None], diff[None] * lg, 0.0)), 0.0)
    scores = jnp.einsum('bhnid,bhnjd->bhnij', qc, kc) * decay[None, :, None].astype(dt)
    intra = jnp.einsum('bhnij,bhnje->bhnie', scores, vc)

    xi = jnp.exp((idx[None, :] + 1.0) * log_gamma[:, None]).astype(dt)
    zeta = jnp.exp((c - 1.0 - idx[None, :]) * log_gamma[:, None]).astype(dt)
    g_chunk = jnp.exp(c * log_gamma).astype(dt)

    def step(state, inp):
        q_n, k_n, v_n = inp
        cross = jnp.einsum('bhid,bhde->bhie', q_n * xi[None, :, :, None], state)
        state = state * g_chunk[None, :, None, None] + jnp.einsum(
            'bhjd,bhje->bhde', k_n * zeta[None, :, :, None], v_n)
        return state, cross

    state0 = jnp.zeros((b, h, dk, dv), dt)
    xs = (jnp.moveaxis(qc, 2, 0), jnp.moveaxis(kc, 2, 0), jnp.moveaxis(vc, 2, 0))
    _, cross = lax.scan(step, state0, xs)
    cross = jnp.moveaxis(cross, 0, 2)
    return (intra + cross).reshape(b, h, s, dv)


def retention_mixer(h, w_in, decay_logit, gn_gain, w_out):
    b, s, _ = h.shape
    proj = h @ w_in
    q, k, v, g = jnp.split(proj, [RET_QK_WIDTH, 2 * RET_QK_WIDTH, 2 * RET_QK_WIDTH + RET_V_WIDTH], axis=-1)
    q = rotary(q.reshape(b, s, RET_HEADS, RET_QK_DIM))
    k = rotary(k.reshape(b, s, RET_HEADS, RET_QK_DIM)) * (RET_QK_DIM ** -0.5)
    v = v.reshape(b, s, RET_HEADS, RET_V_DIM)
    q, k, v = (jnp.transpose(t, (0, 2, 1, 3)) for t in (q, k, v))
    log_g = jax.nn.log_sigmoid(decay_logit.astype(jnp.float32))
    fwd = chunk_retention(q, k, v, log_g[0], True)
    bwd = jnp.flip(chunk_retention(jnp.flip(q, 2), jnp.flip(k, 2), jnp.flip(v, 2), log_g[1], False), 2)
    y = (fwd + bwd).astype(jnp.float32)
    mu = jnp.mean(y, axis=-1, keepdims=True)
    var = jnp.mean(jnp.square(y - mu), axis=-1, keepdims=True)
    yn = (y - mu) * lax.rsqrt(var + EPS)
    yn = jnp.transpose(yn, (0, 2, 1, 3)).reshape(b, s, RET_V_WIDTH) * gn_gain.astype(jnp.float32)
    return (jax.nn.silu(g) * yn.astype(h.dtype)) @ w_out


def fourier_mixer(h, w_out):
    b, s, d = h.shape
    hg = h.astype(jnp.float32).reshape(b, s, FNET_GROUPS, FNET_GROUP_DIM)
    y = jnp.fft.fft2(hg, axes=(1, 3), norm='ortho').real
    return y.reshape(b, s, d).astype(h.dtype) @ w_out


def swiglu(t, w_gate, w_up, w_down):
    return (jax.nn.silu(t @ w_gate) * (t @ w_up)) @ w_down


def moe_swiglu(h, w_router, w_gate, w_up, w_down):
    b, s, d = h.shape
    t = h.reshape(-1, d)
    logits = (t @ w_router).astype(jnp.float32)
    top_val, top_idx = lax.top_k(logits, TOP_K)
    top_w = jax.nn.softmax(top_val, axis=-1)
    gates = jnp.sum(jax.nn.one_hot(top_idx, N_EXPERTS, dtype=jnp.float32) * top_w[..., None], axis=1)
    gates = gates.astype(h.dtype)
    out = jnp.zeros_like(t)
    for e in range(N_EXPERTS):
        out = out + gates[:, e:e + 1] * swiglu(t, w_gate[e], w_up[e], w_down[e])
    return out.reshape(b, s, d)


def setup_inputs(seed: int = 0) -> dict:
    key = jax.random.key(seed)
    ks = jax.random.split(key, 20)
    n_even = (DEPTH + 1) // 2
    n_odd = DEPTH // 2
    f32 = jnp.float32

    def nrm(k, shape, fan_in):
        return jax.random.normal(k, shape, f32) * (fan_in ** -0.5)

    x = jax.random.normal(ks[0], (BATCH, SEQ, D_MODEL), f32)
    mix_norm = 1.0 + 0.02 * jax.random.normal(ks[1], (DEPTH, D_MODEL), f32)
    ffn_norm = 1.0 + 0.02 * jax.random.normal(ks[2], (DEPTH, D_MODEL), f32)
    ret_w_in = nrm(ks[3], (n_even, D_MODEL, RET_IN_WIDTH), D_MODEL)
    gam = 1.0 - jnp.exp(jnp.linspace(math.log(1.0 / 32), math.log(1.0 / 512), RET_HEADS)).astype(f32)
    base_logit = jnp.log(gam) - jnp.log1p(-gam)
    ret_decay_logit = base_logit[None, None, :] + 0.05 * jax.random.normal(ks[4], (n_even, 2, RET_HEADS), f32)
    ret_gn_gain = 1.0 + 0.02 * jax.random.normal(ks[5], (n_even, RET_V_WIDTH), f32)
    ret_w_out = nrm(ks[6], (n_even, RET_V_WIDTH, D_MODEL), RET_V_WIDTH)
    dense_w_gate = nrm(ks[7], (n_even, D_MODEL, D_FF_DENSE), D_MODEL)
    dense_w_up = nrm(ks[8], (n_even, D_MODEL, D_FF_DENSE), D_MODEL)
    dense_w_down = nrm(ks[9], (n_even, D_FF_DENSE, D_MODEL), D_FF_DENSE)
    fnet_w_out = nrm(ks[10], (n_odd, D_MODEL, D_MODEL), D_MODEL)
    moe_router = nrm(ks[11], (n_odd, D_MODEL, N_EXPERTS), D_MODEL)
    moe_w_gate = nrm(ks[12], (n_odd, N_EXPERTS, D_MODEL, D_FF_EXPERT), D_MODEL)
    moe_w_up = nrm(ks[13], (n_odd, N_EXPERTS, D_MODEL, D_FF_EXPERT), D_MODEL)
    moe_w_down = nrm(ks[14], (n_odd, N_EXPERTS, D_FF_EXPERT, D_MODEL), D_FF_EXPERT)
    final_norm = 1.0 + 0.02 * jax.random.normal(ks[15], (D_MODEL,), f32)
    return {"x": x, "mix_norm": mix_norm, "ffn_norm": ffn_norm,
            "ret_w_in": ret_w_in, "ret_decay_logit": ret_decay_logit,
            "ret_gn_gain": ret_gn_gain, "ret_w_out": ret_w_out,
            "dense_w_gate": dense_w_gate, "dense_w_up": dense_w_up, "dense_w_down": dense_w_down,
            "fnet_w_out": fnet_w_out, "moe_router": moe_router,
            "moe_w_gate": moe_w_gate, "moe_w_up": moe_w_up, "moe_w_down": moe_w_down,
            "final_norm": final_norm}


def reference(x, mix_norm, ffn_norm, ret_w_in, ret_decay_logit, ret_gn_gain, ret_w_out,
              dense_w_gate, dense_w_up, dense_w_down, fnet_w_out, moe_router,
              moe_w_gate, moe_w_up, moe_w_down, final_norm):
    h = x
    for i in range(DEPTH):
        j = i // 2
        hn = rms_norm(h, mix_norm[i])
        if i % N_MIXERS == 0:
            h = h + retention_mixer(hn, ret_w_in[j], ret_decay_logit[j], ret_gn_gain[j], ret_w_out[j])
        else:
            h = h + fourier_mixer(hn, fnet_w_out[j])
        hn = rms_norm(h, ffn_norm[i])
        if i % 2 == 0:
            h = h + swiglu(hn, dense_w_gate[j], dense_w_up[j], dense_w_down[j])
        else:
            h = h + moe_swiglu(hn, moe_router[j], moe_w_gate[j], moe_w_up[j], moe_w_down[j])
    return rms_norm(h, final_norm)
```

```python
import functools
import math

import jax
import jax.numpy as jnp
from jax import lax
from jax.experimental import pallas as pl
from jax.experimental.pallas import tpu as pltpu

F32 = jnp.float32
BF16 = jnp.bfloat16

D_MODEL = 1024
RET_HEADS = 4
RET_QK_DIM = D_MODEL // RET_HEADS
RET_V_DIM = 2 * RET_QK_DIM
RET_QK_WIDTH = RET_HEADS * RET_QK_DIM
RET_V_WIDTH = RET_HEADS * RET_V_DIM
RET_IN_WIDTH = 2 * RET_QK_WIDTH + 2 * RET_V_WIDTH
ROPE_BASE = 10000.0
FNET_GROUPS = 4
FNET_GROUP_DIM = D_MODEL // FNET_GROUPS
N_EXPERTS = 8
TOP_K = 2
EPS = 1e-6

RET_CHUNK = 256
VMEM_LIMIT_BYTES = 56 * 1024 * 1024

PROJ_TM, PROJ_TN = 1024, 1024
FFN0_TM, FFN0_FC = 512, 1408
ROW_TM = 1024
DFT_TK, DFT_TS = 1024, 1024
MOE_TM, MOE_FC = 1024, 512


def _params(*semantics):
    return pltpu.CompilerParams(dimension_semantics=semantics, vmem_limit_bytes=VMEM_LIMIT_BYTES)


def _dot(a, b):
    return jnp.dot(a, b, preferred_element_type=F32)


def _rms(x, g):
    return x * lax.rsqrt(jnp.mean(x * x, axis=-1, keepdims=True) + EPS) * g


def _silu(a):
    return a * (1.0 / (1.0 + jnp.exp(-a)))


def _proj_kernel(x_ref, g_ref, cos_ref, sin_ref, w_ref, o_ref, hn_ref):
    j = pl.program_id(1)

    @pl.when(j == 0)
    def _():
        hn_ref[...] = _rms(x_ref[...], g_ref[...]).astype(BF16)

    y = _dot(hn_ref[...], w_ref[...])

    @pl.when(j < 2)
    def _():
        scale = jnp.where(j == 1, RET_QK_DIM ** -0.5, 1.0).astype(F32)
        cos = cos_ref[...] * scale
        sin = sin_ref[...] * scale
        half = RET_QK_DIM // 2
        for h in range(PROJ_TN // RET_QK_DIM):
            lo = h * RET_QK_DIM
            t1 = y[:, lo:lo + half]
            t2 = y[:, lo + half:lo + RET_QK_DIM]
            o_ref[:, lo:lo + half] = (t1 * cos - t2 * sin).astype(BF16)
            o_ref[:, lo + half:lo + RET_QK_DIM] = (t1 * sin + t2 * cos).astype(BF16)

    @pl.when(j >= 2)
    def _():
        o_ref[...] = y.astype(BF16)


def _proj(x2d, g, cos, sin, w_in, seq):
    t = x2d.shape[0]
    tm, tn = PROJ_TM, PROJ_TN
    pos_tiles = seq // tm
    return pl.pallas_call(
        _proj_kernel,
        grid=(t // tm, RET_IN_WIDTH // tn),
        in_specs=[
            pl.BlockSpec((tm, D_MODEL), lambda i, j: (i, 0)),
            pl.BlockSpec((1, D_MODEL), lambda i, j: (0, 0)),
            pl.BlockSpec((tm, RET_QK_DIM // 2), lambda i, j: (i % pos_tiles, 0)),
            pl.BlockSpec((tm, RET_QK_DIM // 2), lambda i, j: (i % pos_tiles, 0)),
            pl.BlockSpec((D_MODEL, tn), lambda i, j: (0, j)),
        ],
        out_specs=pl.BlockSpec((tm, tn), lambda i, j: (i, j)),
        out_shape=jax.ShapeDtypeStruct((t, RET_IN_WIDTH), BF16),
        scratch_shapes=[pltpu.VMEM((tm, D_MODEL), BF16)],
        compiler_params=_params("arbitrary", "arbitrary"),
        name="proj",
    )(x2d, g, cos, sin, w_in)


def _ret_kernel(lg_ref, q_ref, k_ref, v_ref, g_ref, gain_ref, o_ref, yb_ref, sf_ref, sb_ref):
    c = RET_CHUNK
    n_chunks = q_ref.shape[0] // c
    h = pl.program_id(1)
    lgf = lg_ref[0, h]
    lgb = lg_ref[1, h]

    row = lax.broadcasted_iota(jnp.int32, (c, RET_QK_DIM), 0).astype(F32)
    xi_f = jnp.exp((row + 1.0) * lgf)
    zeta_f = jnp.exp((c - 1.0 - row) * lgf)
    xi_b = jnp.exp((c - row) * lgb)
    zeta_b = jnp.exp(row * lgb)
    g_f = jnp.exp(jnp.full((1, RET_V_DIM), c, F32) * lgf)
    g_b = jnp.exp(jnp.full((1, RET_V_DIM), c, F32) * lgb)
    di = lax.broadcasted_iota(jnp.int32, (c, c), 0)
    dj = lax.broadcasted_iota(jnp.int32, (c, c), 1)
    diff = (di - dj).astype(F32)
    decay = jnp.exp(jnp.where(diff >= 0, diff * lgf, -diff * lgb))

    def chunk(n):
        r0 = pl.multiple_of(n * c, c)
        return pl.ds(r0, c)

    def scaled(t, s):
        return (t.astype(F32) * s).astype(BF16)

    def outer(kz, vn):
        return lax.dot_general(kz, vn, (((0,), (0,)), ((), ())), preferred_element_type=F32)

    sb_ref[...] = jnp.zeros_like(sb_ref)

    def bwd(step, carry):
        rows = chunk(n_chunks - 1 - step)
        qn, kn, vn = q_ref[rows, :], k_ref[rows, :], v_ref[rows, :]
        sb = sb_ref[...]
        yb_ref[rows, :] = _dot(scaled(qn, xi_b), sb.astype(BF16))
        sb_ref[...] = sb * g_b + outer(scaled(kn, zeta_b), vn)
        return carry

    lax.fori_loop(0, n_chunks, bwd, 0)

    sf_ref[...] = jnp.zeros_like(sf_ref)
    gain = gain_ref[...]

    def fwd(n, carry):
        rows = chunk(n)
        qn, kn, vn = q_ref[rows, :], k_ref[rows, :], v_ref[rows, :]
        sf = sf_ref[...]
        scores = lax.dot_general(qn, kn, (((1,), (1,)), ((), ())), preferred_element_type=F32) * decay
        y = _dot(scores.astype(BF16), vn) + _dot(scaled(qn, xi_f), sf.astype(BF16)) + yb_ref[rows, :]
        sf_ref[...] = sf * g_f + outer(scaled(kn, zeta_f), vn)
        mu = jnp.mean(y, axis=-1, keepdims=True)
        yc = y - mu
        var = jnp.mean(yc * yc, axis=-1, keepdims=True)
        yn = yc * lax.rsqrt(var + EPS) * gain
        o_ref[rows, :] = (_silu(g_ref[rows, :].astype(F32)) * yn).astype(BF16)
        return carry

    lax.fori_loop(0, n_chunks, fwd, 0)


def _retention(proj, log_gamma, gain, batch, seq):
    t = proj.shape[0]
    qk_blocks = RET_QK_WIDTH // RET_QK_DIM
    k_off = qk_blocks
    v_off = 2 * RET_QK_WIDTH // RET_V_DIM
    g_off = v_off + RET_V_WIDTH // RET_V_DIM
    return pl.pallas_call(
        _ret_kernel,
        grid=(batch, RET_HEADS),
        in_specs=[
            pl.BlockSpec(memory_space=pltpu.SMEM),
            pl.BlockSpec((seq, RET_QK_DIM), lambda b, h: (b, h)),
            pl.BlockSpec((seq, RET_QK_DIM), lambda b, h: (b, k_off + h)),
            pl.BlockSpec((seq, RET_V_DIM), lambda b, h: (b, v_off + h)),
            pl.BlockSpec((seq, RET_V_DIM), lambda b, h: (b, g_off + h)),
            pl.BlockSpec((1, RET_V_DIM), lambda b, h: (0, h)),
        ],
        out_specs=pl.BlockSpec((seq, RET_V_DIM), lambda b, h: (b, h)),
        out_shape=jax.ShapeDtypeStruct((t, RET_V_WIDTH), BF16),
        scratch_shapes=[
            pltpu.VMEM((seq, RET_V_DIM), F32),
            pltpu.VMEM((RET_QK_DIM, RET_V_DIM), F32),
            pltpu.VMEM((RET_QK_DIM, RET_V_DIM), F32),
        ],
        compiler_params=_params("arbitrary", "arbitrary"),
        name="retention",
    )(log_gamma, proj, proj, proj, proj, gain)


def _ffn0_kernel(x_ref, z_ref, wo_ref, g_ref, wg_ref, wu_ref, wd_ref, o_ref, h1_ref, hn_ref, acc_ref):
    j = pl.program_id(1)

    @pl.when(j == 0)
    def _():
        h1 = x_ref[...] + _dot(z_ref[...], wo_ref[...])
        h1_ref[...] = h1
        hn_ref[...] = _rms(h1, g_ref[...]).astype(BF16)

    hn = hn_ref[...]
    hidden = (_silu(_dot(hn, wg_ref[...])) * _dot(hn, wu_ref[...])).astype(BF16)
    part = _dot(hidden, wd_ref[...])

    @pl.when(j == 0)
    def _():
        acc_ref[...] = part

    @pl.when(j > 0)
    def _():
        acc_ref[...] += part

    @pl.when(j == pl.num_programs(1) - 1)
    def _():
        o_ref[...] = h1_ref[...] + acc_ref[...]


def _ffn0(x2d, z, w_out, g, w_gate, w_up, w_down):
    t = x2d.shape[0]
    d_ff = w_gate.shape[1]
    tm, fc = FFN0_TM, FFN0_FC
    return pl.pallas_call(
        _ffn0_kernel,
        grid=(t // tm, d_ff // fc),
        in_specs=[
            pl.BlockSpec((tm, D_MODEL), lambda i, j: (i, 0)),
            pl.BlockSpec((tm, RET_V_WIDTH), lambda i, j: (i, 0)),
            pl.BlockSpec((RET_V_WIDTH, D_MODEL), lambda i, j: (0, 0)),
            pl.BlockSpec((1, D_MODEL), lambda i, j: (0, 0)),
            pl.BlockSpec((D_MODEL, fc), lambda i, j: (0, j)),
            pl.BlockSpec((D_MODEL, fc), lambda i, j: (0, j)),
            pl.BlockSpec((fc, D_MODEL), lambda i, j: (j, 0)),
        ],
        out_specs=pl.BlockSpec((tm, D_MODEL), lambda i, j: (i, 0)),
        out_shape=jax.ShapeDtypeStruct((t, D_MODEL), F32),
        scratch_shapes=[
            pltpu.VMEM((tm, D_MODEL), F32),
            pltpu.VMEM((tm, D_MODEL), BF16),
            pltpu.VMEM((tm, D_MODEL), F32),
        ],
        compiler_params=_params("arbitrary", "arbitrary"),
        name="ffn0",
    )(x2d, z, w_out, g, w_gate, w_up, w_down)


def _fnet_ch_kernel(h_ref, g_ref, cc_ref, sc_ref, xc_ref, xs_ref):
    hn = _rms(h_ref[...], g_ref[...]).astype(BF16)
    cc = cc_ref[...]
    sc = sc_ref[...]
    for grp in range(FNET_GROUPS):
        cols = slice(grp * FNET_GROUP_DIM, (grp + 1) * FNET_GROUP_DIM)
        xc_ref[:, cols] = _dot(hn[:, cols], cc).astype(BF16)
        xs_ref[:, cols] = _dot(hn[:, cols], sc).astype(BF16)


def _fnet_channels(h, g, cc, sc):
    t = h.shape[0]
    tm = ROW_TM
    row = pl.BlockSpec((tm, D_MODEL), lambda i: (i, 0))
    tab = pl.BlockSpec((FNET_GROUP_DIM, FNET_GROUP_DIM), lambda i: (0, 0))
    return pl.pallas_call(
        _fnet_ch_kernel,
        grid=(t // tm,),
        in_specs=[row, pl.BlockSpec((1, D_MODEL), lambda i: (0, 0)), tab, tab],
        out_specs=[row, row],
        out_shape=[jax.ShapeDtypeStruct((t, D_MODEL), BF16)] * 2,
        compiler_params=_params("arbitrary"),
        name="fnet_channels",
    )(h, g, cc, sc)


def _fnet_seq_kernel(cs_ref, ss_ref, xc_ref, xs_ref, o_ref, acc_ref):
    s = pl.program_id(2)
    part = _dot(cs_ref[...], xc_ref[...]) - _dot(ss_ref[...], xs_ref[...])

    @pl.when(s == 0)
    def _():
        acc_ref[...] = part

    @pl.when(s > 0)
    def _():
        acc_ref[...] += part

    @pl.when(s == pl.num_programs(2) - 1)
    def _():
        o_ref[...] = acc_ref[...].astype(BF16)


def _fnet_sequence(cs, ss, xc, xs, batch, seq):
    t = xc.shape[0]
    tk, ts = DFT_TK, DFT_TS
    nk, ns = seq // tk, seq // ts
    tab = pl.BlockSpec((tk, ts), lambda b, k, s: (k, s))
    src = pl.BlockSpec((ts, D_MODEL), lambda b, k, s: (b * ns + s, 0))
    return pl.pallas_call(
        _fnet_seq_kernel,
        grid=(batch, nk, ns),
        in_specs=[tab, tab, src, src],
        out_specs=pl.BlockSpec((tk, D_MODEL), lambda b, k, s: (b * nk + k, 0)),
        out_shape=jax.ShapeDtypeStruct((t, D_MODEL), BF16),
        scratch_shapes=[pltpu.VMEM((tk, D_MODEL), F32)],
        compiler_params=_params("arbitrary", "arbitrary", "arbitrary"),
        name="fnet_sequence",
    )(cs, ss, xc, xs)


def _router_kernel(h_ref, y_ref, w_ref, g_ref, rhi_ref, rlo_ref, h3_ref, hn_ref, idx_ref, wt_ref):
    h3 = h_ref[...] + _dot(y_ref[...], w_ref[...])
    h3_ref[...] = h3
    hn = _rms(h3, g_ref[...])
    hn_ref[...] = hn
    hi = hn.astype(BF16)
    lo = (hn - hi.astype(F32)).astype(BF16)
    logits = _dot(hi, rhi_ref[...]) + (_dot(lo, rhi_ref[...]) + _dot(hi, rlo_ref[...]))
    col = lax.broadcasted_iota(jnp.int32, logits.shape, 1)
    m1 = jnp.max(logits, axis=-1, keepdims=True)
    i1 = jnp.min(jnp.where(logits == m1, col, N_EXPERTS), axis=-1, keepdims=True)
    rest = jnp.where(col == i1, -jnp.inf, logits)
    m2 = jnp.max(rest, axis=-1, keepdims=True)
    i2 = jnp.min(jnp.where(rest == m2, col, N_EXPERTS), axis=-1, keepdims=True)
    e2 = jnp.exp(m2 - m1)
    inv = 1.0 / (1.0 + e2)
    idx_ref[:, 0:1] = i1
    idx_ref[:, 1:2] = i2
    wt_ref[:, 0:1] = inv
    wt_ref[:, 1:2] = e2 * inv


def _router(h2, y, w_fnet, g, r_hi, r_lo):
    t = h2.shape[0]
    tm = ROW_TM
    row = pl.BlockSpec((tm, D_MODEL), lambda i: (i, 0))
    pair = pl.BlockSpec((tm, TOP_K), lambda i: (i, 0))
    rtr = pl.BlockSpec((D_MODEL, N_EXPERTS), lambda i: (0, 0))
    return pl.pallas_call(
        _router_kernel,
        grid=(t // tm,),
        in_specs=[row, row, pl.BlockSpec((D_MODEL, D_MODEL), lambda i: (0, 0)),
                  pl.BlockSpec((1, D_MODEL), lambda i: (0, 0)), rtr, rtr],
        out_specs=[row, row, pair, pair],
        out_shape=[jax.ShapeDtypeStruct((t, D_MODEL), F32), jax.ShapeDtypeStruct((t, D_MODEL), F32),
                   jax.ShapeDtypeStruct((t, TOP_K), jnp.int32), jax.ShapeDtypeStruct((t, TOP_K), F32)],
        compiler_params=_params("arbitrary"),
        name="router",
    )(h2, y, w_fnet, g, r_hi, r_lo)


def _routing_tables(top_idx, top_w, tm):
    t = top_idx.shape[0]
    n_pairs = TOP_K * t
    n_tiles = n_pairs // tm + N_EXPERTS
    e_flat = top_idx.T.reshape(n_pairs)
    w_flat = top_w.T.reshape(n_pairs)
    order = jnp.argsort(e_flat, stable=True).astype(jnp.int32)
    experts = jnp.arange(N_EXPERTS, dtype=jnp.int32)
    counts = jnp.sum((e_flat[:, None] == experts[None, :]).astype(jnp.int32), axis=0)
    off = jnp.cumsum(counts) - counts
    tiles_e = (counts + tm - 1) // tm
    tile_end = jnp.cumsum(tiles_e)
    tile_off = tile_end - tiles_e
    n_used = tile_end[-1]
    last_e = jnp.max(jnp.where(tiles_e > 0, experts, 0))
    tile_ids = jnp.arange(n_tiles, dtype=jnp.int32)
    tile_e = jnp.minimum(jnp.sum((tile_ids[:, None] >= tile_end[None, :]).astype(jnp.int32), axis=1), last_e)
    u = jnp.arange(tm, dtype=jnp.int32)
    rank = (tile_ids - tile_off[tile_e])[:, None] * tm + u[None, :]
    valid = (tile_ids < n_used)[:, None] & (rank < counts[tile_e][:, None])
    src = order[jnp.clip(off[tile_e][:, None] + rank, 0, n_pairs - 1)]
    tok = jnp.where(valid, src % t, 0).astype(jnp.int32)
    dst = jnp.where(valid, src, 0).astype(jnp.int32)
    w_row = jnp.where(valid, w_flat[src], 0.0).astype(F32)
    n_valid = jnp.where(tile_ids < n_used, jnp.clip(counts[tile_e] - (tile_ids - tile_off[tile_e]) * tm, 0, tm), 0)
    return (tile_e.astype(jnp.int32), n_used.reshape(1).astype(jnp.int32), n_valid.astype(jnp.int32),
            tok.reshape(n_tiles, 1, tm), dst.reshape(n_tiles, 1, tm), w_row.reshape(n_tiles * tm, 1))


def _moe_kernel(te_ref, nu_ref, nv_ref, tok_ref, tokn_ref, dst_ref, wrow_ref, hn_hbm, wg_ref, wu_ref, wd_ref,
                y2_hbm, xbuf, xb, acc, ybuf, gsem, ssem):
    del te_ref
    i = pl.program_id(0)
    j = pl.program_id(1)
    n_tiles = pl.num_programs(0)
    n_ff = pl.num_programs(1)
    n_used = nu_ref[0]
    tm = xbuf.shape[0]
    unroll = 8

    def gather_copy(tok, u):
        return pltpu.make_async_copy(hn_hbm.at[pl.ds(tok, 1), :], xbuf.at[pl.ds(u, 1), :], gsem)

    def scatter_copy(u, row):
        return pltpu.make_async_copy(ybuf.at[pl.ds(u, 1), :], y2_hbm.at[pl.ds(row, 1), :], ssem)

    def start_gather(ref):
        def body(u, carry):
            gather_copy(ref[0, 0, u], u).start()
            return carry
        lax.fori_loop(0, tm, body, 0, unroll=unroll)

    def wait_gather():
        pltpu.make_async_copy(hn_hbm.at[pl.ds(0, tm), :], xbuf, gsem).wait()

    def start_scatter(n_rows):
        def body_many(c, carry):
            for r in range(unroll):
                u = c * unroll + r
                scatter_copy(u, dst_ref[0, 0, u]).start()
            return carry

        def body_one(u, carry):
            scatter_copy(u, dst_ref[0, 0, u]).start()
            return carry
        whole = n_rows // unroll
        lax.fori_loop(0, whole, body_many, 0)
        lax.fori_loop(whole * unroll, n_rows, body_one, 0)

    def wait_scatter(n_rows):
        whole = pl.multiple_of((n_rows // unroll) * unroll, unroll)

        @pl.when(whole > 0)
        def _():
            pltpu.make_async_copy(ybuf.at[pl.ds(0, whole), :], y2_hbm.at[pl.ds(0, whole), :], ssem).wait()

        def body_one(u, carry):
            scatter_copy(0, 0).wait()
            return carry
        lax.fori_loop(whole, n_rows, body_one, 0)

    @pl.when(i < n_used)
    def _():
        @pl.when(j == 0)
        def _():
            @pl.when(i == 0)
            def _():
                start_gather(tok_ref)
            wait_gather()
            xb[...] = xbuf[...].astype(BF16)

            @pl.when(i + 1 < n_used)
            def _():
                start_gather(tokn_ref)

        x = xb[...]
        hidden = (_silu(_dot(x, wg_ref[0])) * _dot(x, wu_ref[0])).astype(BF16)
        part = _dot(hidden, wd_ref[0])

        @pl.when(j == 0)
        def _():
            acc[...] = part

        @pl.when(j > 0)
        def _():
            acc[...] += part

    @pl.when((j == n_ff - 2) & (i >= 1) & (i - 1 < n_used))
    def _():
        wait_scatter(nv_ref[jnp.maximum(i - 1, 0)])

    @pl.when((j == n_ff - 1) & (i < n_used))
    def _():
        ybuf[...] = acc[...] * wrow_ref[...]
        start_scatter(nv_ref[i])

        @pl.when(i == n_tiles - 1)
        def _():
            wait_scatter(nv_ref[i])


def _moe(hn, tile_e, n_used, n_valid, tok, dst, w_row, w_gate, w_up, w_down):
    t = hn.shape[0]
    d_ff = w_gate.shape[2]
    tm, fc = MOE_TM, MOE_FC
    n_tiles = tok.shape[0]
    n_ff = d_ff // fc
    assert n_ff >= 2
    last = n_tiles - 1

    def ff_step(i, j, nu):
        return jnp.where(i < nu[0], j, n_ff - 1)

    grid_spec = pltpu.PrefetchScalarGridSpec(
        num_scalar_prefetch=3,
        grid=(n_tiles, n_ff),
        in_specs=[
            pl.BlockSpec((1, 1, tm), lambda i, j, te, nu, nv: (i, 0, 0), memory_space=pltpu.SMEM),
            pl.BlockSpec((1, 1, tm), lambda i, j, te, nu, nv: (jnp.minimum(i + 1, last), 0, 0),
                         memory_space=pltpu.SMEM),
            pl.BlockSpec((1, 1, tm), lambda i, j, te, nu, nv: (i, 0, 0), memory_space=pltpu.SMEM),
            pl.BlockSpec((tm, 1), lambda i, j, te, nu, nv: (i, 0)),
            pl.BlockSpec(memory_space=pl.ANY),
            pl.BlockSpec((1, D_MODEL, fc), lambda i, j, te, nu, nv: (te[i], 0, ff_step(i, j, nu))),
            pl.BlockSpec((1, D_MODEL, fc), lambda i, j, te, nu, nv: (te[i], 0, ff_step(i, j, nu))),
            pl.BlockSpec((1, fc, D_MODEL), lambda i, j, te, nu, nv: (te[i], ff_step(i, j, nu), 0)),
        ],
        out_specs=pl.BlockSpec(memory_space=pl.ANY),
        scratch_shapes=[
            pltpu.VMEM((tm, D_MODEL), F32),
            pltpu.VMEM((tm, D_MODEL), BF16),
            pltpu.VMEM((tm, D_MODEL), F32),
            pltpu.VMEM((tm, D_MODEL), F32),
            pltpu.SemaphoreType.DMA(()),
            pltpu.SemaphoreType.DMA(()),
        ],
    )
    return pl.pallas_call(
        _moe_kernel,
        grid_spec=grid_spec,
        out_shape=jax.ShapeDtypeStruct((TOP_K * t, D_MODEL), F32),
        compiler_params=_params("arbitrary", "arbitrary"),
        name="moe",
    )(tile_e, n_used, n_valid, tok, tok, dst, w_row, hn, w_gate, w_up, w_down)


def _final_kernel(h_ref, ya_ref, yb_ref, g_ref, o_ref):
    o_ref[...] = _rms(h_ref[...] + (ya_ref[...] + yb_ref[...]), g_ref[...])


def _final(h3, y2, g):
    t = h3.shape[0]
    tm = ROW_TM
    second = t // tm
    row = pl.BlockSpec((tm, D_MODEL), lambda i: (i, 0))
    return pl.pallas_call(
        _final_kernel,
        grid=(t // tm,),
        in_specs=[row, row, pl.BlockSpec((tm, D_MODEL), lambda i: (i + second, 0)),
                  pl.BlockSpec((1, D_MODEL), lambda i: (0, 0))],
        out_specs=row,
        out_shape=jax.ShapeDtypeStruct((t, D_MODEL), F32),
        compiler_params=_params("arbitrary"),
        name="final_norm",
    )(h3, y2, y2, g)


def _rotary_tables(seq):
    half = RET_QK_DIM // 2
    inv = ROPE_BASE ** (-jnp.arange(half, dtype=F32) / half)
    ang = jnp.arange(seq, dtype=F32)[:, None] * inv[None, :]
    return jnp.cos(ang), jnp.sin(ang)


def _dft_tables(n, scale):
    k = jnp.arange(n, dtype=jnp.int32)
    ang = ((k[:, None] * k[None, :]) % n).astype(F32) * (2.0 * math.pi / n)
    return (jnp.cos(ang) * scale).astype(BF16), (jnp.sin(ang) * scale).astype(BF16)


def kernel(x, mix_norm, ffn_norm, ret_w_in, ret_decay_logit, ret_gn_gain, ret_w_out,
           dense_w_gate, dense_w_up, dense_w_down, fnet_w_out, moe_router,
           moe_w_gate, moe_w_up, moe_w_down, final_norm):
    batch, seq, d = x.shape
    t = batch * seq
    assert d == D_MODEL and seq % RET_CHUNK == 0 and seq % PROJ_TM == 0 and seq % DFT_TK == 0
    x2d = x.reshape(t, d)

    cos, sin = _rotary_tables(seq)
    proj = _proj(x2d, mix_norm[0:1], cos, sin, ret_w_in[0].astype(BF16), seq)
    log_gamma = jax.nn.log_sigmoid(ret_decay_logit[0].astype(F32))
    z = _retention(proj, log_gamma, ret_gn_gain[0:1], batch, seq)
    h2 = _ffn0(x2d, z, ret_w_out[0].astype(BF16), ffn_norm[0:1],
               dense_w_gate[0].astype(BF16), dense_w_up[0].astype(BF16), dense_w_down[0].astype(BF16))

    ch_scale = FNET_GROUP_DIM ** -0.5
    cc, sc = _dft_tables(FNET_GROUP_DIM, ch_scale)
    cs, ss = _dft_tables(seq, seq ** -0.5)
    xc, xs = _fnet_channels(h2, mix_norm[1:2], cc, sc)
    y = _fnet_sequence(cs, ss, xc, xs, batch, seq)
    r_hi = moe_router[0].astype(BF16)
    r_lo = (moe_router[0] - r_hi.astype(F32)).astype(BF16)
    h3, hn3, top_idx, top_w = _router(h2, y, fnet_w_out[0].astype(BF16), ffn_norm[1:2], r_hi, r_lo)

    tile_e, n_used, n_valid, tok, dst, w_row = _routing_tables(top_idx, top_w, MOE_TM)
    y2 = _moe(hn3, tile_e, n_used, n_valid, tok, dst, w_row,
              moe_w_gate[0].astype(BF16), moe_w_up[0].astype(BF16), moe_w_down[0].astype(BF16))
    out = _final(h3, y2, final_norm.reshape(1, d))
    return out.reshape(batch, seq, d)
```

```python
import functools
import math

import jax
import jax.numpy as jnp
from jax import lax
from jax.experimental import pallas as pl
from jax.experimental.pallas import tpu as pltpu

F32 = jnp.float32
BF16 = jnp.bfloat16

D_MODEL = 1024
RET_HEADS = 4
RET_QK_DIM = D_MODEL // RET_HEADS
RET_V_DIM = 2 * RET_QK_DIM
RET_QK_WIDTH = RET_HEADS * RET_QK_DIM
RET_V_WIDTH = RET_HEADS * RET_V_DIM
RET_IN_WIDTH = 2 * RET_QK_WIDTH + 2 * RET_V_WIDTH
ROPE_BASE = 10000.0
FNET_GROUPS = 4
FNET_GROUP_DIM = D_MODEL // FNET_GROUPS
N_EXPERTS = 8
TOP_K = 2
EPS = 1e-6

RET_CHUNK = 256
VMEM_LIMIT_BYTES = 56 * 1024 * 1024

PROJ_TM, PROJ_TN = 1024, 1024
FFN0_TM, FFN0_FC = 512, 1408
ROW_TM = 1024
FNET_SEQ_BLOCK = 256
FNET_DT = 256
FNET_ROW_CHUNK = 16
FNET_K1_GROUP = 4
MOE_TM, MOE_FC, MOE_SUB = 1024, 1792, 512


def _params(*semantics):
    return pltpu.CompilerParams(dimension_semantics=semantics, vmem_limit_bytes=VMEM_LIMIT_BYTES)


def _dot(a, b):
    return jnp.dot(a, b, preferred_element_type=F32)


def _rms(x, g):
    return x * lax.rsqrt(jnp.mean(x * x, axis=-1, keepdims=True) + EPS) * g


def _silu(a):
    return a * (1.0 / (1.0 + jnp.exp(-a)))


def _proj_kernel(x_ref, g_ref, cos_ref, sin_ref, w_ref, o_ref, hn_ref):
    j = pl.program_id(1)

    @pl.when(j == 0)
    def _():
        hn_ref[...] = _rms(x_ref[...], g_ref[...]).astype(BF16)

    y = _dot(hn_ref[...], w_ref[...])

    @pl.when(j < 2)
    def _():
        scale = jnp.where(j == 1, RET_QK_DIM ** -0.5, 1.0).astype(F32)
        cos = cos_ref[...] * scale
        sin = sin_ref[...] * scale
        half = RET_QK_DIM // 2
        for h in range(PROJ_TN // RET_QK_DIM):
            lo = h * RET_QK_DIM
            t1 = y[:, lo:lo + half]
            t2 = y[:, lo + half:lo + RET_QK_DIM]
            o_ref[:, lo:lo + half] = (t1 * cos - t2 * sin).astype(BF16)
            o_ref[:, lo + half:lo + RET_QK_DIM] = (t1 * sin + t2 * cos).astype(BF16)

    @pl.when(j >= 2)
    def _():
        o_ref[...] = y.astype(BF16)


def _proj(x2d, g, cos, sin, w_in, seq):
    t = x2d.shape[0]
    tm, tn = PROJ_TM, PROJ_TN
    pos_tiles = seq // tm
    return pl.pallas_call(
        _proj_kernel,
        grid=(t // tm, RET_IN_WIDTH // tn),
        in_specs=[
            pl.BlockSpec((tm, D_MODEL), lambda i, j: (i, 0)),
            pl.BlockSpec((1, D_MODEL), lambda i, j: (0, 0)),
            pl.BlockSpec((tm, RET_QK_DIM // 2), lambda i, j: (i % pos_tiles, 0)),
            pl.BlockSpec((tm, RET_QK_DIM // 2), lambda i, j: (i % pos_tiles, 0)),
            pl.BlockSpec((D_MODEL, tn), lambda i, j: (0, j)),
        ],
        out_specs=pl.BlockSpec((tm, tn), lambda i, j: (i, j)),
        out_shape=jax.ShapeDtypeStruct((t, RET_IN_WIDTH), BF16),
        scratch_shapes=[pltpu.VMEM((tm, D_MODEL), BF16)],
        compiler_params=_params("arbitrary", "arbitrary"),
        name="proj",
    )(x2d, g, cos, sin, w_in)


def _ret_kernel(lg_ref, q_ref, k_ref, v_ref, g_ref, gain_ref, o_ref, yb_ref, sf_ref, sb_ref):
    c = RET_CHUNK
    n_chunks = q_ref.shape[0] // c
    h = pl.program_id(1)
    lgf = lg_ref[0, h]
    lgb = lg_ref[1, h]

    row = lax.broadcasted_iota(jnp.int32, (c, RET_QK_DIM), 0).astype(F32)
    xi_f = jnp.exp((row + 1.0) * lgf)
    zeta_f = jnp.exp((c - 1.0 - row) * lgf)
    xi_b = jnp.exp((c - row) * lgb)
    zeta_b = jnp.exp(row * lgb)
    g_f = jnp.exp(jnp.full((1, RET_V_DIM), c, F32) * lgf)
    g_b = jnp.exp(jnp.full((1, RET_V_DIM), c, F32) * lgb)
    di = lax.broadcasted_iota(jnp.int32, (c, c), 0)
    dj = lax.broadcasted_iota(jnp.int32, (c, c), 1)
    diff = (di - dj).astype(F32)
    decay = jnp.exp(jnp.where(diff >= 0, diff * lgf, -diff * lgb))

    def chunk(n):
        r0 = pl.multiple_of(n * c, c)
        return pl.ds(r0, c)

    def scaled(t, s):
        return (t.astype(F32) * s).astype(BF16)

    def outer(kz, vn):
        return lax.dot_general(kz, vn, (((0,), (0,)), ((), ())), preferred_element_type=F32)

    sb_ref[...] = jnp.zeros_like(sb_ref)

    def bwd(step, carry):
        rows = chunk(n_chunks - 1 - step)
        qn, kn, vn = q_ref[rows, :], k_ref[rows, :], v_ref[rows, :]
        sb = sb_ref[...]
        yb_ref[rows, :] = _dot(scaled(qn, xi_b), sb.astype(BF16))
        sb_ref[...] = sb * g_b + outer(scaled(kn, zeta_b), vn)
        return carry

    lax.fori_loop(0, n_chunks, bwd, 0)

    sf_ref[...] = jnp.zeros_like(sf_ref)
    gain = gain_ref[...]

    def fwd(n, carry):
        rows = chunk(n)
        qn, kn, vn = q_ref[rows, :], k_ref[rows, :], v_ref[rows, :]
        sf = sf_ref[...]
        scores = lax.dot_general(qn, kn, (((1,), (1,)), ((), ())), preferred_element_type=F32) * decay
        y = _dot(scores.astype(BF16), vn) + _dot(scaled(qn, xi_f), sf.astype(BF16)) + yb_ref[rows, :]
        sf_ref[...] = sf * g_f + outer(scaled(kn, zeta_f), vn)
        mu = jnp.mean(y, axis=-1, keepdims=True)
        yc = y - mu
        var = jnp.mean(yc * yc, axis=-1, keepdims=True)
        yn = yc * lax.rsqrt(var + EPS) * gain
        o_ref[rows, :] = (_silu(g_ref[rows, :].astype(F32)) * yn).astype(BF16)
        return carry

    lax.fori_loop(0, n_chunks, fwd, 0)


def _retention(proj, log_gamma, gain, batch, seq):
    t = proj.shape[0]
    qk_blocks = RET_QK_WIDTH // RET_QK_DIM
    k_off = qk_blocks
    v_off = 2 * RET_QK_WIDTH // RET_V_DIM
    g_off = v_off + RET_V_WIDTH // RET_V_DIM
    return pl.pallas_call(
        _ret_kernel,
        grid=(batch, RET_HEADS),
        in_specs=[
            pl.BlockSpec(memory_space=pltpu.SMEM),
            pl.BlockSpec((seq, RET_QK_DIM), lambda b, h: (b, h)),
            pl.BlockSpec((seq, RET_QK_DIM), lambda b, h: (b, k_off + h)),
            pl.BlockSpec((seq, RET_V_DIM), lambda b, h: (b, v_off + h)),
            pl.BlockSpec((seq, RET_V_DIM), lambda b, h: (b, g_off + h)),
            pl.BlockSpec((1, RET_V_DIM), lambda b, h: (0, h)),
        ],
        out_specs=pl.BlockSpec((seq, RET_V_DIM), lambda b, h: (b, h)),
        out_shape=jax.ShapeDtypeStruct((t, RET_V_WIDTH), BF16),
        scratch_shapes=[
            pltpu.VMEM((seq, RET_V_DIM), F32),
            pltpu.VMEM((RET_QK_DIM, RET_V_DIM), F32),
            pltpu.VMEM((RET_QK_DIM, RET_V_DIM), F32),
        ],
        compiler_params=_params("arbitrary", "arbitrary"),
        name="retention",
    )(log_gamma, proj, proj, proj, proj, gain)


def _ffn0_kernel(x_ref, z_ref, wo_ref, g_ref, wg_ref, wu_ref, wd_ref, o_ref, h1_ref, hn_ref, acc_ref):
    j = pl.program_id(1)

    @pl.when(j == 0)
    def _():
        h1 = x_ref[...] + _dot(z_ref[...], wo_ref[...])
        h1_ref[...] = h1
        hn_ref[...] = _rms(h1, g_ref[...]).astype(BF16)

    hn = hn_ref[...]
    hidden = (_silu(_dot(hn, wg_ref[...])) * _dot(hn, wu_ref[...])).astype(BF16)
    part = _dot(hidden, wd_ref[...])

    @pl.when(j == 0)
    def _():
        acc_ref[...] = part

    @pl.when(j > 0)
    def _():
        acc_ref[...] += part

    @pl.when(j == pl.num_programs(1) - 1)
    def _():
        o_ref[...] = h1_ref[...] + acc_ref[...]


def _ffn0(x2d, z, w_out, g, w_gate, w_up, w_down):
    t = x2d.shape[0]
    d_ff = w_gate.shape[1]
    tm, fc = FFN0_TM, FFN0_FC
    return pl.pallas_call(
        _ffn0_kernel,
        grid=(t // tm, d_ff // fc),
        in_specs=[
            pl.BlockSpec((tm, D_MODEL), lambda i, j: (i, 0)),
            pl.BlockSpec((tm, RET_V_WIDTH), lambda i, j: (i, 0)),
            pl.BlockSpec((RET_V_WIDTH, D_MODEL), lambda i, j: (0, 0)),
            pl.BlockSpec((1, D_MODEL), lambda i, j: (0, 0)),
            pl.BlockSpec((D_MODEL, fc), lambda i, j: (0, j)),
            pl.BlockSpec((D_MODEL, fc), lambda i, j: (0, j)),
            pl.BlockSpec((fc, D_MODEL), lambda i, j: (j, 0)),
        ],
        out_specs=pl.BlockSpec((tm, D_MODEL), lambda i, j: (i, 0)),
        out_shape=jax.ShapeDtypeStruct((t, D_MODEL), F32),
        scratch_shapes=[
            pltpu.VMEM((tm, D_MODEL), F32),
            pltpu.VMEM((tm, D_MODEL), BF16),
            pltpu.VMEM((tm, D_MODEL), F32),
        ],
        compiler_params=_params("arbitrary", "arbitrary"),
        name="ffn0",
    )(x2d, z, w_out, g, w_gate, w_up, w_down)


def _fnet_ch_kernel(h_ref, g_ref, cc_ref, sc_ref, xc_ref, xs_ref):
    hn = _rms(h_ref[...], g_ref[...]).astype(BF16)
    cc = cc_ref[...]
    sc = sc_ref[...]
    for grp in range(FNET_GROUPS):
        cols = slice(grp * FNET_GROUP_DIM, (grp + 1) * FNET_GROUP_DIM)
        xc_ref[:, cols] = _dot(hn[:, cols], cc).astype(BF16)
        xs_ref[:, cols] = _dot(hn[:, cols], sc).astype(BF16)


def _fnet_channels(h, g, cc, sc):
    t = h.shape[0]
    tm = ROW_TM
    row = pl.BlockSpec((tm, D_MODEL), lambda i: (i, 0))
    tab = pl.BlockSpec((FNET_GROUP_DIM, FNET_GROUP_DIM), lambda i: (0, 0))
    return pl.pallas_call(
        _fnet_ch_kernel,
        grid=(t // tm,),
        in_specs=[row, pl.BlockSpec((1, D_MODEL), lambda i: (0, 0)), tab, tab],
        out_specs=[row, row],
        out_shape=[jax.ShapeDtypeStruct((t, D_MODEL), BF16)] * 2,
        compiler_params=_params("arbitrary"),
        name="fnet_channels",
    )(h, g, cc, sc)


def _dft_across_blocks(xs):
    n = len(xs)
    if n == 1:
        return xs
    even = _dft_across_blocks(xs[0::2])
    odd = _dft_across_blocks(xs[1::2])
    out = [None] * n
    for k in range(n // 2):
        er, ei = even[k]
        o_r, o_i = odd[k]
        if k == 0:
            tr, ti = o_r, o_i
            out[k], out[k + n // 2] = (er + tr, ei + ti), (er - tr, ei - ti)
        elif 4 * k == n:
            out[k], out[k + n // 2] = (er + o_i, ei - o_r), (er - o_i, ei + o_r)
        else:
            c, s = math.cos(2.0 * math.pi * k / n), math.sin(2.0 * math.pi * k / n)
            tr = o_r * c + o_i * s
            ti = o_i * c - o_r * s
            out[k], out[k + n // 2] = (er + tr, ei + ti), (er - tr, ei - ti)
    return out


def _fnet_blockdft_kernel(zr_ref, zi_ref, ar_ref, ai_ref):
    blk = FNET_SEQ_BLOCK
    n1 = zr_ref.shape[0] // blk
    rc = FNET_ROW_CHUNK
    lanes = 128

    def body(r, carry):
        r0 = pl.multiple_of(r * rc, rc)
        for l in range(zr_ref.shape[1] // lanes):
            cols = slice(l * lanes, (l + 1) * lanes)
            zs = [(zr_ref[pl.ds(s1 * blk + r0, rc), cols].astype(F32),
                   zi_ref[pl.ds(s1 * blk + r0, rc), cols].astype(F32)) for s1 in range(n1)]
            for k1, (a_r, a_i) in enumerate(_dft_across_blocks(zs)):
                ar_ref[pl.ds(k1 * blk + r0, rc), cols] = a_r.astype(BF16)
                ai_ref[pl.ds(k1 * blk + r0, rc), cols] = a_i.astype(BF16)
        return carry

    lax.fori_loop(0, blk // rc, body, 0)


def _fnet_blockdft(zr, zi, batch, seq):
    t = zr.shape[0]
    dt = FNET_DT
    blkspec = pl.BlockSpec((seq, dt), lambda b, c: (b, c))
    return pl.pallas_call(
        _fnet_blockdft_kernel,
        grid=(batch, D_MODEL // dt),
        in_specs=[blkspec, blkspec],
        out_specs=[blkspec, blkspec],
        out_shape=[jax.ShapeDtypeStruct((t, D_MODEL), BF16)] * 2,
        compiler_params=_params("arbitrary", "arbitrary"),
        name="fnet_blockdft",
    )(zr, zi)


def _fnet_seq_kernel(gc_ref, gs_ref, ar_ref, ai_ref, o_ref):
    blk = FNET_SEQ_BLOCK
    for q in range(gc_ref.shape[0] // blk):
        rows = slice(q * blk, (q + 1) * blk)
        y = _dot(gc_ref[rows, :], ar_ref[rows, :]) + _dot(gs_ref[rows, :], ai_ref[rows, :])
        o_ref[:, q * D_MODEL:(q + 1) * D_MODEL] = y.astype(BF16)


def _fnet_sequence(gc, gs, ar, ai, batch, seq):
    blk = FNET_SEQ_BLOCK
    n1 = seq // blk
    kg = min(FNET_K1_GROUP, n1)
    groups = n1 // kg
    tab = pl.BlockSpec((kg * blk, blk), lambda b, g: (g, 0))
    src = pl.BlockSpec((kg * blk, D_MODEL), lambda b, g: (b * groups + g, 0))
    return pl.pallas_call(
        _fnet_seq_kernel,
        grid=(batch, groups),
        in_specs=[tab, tab, src, src],
        out_specs=pl.BlockSpec((blk, kg * D_MODEL), lambda b, g: (b, g)),
        out_shape=jax.ShapeDtypeStruct((batch * blk, n1 * D_MODEL), BF16),
        compiler_params=_params("arbitrary", "arbitrary"),
        name="fnet_sequence",
    )(gc, gs, ar, ai)


def _router_kernel(h_ref, y_ref, w_ref, g_ref, rhi_ref, rlo_ref, h3_ref, hn_ref, idx_ref, wt_ref):
    h3 = h_ref[...] + _dot(y_ref[...], w_ref[...])
    h3_ref[...] = h3
    hn = _rms(h3, g_ref[...])
    hn_ref[...] = hn
    hi = hn.astype(BF16)
    lo = (hn - hi.astype(F32)).astype(BF16)
    logits = _dot(hi, rhi_ref[...]) + (_dot(lo, rhi_ref[...]) + _dot(hi, rlo_ref[...]))
    col = lax.broadcasted_iota(jnp.int32, logits.shape, 1)
    m1 = jnp.max(logits, axis=-1, keepdims=True)
    i1 = jnp.min(jnp.where(logits == m1, col, N_EXPERTS), axis=-1, keepdims=True)
    rest = jnp.where(col == i1, -jnp.inf, logits)
    m2 = jnp.max(rest, axis=-1, keepdims=True)
    i2 = jnp.min(jnp.where(rest == m2, col, N_EXPERTS), axis=-1, keepdims=True)
    e2 = jnp.exp(m2 - m1)
    inv = 1.0 / (1.0 + e2)
    idx_ref[:, 0:1] = i1
    idx_ref[:, 1:2] = i2
    wt_ref[:, 0:1] = inv
    wt_ref[:, 1:2] = e2 * inv


def _router(h2, y, w_fnet, g, r_hi, r_lo):
    t = h2.shape[0]
    tm = ROW_TM
    row = pl.BlockSpec((tm, D_MODEL), lambda i: (i, 0))
    pair = pl.BlockSpec((tm, TOP_K), lambda i: (i, 0))
    rtr = pl.BlockSpec((D_MODEL, N_EXPERTS), lambda i: (0, 0))
    return pl.pallas_call(
        _router_kernel,
        grid=(t // tm,),
        in_specs=[row, row, pl.BlockSpec((D_MODEL, D_MODEL), lambda i: (0, 0)),
                  pl.BlockSpec((1, D_MODEL), lambda i: (0, 0)), rtr, rtr],
        out_specs=[row, row, pair, pair],
        out_shape=[jax.ShapeDtypeStruct((t, D_MODEL), F32), jax.ShapeDtypeStruct((t, D_MODEL), F32),
                   jax.ShapeDtypeStruct((t, TOP_K), jnp.int32), jax.ShapeDtypeStruct((t, TOP_K), F32)],
        compiler_params=_params("arbitrary"),
        name="router",
    )(h2, y, w_fnet, g, r_hi, r_lo)


def _routing_tables(top_idx, top_w, tm):
    t = top_idx.shape[0]
    n_pairs = TOP_K * t
    n_tiles = n_pairs // tm + N_EXPERTS
    e_flat = top_idx.T.reshape(n_pairs)
    w_flat = top_w.T.reshape(n_pairs)
    order = jnp.argsort(e_flat, stable=True).astype(jnp.int32)
    experts = jnp.arange(N_EXPERTS, dtype=jnp.int32)
    counts = jnp.sum((e_flat[:, None] == experts[None, :]).astype(jnp.int32), axis=0)
    off = jnp.cumsum(counts) - counts
    tiles_e = (counts + tm - 1) // tm
    tile_end = jnp.cumsum(tiles_e)
    tile_off = tile_end - tiles_e
    n_used = tile_end[-1]
    last_e = jnp.max(jnp.where(tiles_e > 0, experts, 0))
    tile_ids = jnp.arange(n_tiles, dtype=jnp.int32)
    tile_e = jnp.minimum(jnp.sum((tile_ids[:, None] >= tile_end[None, :]).astype(jnp.int32), axis=1), last_e)
    u = jnp.arange(tm, dtype=jnp.int32)
    rank = (tile_ids - tile_off[tile_e])[:, None] * tm + u[None, :]
    valid = (tile_ids < n_used)[:, None] & (rank < counts[tile_e][:, None])
    src = order[jnp.clip(off[tile_e][:, None] + rank, 0, n_pairs - 1)]
    tok = jnp.where(valid, src % t, 0).astype(jnp.int32)
    spare = n_pairs + jnp.broadcast_to(u[None, :], (1, tm))
    dst = jnp.where(valid, src, spare).astype(jnp.int32)
    w_row = jnp.where(valid, w_flat[src], 0.0).astype(F32)
    tok_next = jnp.concatenate([tok[1:], jnp.zeros((2, tm), jnp.int32)], axis=0)
    dst_prev = jnp.concatenate([spare.astype(jnp.int32), dst], axis=0)
    tile_e = jnp.concatenate([tile_e, tile_e[-1:]]).astype(jnp.int32)
    return (tile_e, n_used.reshape(1).astype(jnp.int32), tok[0].reshape(1, 1, tm),
            tok_next.reshape(n_tiles + 1, 1, tm), dst_prev.reshape(n_tiles + 1, 1, tm),
            w_row.reshape(n_tiles * tm, 1))


def _moe_kernel(te_ref, nu_ref, tok0_ref, tokn_ref, dstp_ref, wrow_ref, hn_hbm, wg_ref, wu_ref, wd_ref,
                y2_hbm, xbuf, xb, acc, ybuf, gsem, ssem, *, n_ff):
    del te_ref
    i = pl.program_id(0)
    j = pl.program_id(1)
    n_used = nu_ref[0]
    tm = xbuf.shape[0]
    fc = wg_ref.shape[2]
    rows_per_step = tm // n_ff
    spare0 = y2_hbm.shape[0] - tm
    unroll = 8

    def gather_copy(tok, u):
        return pltpu.make_async_copy(hn_hbm.at[pl.ds(tok, 1), :], xbuf.at[pl.ds(u, 1), :], gsem)

    def scatter_copy(u, row):
        return pltpu.make_async_copy(ybuf.at[pl.ds(u, 1), :], y2_hbm.at[pl.ds(row, 1), :], ssem)

    def start_gather(ref):
        def body(u, carry):
            gather_copy(ref[0, 0, u], u).start()
            return carry
        lax.fori_loop(0, tm, body, 0, unroll=unroll)

    def wait_gather():
        pltpu.make_async_copy(hn_hbm.at[pl.ds(0, tm), :], xbuf, gsem).wait()

    def start_scatter_all():
        def body(u, carry):
            scatter_copy(u, dstp_ref[0, 0, u]).start()
            return carry
        lax.fori_loop(0, tm, body, 0, unroll=unroll)

    def wait_scatter():
        pltpu.make_async_copy(ybuf, y2_hbm.at[pl.ds(0, tm), :], ssem).wait()

    @pl.when((i == 0) & (j == 0))
    def _():
        ybuf[...] = jnp.zeros_like(ybuf)
        fill = pltpu.make_async_copy(ybuf, y2_hbm.at[pl.ds(spare0, tm), :], ssem)
        fill.start()
        fill.wait()
        start_gather(tok0_ref)

    @pl.when(i < n_used)
    def _():
        @pl.when(j == 0)
        def _():
            wait_gather()
            xb[...] = xbuf[...].astype(BF16)

        base = j * rows_per_step
        for r in range(rows_per_step):
            u = base + r
            gather_copy(tokn_ref[0, 0, u], u).start()
            scatter_copy(u, dstp_ref[0, 0, u]).start()

        x = xb[...]
        part = None
        for lo in range(0, fc, MOE_SUB):
            cols = slice(lo, min(lo + MOE_SUB, fc))
            hidden = (_silu(_dot(x, wg_ref[0, :, cols])) * _dot(x, wu_ref[0, :, cols])).astype(BF16)
            d = _dot(hidden, wd_ref[0, cols, :])
            part = d if part is None else part + d

        @pl.when(j == 0)
        def _():
            acc[...] = part

        @pl.when((j > 0) & (j < n_ff - 1))
        def _():
            acc[...] += part

        @pl.when(j == n_ff - 1)
        def _():
            wait_scatter()
            ybuf[...] = (acc[...] + part) * wrow_ref[...]

    @pl.when((i == n_used) & (j == 0))
    def _():
        start_scatter_all()
        wait_scatter()
        wait_gather()


def _moe(hn, tile_e, n_used, tok0, tok_next, dst_prev, w_row, w_gate, w_up, w_down):
    t = hn.shape[0]
    d_ff = w_gate.shape[2]
    tm, fc = MOE_TM, MOE_FC
    n_steps = tok_next.shape[0]
    n_ff = d_ff // fc
    assert n_ff >= 2 and n_ff * fc == d_ff and tm % n_ff == 0
    last_tile = n_steps - 2

    def ff_step(i, j, nu):
        return jnp.where(i < nu[0], j, n_ff - 1)

    grid_spec = pltpu.PrefetchScalarGridSpec(
        num_scalar_prefetch=2,
        grid=(n_steps, n_ff),
        in_specs=[
            pl.BlockSpec((1, 1, tm), lambda i, j, te, nu: (0, 0, 0), memory_space=pltpu.SMEM),
            pl.BlockSpec((1, 1, tm), lambda i, j, te, nu: (i, 0, 0), memory_space=pltpu.SMEM),
            pl.BlockSpec((1, 1, tm), lambda i, j, te, nu: (i, 0, 0), memory_space=pltpu.SMEM),
            pl.BlockSpec((tm, 1), lambda i, j, te, nu: (jnp.minimum(i, last_tile), 0)),
            pl.BlockSpec(memory_space=pl.ANY),
            pl.BlockSpec((1, D_MODEL, fc), lambda i, j, te, nu: (te[i], 0, ff_step(i, j, nu))),
            pl.BlockSpec((1, D_MODEL, fc), lambda i, j, te, nu: (te[i], 0, ff_step(i, j, nu))),
            pl.BlockSpec((1, fc, D_MODEL), lambda i, j, te, nu: (te[i], ff_step(i, j, nu), 0)),
        ],
        out_specs=pl.BlockSpec(memory_space=pl.ANY),
        scratch_shapes=[
            pltpu.VMEM((tm, D_MODEL), F32),
            pltpu.VMEM((tm, D_MODEL), BF16),
            pltpu.VMEM((tm, D_MODEL), F32),
            pltpu.VMEM((tm, D_MODEL), F32),
            pltpu.SemaphoreType.DMA(()),
            pltpu.SemaphoreType.DMA(()),
        ],
    )
    return pl.pallas_call(
        functools.partial(_moe_kernel, n_ff=n_ff),
        grid_spec=grid_spec,
        out_shape=jax.ShapeDtypeStruct((TOP_K * t + tm, D_MODEL), F32),
        compiler_params=_params("arbitrary", "arbitrary"),
        name="moe",
    )(tile_e, n_used, tok0, tok_next, dst_prev, w_row, hn, w_gate, w_up, w_down)


def _final_kernel(h_ref, ya_ref, yb_ref, g_ref, o_ref):
    o_ref[...] = _rms(h_ref[...] + (ya_ref[...] + yb_ref[...]), g_ref[...])


def _final(h3, y2, g):
    t = h3.shape[0]
    tm = ROW_TM
    second = t // tm
    row = pl.BlockSpec((tm, D_MODEL), lambda i: (i, 0))
    return pl.pallas_call(
        _final_kernel,
        grid=(t // tm,),
        in_specs=[row, row, pl.BlockSpec((tm, D_MODEL), lambda i: (i + second, 0)),
                  pl.BlockSpec((1, D_MODEL), lambda i: (0, 0))],
        out_specs=row,
        out_shape=jax.ShapeDtypeStruct((t, D_MODEL), F32),
        compiler_params=_params("arbitrary"),
        name="final_norm",
    )(h3, y2, y2, g)


def _rotary_tables(seq):
    half = RET_QK_DIM // 2
    inv = ROPE_BASE ** (-jnp.arange(half, dtype=F32) / half)
    ang = jnp.arange(seq, dtype=F32)[:, None] * inv[None, :]
    return jnp.cos(ang), jnp.sin(ang)


def _channel_dft_tables(n, scale):
    k = jnp.arange(n, dtype=jnp.int32)
    ang = ((k[:, None] * k[None, :]) % n).astype(F32) * (2.0 * math.pi / n)
    return (jnp.cos(ang) * scale).astype(BF16), (-jnp.sin(ang) * scale).astype(BF16)


def _sequence_dft_tables(seq, scale):
    blk = FNET_SEQ_BLOCK
    n1 = seq // blk
    k1 = jnp.arange(n1, dtype=jnp.int32)[:, None, None]
    k2 = jnp.arange(blk, dtype=jnp.int32)[None, :, None]
    s2 = jnp.arange(blk, dtype=jnp.int32)[None, None, :]
    ang = (((k1 + n1 * k2) * s2) % seq).astype(F32) * (2.0 * math.pi / seq)
    gc = (jnp.cos(ang) * scale).astype(BF16).reshape(n1 * blk, blk)
    gs = (jnp.sin(ang) * scale).astype(BF16).reshape(n1 * blk, blk)
    return gc, gs


def kernel(x, mix_norm, ffn_norm, ret_w_in, ret_decay_logit, ret_gn_gain, ret_w_out,
           dense_w_gate, dense_w_up, dense_w_down, fnet_w_out, moe_router,
           moe_w_gate, moe_w_up, moe_w_down, final_norm):
    batch, seq, d = x.shape
    t = batch * seq
    n1 = seq // FNET_SEQ_BLOCK
    assert d == D_MODEL and seq % RET_CHUNK == 0 and seq % PROJ_TM == 0
    assert n1 * FNET_SEQ_BLOCK == seq and n1 & (n1 - 1) == 0
    x2d = x.reshape(t, d)

    cos, sin = _rotary_tables(seq)
    proj = _proj(x2d, mix_norm[0:1], cos, sin, ret_w_in[0].astype(BF16), seq)
    log_gamma = jax.nn.log_sigmoid(ret_decay_logit[0].astype(F32))
    z = _retention(proj, log_gamma, ret_gn_gain[0:1], batch, seq)
    h2 = _ffn0(x2d, z, ret_w_out[0].astype(BF16), ffn_norm[0:1],
               dense_w_gate[0].astype(BF16), dense_w_up[0].astype(BF16), dense_w_down[0].astype(BF16))

    ch_scale = FNET_GROUP_DIM ** -0.5
    cc, sc = _channel_dft_tables(FNET_GROUP_DIM, ch_scale)
    gc, gs = _sequence_dft_tables(seq, seq ** -0.5)
    zr, zi = _fnet_channels(h2, mix_norm[1:2], cc, sc)
    ar, ai = _fnet_blockdft(zr, zi, batch, seq)
    y = _fnet_sequence(gc, gs, ar, ai, batch, seq).reshape(t, d)
    r_hi = moe_router[0].astype(BF16)
    r_lo = (moe_router[0] - r_hi.astype(F32)).astype(BF16)
    h3, hn3, top_idx, top_w = _router(h2, y, fnet_w_out[0].astype(BF16), ffn_norm[1:2], r_hi, r_lo)

    tile_e, n_used, tok0, tok_next, dst_prev, w_row = _routing_tables(top_idx, top_w, MOE_TM)
    y2 = _moe(hn3, tile_e, n_used, tok0, tok_next, dst_prev, w_row,
              moe_w_gate[0].astype(BF16), moe_w_up[0].astype(BF16), moe_w_down[0].astype(BF16))
    out = _final(h3, y2, final_norm.reshape(1, d))
    return out.reshape(batch, seq, d)
```

```python
import functools
import math

import jax
import jax.numpy as jnp
from jax import lax
from jax.experimental import pallas as pl
from jax.experimental.pallas import tpu as pltpu

F32 = jnp.float32
BF16 = jnp.bfloat16

D_MODEL = 1024
RET_HEADS = 4
RET_QK_DIM = D_MODEL // RET_HEADS
RET_V_DIM = 2 * RET_QK_DIM
RET_QK_WIDTH = RET_HEADS * RET_QK_DIM
RET_V_WIDTH = RET_HEADS * RET_V_DIM
RET_IN_WIDTH = 2 * RET_QK_WIDTH + 2 * RET_V_WIDTH
ROPE_BASE = 10000.0
FNET_GROUPS = 4
FNET_GROUP_DIM = D_MODEL // FNET_GROUPS
N_EXPERTS = 8
TOP_K = 2
EPS = 1e-6

RET_CHUNK = 256
VMEM_LIMIT_BYTES = 56 * 1024 * 1024

PROJ_TM, PROJ_TN = 1024, 2048
FFN0_TM, FFN0_FC = 512, 1408
ROW_TM = 1024
FNET_SEQ_BLOCK = 256
FNET_DT = 256
FNET_ROW_CHUNK = 16
MOE_TM, MOE_FC, MOE_SUB = 1024, 1792, 512


def _params(*semantics):
    return pltpu.CompilerParams(dimension_semantics=semantics, vmem_limit_bytes=VMEM_LIMIT_BYTES)


def _dot(a, b):
    return jnp.dot(a, b, preferred_element_type=F32)


def _rms(x, g):
    return x * lax.rsqrt(jnp.mean(x * x, axis=-1, keepdims=True) + EPS) * g


def _silu(a):
    return a * (1.0 / (1.0 + jnp.exp(-a)))


def _proj_kernel(x_ref, g_ref, cos_ref, sin_ref, w_ref, o_ref, hn_ref):
    j = pl.program_id(1)

    @pl.when(j == 0)
    def _():
        hn_ref[...] = _rms(x_ref[...], g_ref[...]).astype(BF16)

    tn = o_ref.shape[1]
    qk_tiles = 2 * RET_QK_WIDTH // tn
    v_tiles = RET_V_WIDTH // tn
    hn = hn_ref[...]
    half = RET_QK_DIM // 2

    def head(h):
        return _dot(hn, w_ref[:, h * RET_QK_DIM:(h + 1) * RET_QK_DIM])

    @pl.when(j < qk_tiles)
    def _():
        for h in range(tn // RET_QK_DIM):
            lo = h * RET_QK_DIM
            is_k = j * tn + lo >= RET_QK_WIDTH
            scale = jnp.where(is_k, RET_QK_DIM ** -0.5, 1.0).astype(F32)
            cos = cos_ref[...] * scale
            sin = sin_ref[...] * scale
            y = head(h)
            t1 = y[:, :half]
            t2 = y[:, half:]
            o_ref[:, lo:lo + half] = (t1 * cos - t2 * sin).astype(BF16)
            o_ref[:, lo + half:lo + RET_QK_DIM] = (t1 * sin + t2 * cos).astype(BF16)

    @pl.when((j >= qk_tiles) & (j < qk_tiles + v_tiles))
    def _():
        for h in range(tn // RET_QK_DIM):
            o_ref[:, h * RET_QK_DIM:(h + 1) * RET_QK_DIM] = head(h).astype(BF16)

    @pl.when(j >= qk_tiles + v_tiles)
    def _():
        for h in range(tn // RET_QK_DIM):
            o_ref[:, h * RET_QK_DIM:(h + 1) * RET_QK_DIM] = _silu(head(h)).astype(BF16)


def _proj(x2d, g, cos, sin, w_in, seq):
    t = x2d.shape[0]
    tm, tn = PROJ_TM, PROJ_TN
    pos_tiles = seq // tm
    return pl.pallas_call(
        _proj_kernel,
        grid=(t // tm, RET_IN_WIDTH // tn),
        in_specs=[
            pl.BlockSpec((tm, D_MODEL), lambda i, j: (i, 0)),
            pl.BlockSpec((1, D_MODEL), lambda i, j: (0, 0)),
            pl.BlockSpec((tm, RET_QK_DIM // 2), lambda i, j: (i % pos_tiles, 0)),
            pl.BlockSpec((tm, RET_QK_DIM // 2), lambda i, j: (i % pos_tiles, 0)),
            pl.BlockSpec((D_MODEL, tn), lambda i, j: (0, j)),
        ],
        out_specs=pl.BlockSpec((tm, tn), lambda i, j: (i, j)),
        out_shape=jax.ShapeDtypeStruct((t, RET_IN_WIDTH), BF16),
        scratch_shapes=[pltpu.VMEM((tm, D_MODEL), BF16)],
        compiler_params=_params("arbitrary", "arbitrary"),
        name="proj",
    )(x2d, g, cos, sin, w_in)


def _ret_kernel(lg_ref, q_ref, k_ref, v_ref, g_ref, gain_ref, o_ref, yb_ref, sf_ref, sb_ref):
    c = RET_CHUNK
    n_chunks = q_ref.shape[0] // c
    h = pl.program_id(1)
    lgf = lg_ref[0, h]
    lgb = lg_ref[1, h]

    row = lax.broadcasted_iota(jnp.int32, (c, RET_QK_DIM), 0).astype(F32)
    xi_f = jnp.exp((row + 1.0) * lgf).astype(BF16)
    zeta_f = jnp.exp((c - 1.0 - row) * lgf).astype(BF16)
    xi_b = jnp.exp((c - row) * lgb).astype(BF16)
    zeta_b = jnp.exp(row * lgb).astype(BF16)
    g_f = jnp.exp(jnp.full((1, RET_V_DIM), c, F32) * lgf)
    g_b = jnp.exp(jnp.full((1, RET_V_DIM), c, F32) * lgb)
    di = lax.broadcasted_iota(jnp.int32, (c, c), 0)
    dj = lax.broadcasted_iota(jnp.int32, (c, c), 1)
    diff = (di - dj).astype(F32)
    decay = jnp.exp(jnp.where(diff >= 0, diff * lgf, -diff * lgb))

    def chunk(n):
        r0 = pl.multiple_of(n * c, c)
        return pl.ds(r0, c)

    def scaled(t, s):
        return t * s

    def outer(kz, vn):
        return lax.dot_general(kz, vn, (((0,), (0,)), ((), ())), preferred_element_type=F32)

    sb_ref[...] = jnp.zeros_like(sb_ref)

    def bwd(step, carry):
        rows = chunk(n_chunks - 1 - step)
        qn, kn, vn = q_ref[rows, :], k_ref[rows, :], v_ref[rows, :]
        sb = sb_ref[...]
        yb_ref[rows, :] = _dot(scaled(qn, xi_b), sb.astype(BF16))
        sb_ref[...] = sb * g_b + outer(scaled(kn, zeta_b), vn)
        return carry

    lax.fori_loop(0, n_chunks, bwd, 0)

    sf_ref[...] = jnp.zeros_like(sf_ref)
    gain = gain_ref[...]

    def fwd(n, carry):
        rows = chunk(n)
        qn, kn, vn = q_ref[rows, :], k_ref[rows, :], v_ref[rows, :]
        sf = sf_ref[...]
        scores = lax.dot_general(qn, kn, (((1,), (1,)), ((), ())), preferred_element_type=F32) * decay
        y = _dot(scores.astype(BF16), vn) + _dot(scaled(qn, xi_f), sf.astype(BF16)) + yb_ref[rows, :]
        sf_ref[...] = sf * g_f + outer(scaled(kn, zeta_f), vn)
        mu = jnp.mean(y, axis=-1, keepdims=True)
        yc = y - mu
        var = jnp.mean(yc * yc, axis=-1, keepdims=True)
        yn = yc * lax.rsqrt(var + EPS) * gain
        o_ref[rows, :] = g_ref[rows, :] * yn.astype(BF16)
        return carry

    lax.fori_loop(0, n_chunks, fwd, 0)


def _retention(proj, log_gamma, gain, batch, seq):
    t = proj.shape[0]
    qk_blocks = RET_QK_WIDTH // RET_QK_DIM
    k_off = qk_blocks
    v_off = 2 * RET_QK_WIDTH // RET_V_DIM
    g_off = v_off + RET_V_WIDTH // RET_V_DIM
    return pl.pallas_call(
        _ret_kernel,
        grid=(batch, RET_HEADS),
        in_specs=[
            pl.BlockSpec(memory_space=pltpu.SMEM),
            pl.BlockSpec((seq, RET_QK_DIM), lambda b, h: (b, h)),
            pl.BlockSpec((seq, RET_QK_DIM), lambda b, h: (b, k_off + h)),
            pl.BlockSpec((seq, RET_V_DIM), lambda b, h: (b, v_off + h)),
            pl.BlockSpec((seq, RET_V_DIM), lambda b, h: (b, g_off + h)),
            pl.BlockSpec((1, RET_V_DIM), lambda b, h: (0, h)),
        ],
        out_specs=pl.BlockSpec((seq, RET_V_DIM), lambda b, h: (b, h)),
        out_shape=jax.ShapeDtypeStruct((t, RET_V_WIDTH), BF16),
        scratch_shapes=[
            pltpu.VMEM((seq, RET_V_DIM), F32),
            pltpu.VMEM((RET_QK_DIM, RET_V_DIM), F32),
            pltpu.VMEM((RET_QK_DIM, RET_V_DIM), F32),
        ],
        compiler_params=_params("arbitrary", "arbitrary"),
        name="retention",
    )(log_gamma, proj, proj, proj, proj, gain)


def _ffn0_kernel(x_ref, z_ref, wo_ref, g_ref, wg_ref, wu_ref, wd_ref, o_ref, h1_ref, hn_ref, acc_ref):
    j = pl.program_id(1)

    @pl.when(j == 0)
    def _():
        h1 = x_ref[...] + _dot(z_ref[...], wo_ref[...])
        h1_ref[...] = h1
        hn_ref[...] = _rms(h1, g_ref[...]).astype(BF16)

    hn = hn_ref[...]
    hidden = (_silu(_dot(hn, wg_ref[...])) * _dot(hn, wu_ref[...])).astype(BF16)
    part = _dot(hidden, wd_ref[...])

    @pl.when(j == 0)
    def _():
        acc_ref[...] = part

    @pl.when(j > 0)
    def _():
        acc_ref[...] += part

    @pl.when(j == pl.num_programs(1) - 1)
    def _():
        o_ref[...] = h1_ref[...] + acc_ref[...]


def _ffn0(x2d, z, w_out, g, w_gate, w_up, w_down):
    t = x2d.shape[0]
    d_ff = w_gate.shape[1]
    tm, fc = FFN0_TM, FFN0_FC
    return pl.pallas_call(
        _ffn0_kernel,
        grid=(t // tm, d_ff // fc),
        in_specs=[
            pl.BlockSpec((tm, D_MODEL), lambda i, j: (i, 0)),
            pl.BlockSpec((tm, RET_V_WIDTH), lambda i, j: (i, 0)),
            pl.BlockSpec((RET_V_WIDTH, D_MODEL), lambda i, j: (0, 0)),
            pl.BlockSpec((1, D_MODEL), lambda i, j: (0, 0)),
            pl.BlockSpec((D_MODEL, fc), lambda i, j: (0, j)),
            pl.BlockSpec((D_MODEL, fc), lambda i, j: (0, j)),
            pl.BlockSpec((fc, D_MODEL), lambda i, j: (j, 0)),
        ],
        out_specs=pl.BlockSpec((tm, D_MODEL), lambda i, j: (i, 0)),
        out_shape=jax.ShapeDtypeStruct((t, D_MODEL), F32),
        scratch_shapes=[
            pltpu.VMEM((tm, D_MODEL), F32),
            pltpu.VMEM((tm, D_MODEL), BF16),
            pltpu.VMEM((tm, D_MODEL), F32),
        ],
        compiler_params=_params("arbitrary", "arbitrary"),
        name="ffn0",
    )(x2d, z, w_out, g, w_gate, w_up, w_down)


def _fnet_ch_kernel(h_ref, g_ref, cc_ref, sc_ref, xc_ref, xs_ref):
    hn = _rms(h_ref[...], g_ref[...]).astype(BF16)
    cc = cc_ref[...]
    sc = sc_ref[...]
    for grp in range(FNET_GROUPS):
        cols = slice(grp * FNET_GROUP_DIM, (grp + 1) * FNET_GROUP_DIM)
        xc_ref[:, cols] = _dot(hn[:, cols], cc).astype(BF16)
        xs_ref[:, cols] = _dot(hn[:, cols], sc).astype(BF16)


def _fnet_channels(h, g, cc, sc):
    t = h.shape[0]
    tm = ROW_TM
    row = pl.BlockSpec((tm, D_MODEL), lambda i: (i, 0))
    tab = pl.BlockSpec((FNET_GROUP_DIM, FNET_GROUP_DIM), lambda i: (0, 0))
    return pl.pallas_call(
        _fnet_ch_kernel,
        grid=(t // tm,),
        in_specs=[row, pl.BlockSpec((1, D_MODEL), lambda i: (0, 0)), tab, tab],
        out_specs=[row, row],
        out_shape=[jax.ShapeDtypeStruct((t, D_MODEL), BF16)] * 2,
        compiler_params=_params("arbitrary"),
        name="fnet_channels",
    )(h, g, cc, sc)


def _dft_across_blocks(xs):
    n = len(xs)
    if n == 1:
        return xs
    even = _dft_across_blocks(xs[0::2])
    odd = _dft_across_blocks(xs[1::2])
    out = [None] * n
    for k in range(n // 2):
        er, ei = even[k]
        o_r, o_i = odd[k]
        if k == 0:
            tr, ti = o_r, o_i
            out[k], out[k + n // 2] = (er + tr, ei + ti), (er - tr, ei - ti)
        elif 4 * k == n:
            out[k], out[k + n // 2] = (er + o_i, ei - o_r), (er - o_i, ei + o_r)
        else:
            c, s = math.cos(2.0 * math.pi * k / n), math.sin(2.0 * math.pi * k / n)
            tr = o_r * c + o_i * s
            ti = o_i * c - o_r * s
            out[k], out[k + n // 2] = (er + tr, ei + ti), (er - tr, ei - ti)
    return out


def _fnet_seq_kernel(zr_ref, zi_ref, gc_ref, gs_ref, o_ref, ar_ref, ai_ref, ys_ref):
    blk = FNET_SEQ_BLOCK
    n1 = zr_ref.shape[0] // blk
    rc = FNET_ROW_CHUNK
    lanes = 128

    def body(r, carry):
        r0 = pl.multiple_of(r * rc, rc)
        for l in range(zr_ref.shape[1] // lanes):
            cols = slice(l * lanes, (l + 1) * lanes)
            zs = [(zr_ref[pl.ds(s1 * blk + r0, rc), cols].astype(F32),
                   zi_ref[pl.ds(s1 * blk + r0, rc), cols].astype(F32)) for s1 in range(n1)]
            for k1, (a_r, a_i) in enumerate(_dft_across_blocks(zs)):
                ar_ref[pl.ds(k1 * blk + r0, rc), cols] = a_r.astype(BF16)
                ai_ref[pl.ds(k1 * blk + r0, rc), cols] = a_i.astype(BF16)
        return carry

    lax.fori_loop(0, blk // rc, body, 0)

    for k1 in range(n1):
        rows = slice(k1 * blk, (k1 + 1) * blk)
        y = _dot(gc_ref[rows, :], ar_ref[rows, :]) + _dot(gs_ref[rows, :], ai_ref[rows, :])
        for l in range(y.shape[1] // lanes):
            ys_ref[l, pl.ds(k1, blk, stride=n1), :] = y[:, l * lanes:(l + 1) * lanes]
    for l in range(o_ref.shape[1] // lanes):
        o_ref[:, l * lanes:(l + 1) * lanes] = ys_ref[l]


def _fnet_sequence(zr, zi, gc, gs, batch, seq):
    t = zr.shape[0]
    dt = FNET_DT
    blkspec = pl.BlockSpec((seq, dt), lambda b, c: (b, c))
    tab = pl.BlockSpec((seq, FNET_SEQ_BLOCK), lambda b, c: (0, 0))
    return pl.pallas_call(
        _fnet_seq_kernel,
        grid=(batch, D_MODEL // dt),
        in_specs=[blkspec, blkspec, tab, tab],
        out_specs=blkspec,
        out_shape=jax.ShapeDtypeStruct((t, D_MODEL), F32),
        scratch_shapes=[pltpu.VMEM((seq, dt), BF16), pltpu.VMEM((seq, dt), BF16),
                        pltpu.VMEM((dt // 128, seq, 128), F32)],
        compiler_params=_params("arbitrary", "arbitrary"),
        name="fnet_sequence",
    )(zr, zi, gc, gs)


def _router_kernel(h_ref, y_ref, w_ref, g_ref, rhi_ref, rlo_ref, h3_ref, hn_ref, idx_ref, wt_ref):
    h3 = h_ref[...] + _dot(y_ref[...].astype(BF16), w_ref[...])
    h3_ref[...] = h3
    hn = _rms(h3, g_ref[...])
    hn_ref[...] = hn
    hi = hn.astype(BF16)
    lo = (hn - hi.astype(F32)).astype(BF16)
    logits = _dot(hi, rhi_ref[...]) + (_dot(lo, rhi_ref[...]) + _dot(hi, rlo_ref[...]))
    col = lax.broadcasted_iota(jnp.int32, logits.shape, 1)
    m1 = jnp.max(logits, axis=-1, keepdims=True)
    i1 = jnp.min(jnp.where(logits == m1, col, N_EXPERTS), axis=-1, keepdims=True)
    rest = jnp.where(col == i1, -jnp.inf, logits)
    m2 = jnp.max(rest, axis=-1, keepdims=True)
    i2 = jnp.min(jnp.where(rest == m2, col, N_EXPERTS), axis=-1, keepdims=True)
    e2 = jnp.exp(m2 - m1)
    inv = 1.0 / (1.0 + e2)
    idx_ref[:, 0:1] = i1
    idx_ref[:, 1:2] = i2
    wt_ref[:, 0:1] = inv
    wt_ref[:, 1:2] = e2 * inv


def _router(h2, y, w_fnet, g, r_hi, r_lo):
    t = h2.shape[0]
    tm = ROW_TM
    row = pl.BlockSpec((tm, D_MODEL), lambda i: (i, 0))
    pair = pl.BlockSpec((tm, TOP_K), lambda i: (i, 0))
    rtr = pl.BlockSpec((D_MODEL, N_EXPERTS), lambda i: (0, 0))
    return pl.pallas_call(
        _router_kernel,
        grid=(t // tm,),
        in_specs=[row, row, pl.BlockSpec((D_MODEL, D_MODEL), lambda i: (0, 0)),
                  pl.BlockSpec((1, D_MODEL), lambda i: (0, 0)), rtr, rtr],
        out_specs=[row, row, pair, pair],
        out_shape=[jax.ShapeDtypeStruct((t, D_MODEL), F32), jax.ShapeDtypeStruct((t, D_MODEL), F32),
                   jax.ShapeDtypeStruct((t, TOP_K), jnp.int32), jax.ShapeDtypeStruct((t, TOP_K), F32)],
        compiler_params=_params("arbitrary"),
        name="router",
    )(h2, y, w_fnet, g, r_hi, r_lo)


def _routing_tables(top_idx, top_w, tm):
    t = top_idx.shape[0]
    n_pairs = TOP_K * t
    n_tiles = n_pairs // tm + N_EXPERTS
    e_flat = top_idx.T.reshape(n_pairs)
    w_flat = top_w.T.reshape(n_pairs)
    order = jnp.argsort(e_flat, stable=True).astype(jnp.int32)
    experts = jnp.arange(N_EXPERTS, dtype=jnp.int32)
    counts = jnp.sum((e_flat[:, None] == experts[None, :]).astype(jnp.int32), axis=0)
    off = jnp.cumsum(counts) - counts
    tiles_e = (counts + tm - 1) // tm
    tile_end = jnp.cumsum(tiles_e)
    tile_off = tile_end - tiles_e
    n_used = tile_end[-1]
    last_e = jnp.max(jnp.where(tiles_e > 0, experts, 0))
    tile_ids = jnp.arange(n_tiles, dtype=jnp.int32)
    tile_e = jnp.minimum(jnp.sum((tile_ids[:, None] >= tile_end[None, :]).astype(jnp.int32), axis=1), last_e)
    u = jnp.arange(tm, dtype=jnp.int32)
    rank = (tile_ids - tile_off[tile_e])[:, None] * tm + u[None, :]
    valid = (tile_ids < n_used)[:, None] & (rank < counts[tile_e][:, None])
    src = order[jnp.clip(off[tile_e][:, None] + rank, 0, n_pairs - 1)]
    tok = jnp.where(valid, src % t, 0).astype(jnp.int32)
    spare = n_pairs + jnp.broadcast_to(u[None, :], (1, tm))
    dst = jnp.where(valid, src, spare).astype(jnp.int32)
    w_row = jnp.where(valid, w_flat[src], 0.0).astype(F32)
    tok_next = jnp.concatenate([tok[1:], jnp.zeros((2, tm), jnp.int32)], axis=0)
    dst_prev = jnp.concatenate([spare.astype(jnp.int32), dst], axis=0)
    tile_e = jnp.concatenate([tile_e, tile_e[-1:]]).astype(jnp.int32)
    return (tile_e, n_used.reshape(1).astype(jnp.int32), tok[0].reshape(1, 1, tm),
            tok_next.reshape(n_tiles + 1, 1, tm), dst_prev.reshape(n_tiles + 1, 1, tm),
            w_row.reshape(n_tiles * tm, 1))


def _moe_kernel(te_ref, nu_ref, tok0_ref, tokn_ref, dstp_ref, wrow_ref, hn_hbm, wg_ref, wu_ref, wd_ref,
                y2_hbm, xbuf, xb, acc, ybuf, gsem, ssem, *, n_ff):
    del te_ref
    i = pl.program_id(0)
    j = pl.program_id(1)
    n_used = nu_ref[0]
    tm = xbuf.shape[0]
    fc = wg_ref.shape[2]
    rows_per_step = tm // n_ff
    spare0 = y2_hbm.shape[0] - tm
    unroll = 8

    def gather_copy(tok, u):
        return pltpu.make_async_copy(hn_hbm.at[pl.ds(tok, 1), :], xbuf.at[pl.ds(u, 1), :], gsem)

    def scatter_copy(u, row):
        return pltpu.make_async_copy(ybuf.at[pl.ds(u, 1), :], y2_hbm.at[pl.ds(row, 1), :], ssem)

    def start_gather(ref):
        def body(u, carry):
            gather_copy(ref[0, 0, u], u).start()
            return carry
        lax.fori_loop(0, tm, body, 0, unroll=unroll)

    def wait_gather():
        pltpu.make_async_copy(hn_hbm.at[pl.ds(0, tm), :], xbuf, gsem).wait()

    def start_scatter_all():
        def body(u, carry):
            scatter_copy(u, dstp_ref[0, 0, u]).start()
            return carry
        lax.fori_loop(0, tm, body, 0, unroll=unroll)

    def wait_scatter():
        pltpu.make_async_copy(ybuf, y2_hbm.at[pl.ds(0, tm), :], ssem).wait()

    @pl.when((i == 0) & (j == 0))
    def _():
        ybuf[...] = jnp.zeros_like(ybuf)
        fill = pltpu.make_async_copy(ybuf, y2_hbm.at[pl.ds(spare0, tm), :], ssem)
        fill.start()
        fill.wait()
        start_gather(tok0_ref)

    def ff_step(jj):
        if jj == 0:
            wait_gather()
            xb[...] = xbuf[...].astype(BF16)

        for u in range(jj * rows_per_step, (jj + 1) * rows_per_step):
            gather_copy(tokn_ref[0, 0, u], u).start()
            scatter_copy(u, dstp_ref[0, 0, u]).start()

        x = xb[...]
        part = None
        for lo in range(0, fc, MOE_SUB):
            cols = slice(lo, min(lo + MOE_SUB, fc))
            hidden = (_silu(_dot(x, wg_ref[0, :, cols])) * _dot(x, wu_ref[0, :, cols])).astype(BF16)
            d = _dot(hidden, wd_ref[0, cols, :])
            part = d if part is None else part + d

        if jj == 0:
            acc[...] = part
        elif jj < n_ff - 1:
            acc[...] += part
        else:
            wait_scatter()
            ybuf[...] = (acc[...] + part) * wrow_ref[...]

    for jj in range(n_ff):
        pl.when((i < n_used) & (j == jj))(functools.partial(ff_step, jj))

    @pl.when((i == n_used) & (j == 0))
    def _():
        start_scatter_all()
        wait_scatter()
        wait_gather()


def _moe(hn, tile_e, n_used, tok0, tok_next, dst_prev, w_row, w_gate, w_up, w_down):
    t = hn.shape[0]
    d_ff = w_gate.shape[2]
    tm, fc = MOE_TM, MOE_FC
    n_steps = tok_next.shape[0]
    n_ff = d_ff // fc
    assert n_ff >= 2 and n_ff * fc == d_ff and tm % n_ff == 0
    last_tile = n_steps - 2

    def ff_step(i, j, nu):
        return jnp.where(i < nu[0], j, n_ff - 1)

    grid_spec = pltpu.PrefetchScalarGridSpec(
        num_scalar_prefetch=2,
        grid=(n_steps, n_ff),
        in_specs=[
            pl.BlockSpec((1, 1, tm), lambda i, j, te, nu: (0, 0, 0), memory_space=pltpu.SMEM),
            pl.BlockSpec((1, 1, tm), lambda i, j, te, nu: (i, 0, 0), memory_space=pltpu.SMEM),
            pl.BlockSpec((1, 1, tm), lambda i, j, te, nu: (i, 0, 0), memory_space=pltpu.SMEM),
            pl.BlockSpec((tm, 1), lambda i, j, te, nu: (jnp.minimum(i, last_tile), 0)),
            pl.BlockSpec(memory_space=pl.ANY),
            pl.BlockSpec((1, D_MODEL, fc), lambda i, j, te, nu: (te[i], 0, ff_step(i, j, nu))),
            pl.BlockSpec((1, D_MODEL, fc), lambda i, j, te, nu: (te[i], 0, ff_step(i, j, nu))),
            pl.BlockSpec((1, fc, D_MODEL), lambda i, j, te, nu: (te[i], ff_step(i, j, nu), 0)),
        ],
        out_specs=pl.BlockSpec(memory_space=pl.ANY),
        scratch_shapes=[
            pltpu.VMEM((tm, D_MODEL), F32),
            pltpu.VMEM((tm, D_MODEL), BF16),
            pltpu.VMEM((tm, D_MODEL), F32),
            pltpu.VMEM((tm, D_MODEL), F32),
            pltpu.SemaphoreType.DMA(()),
            pltpu.SemaphoreType.DMA(()),
        ],
    )
    return pl.pallas_call(
        functools.partial(_moe_kernel, n_ff=n_ff),
        grid_spec=grid_spec,
        out_shape=jax.ShapeDtypeStruct((TOP_K * t + tm, D_MODEL), F32),
        compiler_params=_params("arbitrary", "arbitrary"),
        name="moe",
    )(tile_e, n_used, tok0, tok_next, dst_prev, w_row, hn, w_gate, w_up, w_down)


def _final_kernel(h_ref, ya_ref, yb_ref, g_ref, o_ref):
    o_ref[...] = _rms(h_ref[...] + (ya_ref[...] + yb_ref[...]), g_ref[...])


def _final(h3, y2, g):
    t = h3.shape[0]
    tm = ROW_TM
    second = t // tm
    row = pl.BlockSpec((tm, D_MODEL), lambda i: (i, 0))
    return pl.pallas_call(
        _final_kernel,
        grid=(t // tm,),
        in_specs=[row, row, pl.BlockSpec((tm, D_MODEL), lambda i: (i + second, 0)),
                  pl.BlockSpec((1, D_MODEL), lambda i: (0, 0))],
        out_specs=row,
        out_shape=jax.ShapeDtypeStruct((t, D_MODEL), F32),
        compiler_params=_params("arbitrary"),
        name="final_norm",
    )(h3, y2, y2, g)


def _rotary_tables(seq):
    half = RET_QK_DIM // 2
    inv = ROPE_BASE ** (-jnp.arange(half, dtype=F32) / half)
    ang = jnp.arange(seq, dtype=F32)[:, None] * inv[None, :]
    return jnp.cos(ang), jnp.sin(ang)


def _channel_dft_tables(n, scale):
    k = jnp.arange(n, dtype=jnp.int32)
    ang = ((k[:, None] * k[None, :]) % n).astype(F32) * (2.0 * math.pi / n)
    return (jnp.cos(ang) * scale).astype(BF16), (-jnp.sin(ang) * scale).astype(BF16)


def _sequence_dft_tables(seq, scale):
    blk = FNET_SEQ_BLOCK
    n1 = seq // blk
    k1 = jnp.arange(n1, dtype=jnp.int32)[:, None, None]
    k2 = jnp.arange(blk, dtype=jnp.int32)[None, :, None]
    s2 = jnp.arange(blk, dtype=jnp.int32)[None, None, :]
    ang = (((k1 + n1 * k2) * s2) % seq).astype(F32) * (2.0 * math.pi / seq)
    gc = (jnp.cos(ang) * scale).astype(BF16).reshape(n1 * blk, blk)
    gs = (jnp.sin(ang) * scale).astype(BF16).reshape(n1 * blk, blk)
    return gc, gs


def kernel(x, mix_norm, ffn_norm, ret_w_in, ret_decay_logit, ret_gn_gain, ret_w_out,
           dense_w_gate, dense_w_up, dense_w_down, fnet_w_out, moe_router,
           moe_w_gate, moe_w_up, moe_w_down, final_norm):
    batch, seq, d = x.shape
    t = batch * seq
    n1 = seq // FNET_SEQ_BLOCK
    assert d == D_MODEL and seq % RET_CHUNK == 0 and seq % PROJ_TM == 0
    assert n1 * FNET_SEQ_BLOCK == seq and n1 & (n1 - 1) == 0
    x2d = x.reshape(t, d)

    cos, sin = _rotary_tables(seq)
    proj = _proj(x2d, mix_norm[0:1], cos, sin, ret_w_in[0].astype(BF16), seq)
    log_gamma = jax.nn.log_sigmoid(ret_decay_logit[0].astype(F32))
    z = _retention(proj, log_gamma, ret_gn_gain[0:1], batch, seq)
    h2 = _ffn0(x2d, z, ret_w_out[0].astype(BF16), ffn_norm[0:1],
               dense_w_gate[0].astype(BF16), dense_w_up[0].astype(BF16), dense_w_down[0].astype(BF16))

    ch_scale = FNET_GROUP_DIM ** -0.5
    cc, sc = _channel_dft_tables(FNET_GROUP_DIM, ch_scale)
    gc, gs = _sequence_dft_tables(seq, seq ** -0.5)
    zr, zi = _fnet_channels(h2, mix_norm[1:2], cc, sc)
    y = _fnet_sequence(zr, zi, gc, gs, batch, seq)
    r_hi = moe_router[0].astype(BF16)
    r_lo = (moe_router[0] - r_hi.astype(F32)).astype(BF16)
    h3, hn3, top_idx, top_w = _router(h2, y, fnet_w_out[0].astype(BF16), ffn_norm[1:2], r_hi, r_lo)

    tile_e, n_used, tok0, tok_next, dst_prev, w_row = _routing_tables(top_idx, top_w, MOE_TM)
    y2 = _moe(hn3, tile_e, n_used, tok0, tok_next, dst_prev, w_row,
              moe_w_gate[0].astype(BF16), moe_w_up[0].astype(BF16), moe_w_down[0].astype(BF16))
    out = _final(h3, y2, final_norm.reshape(1, d))
    return out.reshape(batch, seq, d)
```

```python
import functools
import math

import jax
import jax.numpy as jnp
from jax import lax
from jax.experimental import pallas as pl
from jax.experimental.pallas import tpu as pltpu

F32 = jnp.float32
BF16 = jnp.bfloat16

D_MODEL = 1024
RET_HEADS = 4
RET_QK_DIM = D_MODEL // RET_HEADS
RET_V_DIM = 2 * RET_QK_DIM
RET_QK_WIDTH = RET_HEADS * RET_QK_DIM
RET_V_WIDTH = RET_HEADS * RET_V_DIM
RET_IN_WIDTH = 2 * RET_QK_WIDTH + 2 * RET_V_WIDTH
ROPE_BASE = 10000.0
FNET_GROUPS = 4
FNET_GROUP_DIM = D_MODEL // FNET_GROUPS
N_EXPERTS = 8
TOP_K = 2
EPS = 1e-6

RET_CHUNK = 256
VMEM_LIMIT_BYTES = 56 * 1024 * 1024

PROJ_TM, PROJ_TN = 1024, 2048
FFN0_TM, FFN0_FC = 512, 1408
ROW_TM = 1024
FNET_SEQ_BLOCK = 256
FNET_DT = 256
FNET_ROW_CHUNK = 16
MOE_TM, MOE_FC, MOE_SUB = 1024, 1792, 512


def _params(*semantics):
    return pltpu.CompilerParams(dimension_semantics=semantics, vmem_limit_bytes=VMEM_LIMIT_BYTES)


def _dot(a, b):
    return jnp.dot(a, b, preferred_element_type=F32)


def _rms(x, g):
    return x * lax.rsqrt(jnp.mean(x * x, axis=-1, keepdims=True) + EPS) * g


def _silu(a):
    return a * (1.0 / (1.0 + jnp.exp(-a)))


def _proj_kernel(x_ref, g_ref, cos_ref, sin_ref, w_ref, o_ref, hn_ref):
    j = pl.program_id(1)

    @pl.when(j == 0)
    def _():
        hn_ref[...] = _rms(x_ref[...], g_ref[...]).astype(BF16)

    tn = o_ref.shape[1]
    qk_tiles = 2 * RET_QK_WIDTH // tn
    v_tiles = RET_V_WIDTH // tn
    hn = hn_ref[...]
    half = RET_QK_DIM // 2

    def head(h):
        return _dot(hn, w_ref[:, h * RET_QK_DIM:(h + 1) * RET_QK_DIM])

    @pl.when(j < qk_tiles)
    def _():
        for h in range(tn // RET_QK_DIM):
            lo = h * RET_QK_DIM
            is_k = j * tn + lo >= RET_QK_WIDTH
            scale = jnp.where(is_k, RET_QK_DIM ** -0.5, 1.0).astype(F32)
            cos = cos_ref[...] * scale
            sin = sin_ref[...] * scale
            y = head(h)
            t1 = y[:, :half]
            t2 = y[:, half:]
            o_ref[:, lo:lo + half] = (t1 * cos - t2 * sin).astype(BF16)
            o_ref[:, lo + half:lo + RET_QK_DIM] = (t1 * sin + t2 * cos).astype(BF16)

    @pl.when((j >= qk_tiles) & (j < qk_tiles + v_tiles))
    def _():
        for h in range(tn // RET_QK_DIM):
            o_ref[:, h * RET_QK_DIM:(h + 1) * RET_QK_DIM] = head(h).astype(BF16)

    @pl.when(j >= qk_tiles + v_tiles)
    def _():
        for h in range(tn // RET_QK_DIM):
            o_ref[:, h * RET_QK_DIM:(h + 1) * RET_QK_DIM] = _silu(head(h)).astype(BF16)


def _proj(x2d, g, cos, sin, w_in, seq):
    t = x2d.shape[0]
    tm, tn = PROJ_TM, PROJ_TN
    pos_tiles = seq // tm
    return pl.pallas_call(
        _proj_kernel,
        grid=(t // tm, RET_IN_WIDTH // tn),
        in_specs=[
            pl.BlockSpec((tm, D_MODEL), lambda i, j: (i, 0)),
            pl.BlockSpec((1, D_MODEL), lambda i, j: (0, 0)),
            pl.BlockSpec((tm, RET_QK_DIM // 2), lambda i, j: (i % pos_tiles, 0)),
            pl.BlockSpec((tm, RET_QK_DIM // 2), lambda i, j: (i % pos_tiles, 0)),
            pl.BlockSpec((D_MODEL, tn), lambda i, j: (0, j)),
        ],
        out_specs=pl.BlockSpec((tm, tn), lambda i, j: (i, j)),
        out_shape=jax.ShapeDtypeStruct((t, RET_IN_WIDTH), BF16),
        scratch_shapes=[pltpu.VMEM((tm, D_MODEL), BF16)],
        compiler_params=_params("arbitrary", "arbitrary"),
        name="proj",
    )(x2d, g, cos, sin, w_in)


def _ret_kernel(lg_ref, q_ref, k_ref, v_ref, g_ref, gain_ref, o_ref, yb_ref, sf_ref, sb_ref):
    c = RET_CHUNK
    n_chunks = q_ref.shape[0] // c
    h = pl.program_id(1)
    lgf = lg_ref[0, h]
    lgb = lg_ref[1, h]

    row = lax.broadcasted_iota(jnp.int32, (c, RET_QK_DIM), 0).astype(F32)
    xi_f = jnp.exp((row + 1.0) * lgf).astype(BF16)
    zeta_f = jnp.exp((c - 1.0 - row) * lgf).astype(BF16)
    xi_b = jnp.exp((c - row) * lgb).astype(BF16)
    zeta_b = jnp.exp(row * lgb).astype(BF16)
    g_f = jnp.exp(jnp.full((1, RET_V_DIM), c, F32) * lgf)
    g_b = jnp.exp(jnp.full((1, RET_V_DIM), c, F32) * lgb)
    di = lax.broadcasted_iota(jnp.int32, (c, c), 0)
    dj = lax.broadcasted_iota(jnp.int32, (c, c), 1)
    diff = (di - dj).astype(F32)
    decay = jnp.exp(jnp.where(diff >= 0, diff * lgf, -diff * lgb))

    def chunk(n):
        r0 = pl.multiple_of(n * c, c)
        return pl.ds(r0, c)

    def scaled(t, s):
        return t * s

    def outer(kz, vn):
        return lax.dot_general(kz, vn, (((0,), (0,)), ((), ())), preferred_element_type=F32)

    sb_ref[...] = jnp.zeros_like(sb_ref)

    def bwd(step, carry):
        rows = chunk(n_chunks - 1 - step)
        qn, kn, vn = q_ref[rows, :], k_ref[rows, :], v_ref[rows, :]
        sb = sb_ref[...]
        yb_ref[rows, :] = _dot(scaled(qn, xi_b), sb.astype(BF16))
        sb_ref[...] = sb * g_b + outer(scaled(kn, zeta_b), vn)
        return carry

    lax.fori_loop(0, n_chunks, bwd, 0)

    sf_ref[...] = jnp.zeros_like(sf_ref)
    gain = gain_ref[...]

    def fwd(n, carry):
        rows = chunk(n)
        qn, kn, vn = q_ref[rows, :], k_ref[rows, :], v_ref[rows, :]
        sf = sf_ref[...]
        scores = lax.dot_general(qn, kn, (((1,), (1,)), ((), ())), preferred_element_type=F32) * decay
        y = _dot(scores.astype(BF16), vn) + _dot(scaled(qn, xi_f), sf.astype(BF16)) + yb_ref[rows, :]
        sf_ref[...] = sf * g_f + outer(scaled(kn, zeta_f), vn)
        mu = jnp.mean(y, axis=-1, keepdims=True)
        yc = y - mu
        var = jnp.mean(yc * yc, axis=-1, keepdims=True)
        yn = yc * lax.rsqrt(var + EPS) * gain
        o_ref[rows, :] = g_ref[rows, :] * yn.astype(BF16)
        return carry

    lax.fori_loop(0, n_chunks, fwd, 0)


def _retention(proj, log_gamma, gain, batch, seq):
    t = proj.shape[0]
    qk_blocks = RET_QK_WIDTH // RET_QK_DIM
    k_off = qk_blocks
    v_off = 2 * RET_QK_WIDTH // RET_V_DIM
    g_off = v_off + RET_V_WIDTH // RET_V_DIM
    return pl.pallas_call(
        _ret_kernel,
        grid=(batch, RET_HEADS),
        in_specs=[
            pl.BlockSpec(memory_space=pltpu.SMEM),
            pl.BlockSpec((seq, RET_QK_DIM), lambda b, h: (b, h)),
            pl.BlockSpec((seq, RET_QK_DIM), lambda b, h: (b, k_off + h)),
            pl.BlockSpec((seq, RET_V_DIM), lambda b, h: (b, v_off + h)),
            pl.BlockSpec((seq, RET_V_DIM), lambda b, h: (b, g_off + h)),
            pl.BlockSpec((1, RET_V_DIM), lambda b, h: (0, h)),
        ],
        out_specs=pl.BlockSpec((seq, RET_V_DIM), lambda b, h: (b, h)),
        out_shape=jax.ShapeDtypeStruct((t, RET_V_WIDTH), BF16),
        scratch_shapes=[
            pltpu.VMEM((seq, RET_V_DIM), F32),
            pltpu.VMEM((RET_QK_DIM, RET_V_DIM), F32),
            pltpu.VMEM((RET_QK_DIM, RET_V_DIM), F32),
        ],
        compiler_params=_params("arbitrary", "arbitrary"),
        name="retention",
    )(log_gamma, proj, proj, proj, proj, gain)


def _ffn0_kernel(x_ref, z_ref, wo_ref, g_ref, wg_ref, wu_ref, wd_ref, o_ref, h1_ref, hn_ref, acc_ref):
    j = pl.program_id(1)

    @pl.when(j == 0)
    def _():
        h1 = x_ref[...] + _dot(z_ref[...], wo_ref[...])
        h1_ref[...] = h1
        hn_ref[...] = _rms(h1, g_ref[...]).astype(BF16)

    hn = hn_ref[...]
    hidden = (_silu(_dot(hn, wg_ref[...])) * _dot(hn, wu_ref[...])).astype(BF16)
    part = _dot(hidden, wd_ref[...])

    @pl.when(j == 0)
    def _():
        acc_ref[...] = part

    @pl.when(j > 0)
    def _():
        acc_ref[...] += part

    @pl.when(j == pl.num_programs(1) - 1)
    def _():
        o_ref[...] = h1_ref[...] + acc_ref[...]


def _ffn0(x2d, z, w_out, g, w_gate, w_up, w_down):
    t = x2d.shape[0]
    d_ff = w_gate.shape[1]
    tm, fc = FFN0_TM, FFN0_FC
    return pl.pallas_call(
        _ffn0_kernel,
        grid=(t // tm, d_ff // fc),
        in_specs=[
            pl.BlockSpec((tm, D_MODEL), lambda i, j: (i, 0)),
            pl.BlockSpec((tm, RET_V_WIDTH), lambda i, j: (i, 0)),
            pl.BlockSpec((RET_V_WIDTH, D_MODEL), lambda i, j: (0, 0)),
            pl.BlockSpec((1, D_MODEL), lambda i, j: (0, 0)),
            pl.BlockSpec((D_MODEL, fc), lambda i, j: (0, j)),
            pl.BlockSpec((D_MODEL, fc), lambda i, j: (0, j)),
            pl.BlockSpec((fc, D_MODEL), lambda i, j: (j, 0)),
        ],
        out_specs=pl.BlockSpec((tm, D_MODEL), lambda i, j: (i, 0)),
        out_shape=jax.ShapeDtypeStruct((t, D_MODEL), F32),
        scratch_shapes=[
            pltpu.VMEM((tm, D_MODEL), F32),
            pltpu.VMEM((tm, D_MODEL), BF16),
            pltpu.VMEM((tm, D_MODEL), F32),
        ],
        compiler_params=_params("arbitrary", "arbitrary"),
        name="ffn0",
    )(x2d, z, w_out, g, w_gate, w_up, w_down)


def _fnet_ch_kernel(h_ref, g_ref, cc_ref, sc_ref, xc_ref, xs_ref):
    hn = _rms(h_ref[...], g_ref[...]).astype(BF16)
    cc = cc_ref[...]
    sc = sc_ref[...]
    for grp in range(FNET_GROUPS):
        cols = slice(grp * FNET_GROUP_DIM, (grp + 1) * FNET_GROUP_DIM)
        xc_ref[:, cols] = _dot(hn[:, cols], cc).astype(BF16)
        xs_ref[:, cols] = _dot(hn[:, cols], sc).astype(BF16)


def _fnet_channels(h, g, cc, sc):
    t = h.shape[0]
    tm = ROW_TM
    row = pl.BlockSpec((tm, D_MODEL), lambda i: (i, 0))
    tab = pl.BlockSpec((FNET_GROUP_DIM, FNET_GROUP_DIM), lambda i: (0, 0))
    return pl.pallas_call(
        _fnet_ch_kernel,
        grid=(t // tm,),
        in_specs=[row, pl.BlockSpec((1, D_MODEL), lambda i: (0, 0)), tab, tab],
        out_specs=[row, row],
        out_shape=[jax.ShapeDtypeStruct((t, D_MODEL), BF16)] * 2,
        compiler_params=_params("arbitrary"),
        name="fnet_channels",
    )(h, g, cc, sc)


def _dft_across_blocks(xs):
    n = len(xs)
    if n == 1:
        return xs
    even = _dft_across_blocks(xs[0::2])
    odd = _dft_across_blocks(xs[1::2])
    out = [None] * n
    for k in range(n // 2):
        er, ei = even[k]
        o_r, o_i = odd[k]
        if k == 0:
            tr, ti = o_r, o_i
            out[k], out[k + n // 2] = (er + tr, ei + ti), (er - tr, ei - ti)
        elif 4 * k == n:
            out[k], out[k + n // 2] = (er + o_i, ei - o_r), (er - o_i, ei + o_r)
        else:
            c, s = math.cos(2.0 * math.pi * k / n), math.sin(2.0 * math.pi * k / n)
            tr = o_r * c + o_i * s
            ti = o_i * c - o_r * s
            out[k], out[k + n // 2] = (er + tr, ei + ti), (er - tr, ei - ti)
    return out


def _fnet_seq_kernel(zr_ref, zi_ref, gc_ref, gs_ref, o_ref, ar_ref, ai_ref, ys_ref):
    blk = FNET_SEQ_BLOCK
    n1 = zr_ref.shape[0] // blk
    rc = FNET_ROW_CHUNK
    lanes = 128

    def body(r, carry):
        r0 = pl.multiple_of(r * rc, rc)
        for l in range(zr_ref.shape[1] // lanes):
            cols = slice(l * lanes, (l + 1) * lanes)
            zs = [(zr_ref[pl.ds(s1 * blk + r0, rc), cols].astype(F32),
                   zi_ref[pl.ds(s1 * blk + r0, rc), cols].astype(F32)) for s1 in range(n1)]
            for k1, (a_r, a_i) in enumerate(_dft_across_blocks(zs)):
                ar_ref[pl.ds(k1 * blk + r0, rc), cols] = a_r.astype(BF16)
                ai_ref[pl.ds(k1 * blk + r0, rc), cols] = a_i.astype(BF16)
        return carry

    lax.fori_loop(0, blk // rc, body, 0)

    for k1 in range(n1):
        rows = slice(k1 * blk, (k1 + 1) * blk)
        y = _dot(gc_ref[rows, :], ar_ref[rows, :]) + _dot(gs_ref[rows, :], ai_ref[rows, :])
        for l in range(y.shape[1] // lanes):
            ys_ref[l, pl.ds(k1, blk, stride=n1), :] = y[:, l * lanes:(l + 1) * lanes]
    for l in range(o_ref.shape[1] // lanes):
        o_ref[:, l * lanes:(l + 1) * lanes] = ys_ref[l]


def _fnet_sequence(zr, zi, gc, gs, batch, seq):
    t = zr.shape[0]
    dt = FNET_DT
    blkspec = pl.BlockSpec((seq, dt), lambda b, c: (b, c))
    tab = pl.BlockSpec((seq, FNET_SEQ_BLOCK), lambda b, c: (0, 0))
    return pl.pallas_call(
        _fnet_seq_kernel,
        grid=(batch, D_MODEL // dt),
        in_specs=[blkspec, blkspec, tab, tab],
        out_specs=blkspec,
        out_shape=jax.ShapeDtypeStruct((t, D_MODEL), F32),
        scratch_shapes=[pltpu.VMEM((seq, dt), BF16), pltpu.VMEM((seq, dt), BF16),
                        pltpu.VMEM((dt // 128, seq, 128), F32)],
        compiler_params=_params("arbitrary", "arbitrary"),
        name="fnet_sequence",
    )(zr, zi, gc, gs)


def _router_kernel(h_ref, y_ref, w_ref, g_ref, rhi_ref, rlo_ref, h3_ref, hn_ref, idx_ref, wt_ref):
    h3 = h_ref[...] + _dot(y_ref[...].astype(BF16), w_ref[...])
    h3_ref[...] = h3
    hn = _rms(h3, g_ref[...])
    hn_ref[...] = hn
    hi = hn.astype(BF16)
    lo = (hn - hi.astype(F32)).astype(BF16)
    logits = _dot(hi, rhi_ref[...]) + (_dot(lo, rhi_ref[...]) + _dot(hi, rlo_ref[...]))
    col = lax.broadcasted_iota(jnp.int32, logits.shape, 1)
    m1 = jnp.max(logits, axis=-1, keepdims=True)
    i1 = jnp.min(jnp.where(logits == m1, col, N_EXPERTS), axis=-1, keepdims=True)
    rest = jnp.where(col == i1, -jnp.inf, logits)
    m2 = jnp.max(rest, axis=-1, keepdims=True)
    i2 = jnp.min(jnp.where(rest == m2, col, N_EXPERTS), axis=-1, keepdims=True)
    e2 = jnp.exp(m2 - m1)
    inv = 1.0 / (1.0 + e2)
    idx_ref[:, 0:1] = i1
    idx_ref[:, 1:2] = i2
    wt_ref[:, 0:1] = inv
    wt_ref[:, 1:2] = e2 * inv


def _router(h2, y, w_fnet, g, r_hi, r_lo):
    t = h2.shape[0]
    tm = ROW_TM
    row = pl.BlockSpec((tm, D_MODEL), lambda i: (i, 0))
    pair = pl.BlockSpec((tm, TOP_K), lambda i: (i, 0))
    rtr = pl.BlockSpec((D_MODEL, N_EXPERTS), lambda i: (0, 0))
    return pl.pallas_call(
        _router_kernel,
        grid=(t // tm,),
        in_specs=[row, row, pl.BlockSpec((D_MODEL, D_MODEL), lambda i: (0, 0)),
                  pl.BlockSpec((1, D_MODEL), lambda i: (0, 0)), rtr, rtr],
        out_specs=[row, row, pair, pair],
        out_shape=[jax.ShapeDtypeStruct((t, D_MODEL), F32), jax.ShapeDtypeStruct((t, D_MODEL), F32),
                   jax.ShapeDtypeStruct((t, TOP_K), jnp.int32), jax.ShapeDtypeStruct((t, TOP_K), F32)],
        compiler_params=_params("arbitrary"),
        name="router",
    )(h2, y, w_fnet, g, r_hi, r_lo)


def _routing_tables(top_idx, top_w, tm):
    t = top_idx.shape[0]
    n_pairs = TOP_K * t
    n_tiles = n_pairs // tm + N_EXPERTS
    e_flat = top_idx.T.reshape(n_pairs)
    w_flat = top_w.T.reshape(n_pairs)
    order = jnp.argsort(e_flat, stable=True).astype(jnp.int32)
    experts = jnp.arange(N_EXPERTS, dtype=jnp.int32)
    counts = jnp.sum((e_flat[:, None] == experts[None, :]).astype(jnp.int32), axis=0)
    off = jnp.cumsum(counts) - counts
    tiles_e = (counts + tm - 1) // tm
    tile_end = jnp.cumsum(tiles_e)
    tile_off = tile_end - tiles_e
    n_used = tile_end[-1]
    last_e = jnp.max(jnp.where(tiles_e > 0, experts, 0))
    tile_ids = jnp.arange(n_tiles, dtype=jnp.int32)
    tile_e = jnp.minimum(jnp.sum((tile_ids[:, None] >= tile_end[None, :]).astype(jnp.int32), axis=1), last_e)
    u = jnp.arange(tm, dtype=jnp.int32)
    rank = (tile_ids - tile_off[tile_e])[:, None] * tm + u[None, :]
    valid = (tile_ids < n_used)[:, None] & (rank < counts[tile_e][:, None])
    src = order[jnp.clip(off[tile_e][:, None] + rank, 0, n_pairs - 1)]
    tok = jnp.where(valid, src % t, 0).astype(jnp.int32)
    spare = n_pairs + jnp.broadcast_to(u[None, :], (1, tm))
    dst = jnp.where(valid, src, spare).astype(jnp.int32)
    w_row = jnp.where(valid, w_flat[src], 0.0).astype(F32)
    tok_next = jnp.concatenate([tok[1:], jnp.zeros((2, tm), jnp.int32)], axis=0)
    dst_prev = jnp.concatenate([spare.astype(jnp.int32), dst], axis=0)
    tile_e = jnp.concatenate([tile_e, tile_e[-1:]]).astype(jnp.int32)
    return (tile_e, n_used.reshape(1).astype(jnp.int32), tok[0].reshape(1, 1, tm),
            tok_next.reshape(n_tiles + 1, 1, tm), dst_prev.reshape(n_tiles + 1, 1, tm),
            w_row.reshape(n_tiles * tm, 1))


def _moe_kernel(te_ref, nu_ref, tok0_ref, tokn_ref, dstp_ref, wrow_ref, hn_hbm, wg_ref, wu_ref, wd_ref,
                y2_hbm, xbuf, xb, acc, ybuf, gsem, ssem, *, n_ff):
    del te_ref
    i = pl.program_id(0)
    j = pl.program_id(1)
    n_used = nu_ref[0]
    tm = xbuf.shape[0]
    fc = wg_ref.shape[2]
    rows_per_step = tm // n_ff
    spare0 = y2_hbm.shape[0] - tm
    unroll = 8

    def gather_copy(tok, u):
        return pltpu.make_async_copy(hn_hbm.at[pl.ds(tok, 1), :], xbuf.at[pl.ds(u, 1), :], gsem)

    def scatter_copy(u, row):
        return pltpu.make_async_copy(ybuf.at[pl.ds(u, 1), :], y2_hbm.at[pl.ds(row, 1), :], ssem)

    def start_gather(ref):
        def body(u, carry):
            gather_copy(ref[0, 0, u], u).start()
            return carry
        lax.fori_loop(0, tm, body, 0, unroll=unroll)

    def wait_gather():
        pltpu.make_async_copy(hn_hbm.at[pl.ds(0, tm), :], xbuf, gsem).wait()

    def start_scatter_all():
        def body(u, carry):
            scatter_copy(u, dstp_ref[0, 0, u]).start()
            return carry
        lax.fori_loop(0, tm, body, 0, unroll=unroll)

    def wait_scatter():
        pltpu.make_async_copy(ybuf, y2_hbm.at[pl.ds(0, tm), :], ssem).wait()

    @pl.when((i == 0) & (j == 0))
    def _():
        ybuf[...] = jnp.zeros_like(ybuf)
        fill = pltpu.make_async_copy(ybuf, y2_hbm.at[pl.ds(spare0, tm), :], ssem)
        fill.start()
        fill.wait()
        start_gather(tok0_ref)

    def ff_step(jj):
        if jj == 0:
            wait_gather()
            xb[...] = xbuf[...].astype(BF16)
            for u in range(tm):
                scatter_copy(u, dstp_ref[0, 0, u]).start(priority=1)
        if jj == n_ff - 1:
            wait_scatter()
        for u in range(jj * rows_per_step, (jj + 1) * rows_per_step):
            gather_copy(tokn_ref[0, 0, u], u).start()

        x = xb[...]
        part = None
        for lo in range(0, fc, MOE_SUB):
            cols = slice(lo, min(lo + MOE_SUB, fc))
            hidden = (_silu(_dot(x, wg_ref[0, :, cols])) * _dot(x, wu_ref[0, :, cols])).astype(BF16)
            d = _dot(hidden, wd_ref[0, cols, :])
            part = d if part is None else part + d

        if jj == 0:
            acc[...] = part
        elif jj < n_ff - 1:
            acc[...] += part
        else:
            ybuf[...] = (acc[...] + part) * wrow_ref[...]

    for jj in range(n_ff):
        pl.when((i < n_used) & (j == jj))(functools.partial(ff_step, jj))

    @pl.when((i == n_used) & (j == 0))
    def _():
        start_scatter_all()
        wait_scatter()
        wait_gather()


def _moe(hn, tile_e, n_used, tok0, tok_next, dst_prev, w_row, w_gate, w_up, w_down):
    t = hn.shape[0]
    d_ff = w_gate.shape[2]
    tm, fc = MOE_TM, MOE_FC
    n_steps = tok_next.shape[0]
    n_ff = d_ff // fc
    assert n_ff >= 2 and n_ff * fc == d_ff and tm % n_ff == 0
    last_tile = n_steps - 2

    def ff_step(i, j, nu):
        return jnp.where(i < nu[0], j, n_ff - 1)

    grid_spec = pltpu.PrefetchScalarGridSpec(
        num_scalar_prefetch=2,
        grid=(n_steps, n_ff),
        in_specs=[
            pl.BlockSpec((1, 1, tm), lambda i, j, te, nu: (0, 0, 0), memory_space=pltpu.SMEM),
            pl.BlockSpec((1, 1, tm), lambda i, j, te, nu: (i, 0, 0), memory_space=pltpu.SMEM),
            pl.BlockSpec((1, 1, tm), lambda i, j, te, nu: (i, 0, 0), memory_space=pltpu.SMEM),
            pl.BlockSpec((tm, 1), lambda i, j, te, nu: (jnp.minimum(i, last_tile), 0)),
            pl.BlockSpec(memory_space=pl.ANY),
            pl.BlockSpec((1, D_MODEL, fc), lambda i, j, te, nu: (te[i], 0, ff_step(i, j, nu))),
            pl.BlockSpec((1, D_MODEL, fc), lambda i, j, te, nu: (te[i], 0, ff_step(i, j, nu))),
            pl.BlockSpec((1, fc, D_MODEL), lambda i, j, te, nu: (te[i], ff_step(i, j, nu), 0)),
        ],
        out_specs=pl.BlockSpec(memory_space=pl.ANY),
        scratch_shapes=[
            pltpu.VMEM((tm, D_MODEL), F32),
            pltpu.VMEM((tm, D_MODEL), BF16),
            pltpu.VMEM((tm, D_MODEL), F32),
            pltpu.VMEM((tm, D_MODEL), F32),
            pltpu.SemaphoreType.DMA(()),
            pltpu.SemaphoreType.DMA(()),
        ],
    )
    return pl.pallas_call(
        functools.partial(_moe_kernel, n_ff=n_ff),
        grid_spec=grid_spec,
        out_shape=jax.ShapeDtypeStruct((TOP_K * t + tm, D_MODEL), F32),
        compiler_params=_params("arbitrary", "arbitrary"),
        name="moe",
    )(tile_e, n_used, tok0, tok_next, dst_prev, w_row, hn, w_gate, w_up, w_down)


def _final_kernel(h_ref, ya_ref, yb_ref, g_ref, o_ref):
    o_ref[...] = _rms(h_ref[...] + (ya_ref[...] + yb_ref[...]), g_ref[...])


def _final(h3, y2, g):
    t = h3.shape[0]
    tm = ROW_TM
    second = t // tm
    row = pl.BlockSpec((tm, D_MODEL), lambda i: (i, 0))
    return pl.pallas_call(
        _final_kernel,
        grid=(t // tm,),
        in_specs=[row, row, pl.BlockSpec((tm, D_MODEL), lambda i: (i + second, 0)),
                  pl.BlockSpec((1, D_MODEL), lambda i: (0, 0))],
        out_specs=row,
        out_shape=jax.ShapeDtypeStruct((t, D_MODEL), F32),
        compiler_params=_params("arbitrary"),
        name="final_norm",
    )(h3, y2, y2, g)


def _rotary_tables(seq):
    half = RET_QK_DIM // 2
    inv = ROPE_BASE ** (-jnp.arange(half, dtype=F32) / half)
    ang = jnp.arange(seq, dtype=F32)[:, None] * inv[None, :]
    return jnp.cos(ang), jnp.sin(ang)


def _channel_dft_tables(n, scale):
    k = jnp.arange(n, dtype=jnp.int32)
    ang = ((k[:, None] * k[None, :]) % n).astype(F32) * (2.0 * math.pi / n)
    return (jnp.cos(ang) * scale).astype(BF16), (-jnp.sin(ang) * scale).astype(BF16)


def _sequence_dft_tables(seq, scale):
    blk = FNET_SEQ_BLOCK
    n1 = seq // blk
    k1 = jnp.arange(n1, dtype=jnp.int32)[:, None, None]
    k2 = jnp.arange(blk, dtype=jnp.int32)[None, :, None]
    s2 = jnp.arange(blk, dtype=jnp.int32)[None, None, :]
    ang = (((k1 + n1 * k2) * s2) % seq).astype(F32) * (2.0 * math.pi / seq)
    gc = (jnp.cos(ang) * scale).astype(BF16).reshape(n1 * blk, blk)
    gs = (jnp.sin(ang) * scale).astype(BF16).reshape(n1 * blk, blk)
    return gc, gs


def kernel(x, mix_norm, ffn_norm, ret_w_in, ret_decay_logit, ret_gn_gain, ret_w_out,
           dense_w_gate, dense_w_up, dense_w_down, fnet_w_out, moe_router,
           moe_w_gate, moe_w_up, moe_w_down, final_norm):
    batch, seq, d = x.shape
    t = batch * seq
    n1 = seq // FNET_SEQ_BLOCK
    assert d == D_MODEL and seq % RET_CHUNK == 0 and seq % PROJ_TM == 0
    assert n1 * FNET_SEQ_BLOCK == seq and n1 & (n1 - 1) == 0
    x2d = x.reshape(t, d)

    cos, sin = _rotary_tables(seq)
    proj = _proj(x2d, mix_norm[0:1], cos, sin, ret_w_in[0].astype(BF16), seq)
    log_gamma = jax.nn.log_sigmoid(ret_decay_logit[0].astype(F32))
    z = _retention(proj, log_gamma, ret_gn_gain[0:1], batch, seq)
    h2 = _ffn0(x2d, z, ret_w_out[0].astype(BF16), ffn_norm[0:1],
               dense_w_gate[0].astype(BF16), dense_w_up[0].astype(BF16), dense_w_down[0].astype(BF16))

    ch_scale = FNET_GROUP_DIM ** -0.5
    cc, sc = _channel_dft_tables(FNET_GROUP_DIM, ch_scale)
    gc, gs = _sequence_dft_tables(seq, seq ** -0.5)
    zr, zi = _fnet_channels(h2, mix_norm[1:2], cc, sc)
    y = _fnet_sequence(zr, zi, gc, gs, batch, seq)
    r_hi = moe_router[0].astype(BF16)
    r_lo = (moe_router[0] - r_hi.astype(F32)).astype(BF16)
    h3, hn3, top_idx, top_w = _router(h2, y, fnet_w_out[0].astype(BF16), ffn_norm[1:2], r_hi, r_lo)

    tile_e, n_used, tok0, tok_next, dst_prev, w_row = _routing_tables(top_idx, top_w, MOE_TM)
    y2 = _moe(hn3, tile_e, n_used, tok0, tok_next, dst_prev, w_row,
              moe_w_gate[0].astype(BF16), moe_w_up[0].astype(BF16), moe_w_down[0].astype(BF16))
    out = _final(h3, y2, final_norm.reshape(1, d))
    return out.reshape(batch, seq, d)
```

```python
import functools
import math

import jax
import jax.numpy as jnp
from jax import lax
from jax.experimental import pallas as pl
from jax.experimental.pallas import tpu as pltpu

F32 = jnp.float32
BF16 = jnp.bfloat16

D_MODEL = 1024
RET_HEADS = 4
RET_QK_DIM = D_MODEL // RET_HEADS
RET_V_DIM = 2 * RET_QK_DIM
RET_QK_WIDTH = RET_HEADS * RET_QK_DIM
RET_V_WIDTH = RET_HEADS * RET_V_DIM
RET_IN_WIDTH = 2 * RET_QK_WIDTH + 2 * RET_V_WIDTH
ROPE_BASE = 10000.0
FNET_GROUPS = 4
FNET_GROUP_DIM = D_MODEL // FNET_GROUPS
N_EXPERTS = 8
TOP_K = 2
EPS = 1e-6

RET_CHUNK = 256
VMEM_LIMIT_BYTES = 56 * 1024 * 1024

PROJ_TM, PROJ_TN = 1024, 2048
FFN0_TM, FFN0_SUB = 512, 512
ROW_TM = 1024
FNET_SEQ_BLOCK = 256
FNET_DT = 256
FNET_ROW_CHUNK = 16
MOE_TM, MOE_FC, MOE_SUB = 1024, 1792, 512


def _params(*semantics):
    return pltpu.CompilerParams(dimension_semantics=semantics, vmem_limit_bytes=VMEM_LIMIT_BYTES)


def _dot(a, b):
    return jnp.dot(a, b, preferred_element_type=F32)


def _rms(x, g):
    return x * lax.rsqrt(jnp.mean(x * x, axis=-1, keepdims=True) + EPS) * g


def _silu(a):
    return a * (1.0 / (1.0 + jnp.exp(-a)))


def _proj_kernel(x_ref, g_ref, cos_ref, sin_ref, w_ref, o_ref, hn_ref):
    j = pl.program_id(1)

    @pl.when(j == 0)
    def _():
        hn_ref[...] = _rms(x_ref[...], g_ref[...]).astype(BF16)

    tn = o_ref.shape[1]
    qk_tiles = 2 * RET_QK_WIDTH // tn
    v_tiles = RET_V_WIDTH // tn
    hn = hn_ref[...]
    half = RET_QK_DIM // 2

    def head(h):
        return _dot(hn, w_ref[:, h * RET_QK_DIM:(h + 1) * RET_QK_DIM])

    @pl.when(j < qk_tiles)
    def _():
        for h in range(tn // RET_QK_DIM):
            lo = h * RET_QK_DIM
            is_k = j * tn + lo >= RET_QK_WIDTH
            scale = jnp.where(is_k, RET_QK_DIM ** -0.5, 1.0).astype(F32)
            cos = cos_ref[...] * scale
            sin = sin_ref[...] * scale
            y = head(h)
            t1 = y[:, :half]
            t2 = y[:, half:]
            o_ref[:, lo:lo + half] = (t1 * cos - t2 * sin).astype(BF16)
            o_ref[:, lo + half:lo + RET_QK_DIM] = (t1 * sin + t2 * cos).astype(BF16)

    @pl.when((j >= qk_tiles) & (j < qk_tiles + v_tiles))
    def _():
        for h in range(tn // RET_QK_DIM):
            o_ref[:, h * RET_QK_DIM:(h + 1) * RET_QK_DIM] = head(h).astype(BF16)

    @pl.when(j >= qk_tiles + v_tiles)
    def _():
        for h in range(tn // RET_QK_DIM):
            o_ref[:, h * RET_QK_DIM:(h + 1) * RET_QK_DIM] = _silu(head(h)).astype(BF16)


def _proj(x2d, g, cos, sin, w_in, seq):
    t = x2d.shape[0]
    tm, tn = PROJ_TM, PROJ_TN
    pos_tiles = seq // tm
    return pl.pallas_call(
        _proj_kernel,
        grid=(t // tm, RET_IN_WIDTH // tn),
        in_specs=[
            pl.BlockSpec((tm, D_MODEL), lambda i, j: (i, 0)),
            pl.BlockSpec((1, D_MODEL), lambda i, j: (0, 0)),
            pl.BlockSpec((tm, RET_QK_DIM // 2), lambda i, j: (i % pos_tiles, 0)),
            pl.BlockSpec((tm, RET_QK_DIM // 2), lambda i, j: (i % pos_tiles, 0)),
            pl.BlockSpec((D_MODEL, tn), lambda i, j: (0, j)),
        ],
        out_specs=pl.BlockSpec((tm, tn), lambda i, j: (i, j)),
        out_shape=jax.ShapeDtypeStruct((t, RET_IN_WIDTH), BF16),
        scratch_shapes=[pltpu.VMEM((tm, D_MODEL), BF16)],
        compiler_params=_params("arbitrary", "arbitrary"),
        name="proj",
    )(x2d, g, cos, sin, w_in)


def _ret_kernel(lg_ref, q_ref, k_ref, v_ref, g_ref, gain_ref, o_ref, sfa_ref, sba_ref, sf_ref, sb_ref):
    c = RET_CHUNK
    n_chunks = q_ref.shape[0] // c
    h = pl.program_id(1)
    lgf = lg_ref[0, h]
    lgb = lg_ref[1, h]

    row = lax.broadcasted_iota(jnp.int32, (c, RET_QK_DIM), 0).astype(F32)
    xi_f = jnp.exp((row + 1.0) * lgf).astype(BF16)
    zeta_f = jnp.exp((c - 1.0 - row) * lgf).astype(BF16)
    xi_b = jnp.exp((c - row) * lgb).astype(BF16)
    zeta_b = jnp.exp(row * lgb).astype(BF16)
    g_f = jnp.exp(jnp.full((1, RET_V_DIM), c, F32) * lgf)
    g_b = jnp.exp(jnp.full((1, RET_V_DIM), c, F32) * lgb)
    di = lax.broadcasted_iota(jnp.int32, (c, c), 0)
    dj = lax.broadcasted_iota(jnp.int32, (c, c), 1)
    diff = (di - dj).astype(F32)
    decay = jnp.exp(jnp.where(diff >= 0, diff * lgf, -diff * lgb))

    def chunk(n):
        r0 = pl.multiple_of(n * c, c)
        return pl.ds(r0, c)

    def scaled(t, s):
        return t * s

    def outer(kz, vn):
        return lax.dot_general(kz, vn, (((0,), (0,)), ((), ())), preferred_element_type=F32)

    sf_ref[...] = jnp.zeros_like(sf_ref)
    sb_ref[...] = jnp.zeros_like(sb_ref)

    def states(step, carry):
        nf = step
        nb = n_chunks - 1 - step
        sf = sf_ref[...]
        sb = sb_ref[...]
        sfa_ref[nf] = sf.astype(BF16)
        sba_ref[nb] = sb.astype(BF16)
        rf, rb = chunk(nf), chunk(nb)
        sf_ref[...] = sf * g_f + outer(scaled(k_ref[rf, :], zeta_f), v_ref[rf, :])
        sb_ref[...] = sb * g_b + outer(scaled(k_ref[rb, :], zeta_b), v_ref[rb, :])
        return carry

    lax.fori_loop(0, n_chunks, states, 0, unroll=2)

    gain = gain_ref[...]

    def fwd(n, carry):
        rows = chunk(n)
        qn, kn, vn = q_ref[rows, :], k_ref[rows, :], v_ref[rows, :]
        scores = lax.dot_general(qn, kn, (((1,), (1,)), ((), ())), preferred_element_type=F32) * decay
        y = (_dot(scores.astype(BF16), vn) + _dot(scaled(qn, xi_f), sfa_ref[n])
             + _dot(scaled(qn, xi_b), sba_ref[n]))
        mu = jnp.mean(y, axis=-1, keepdims=True)
        yc = y - mu
        var = jnp.mean(yc * yc, axis=-1, keepdims=True)
        yn = yc * lax.rsqrt(var + EPS) * gain
        o_ref[rows, :] = g_ref[rows, :] * yn.astype(BF16)
        return carry

    lax.fori_loop(0, n_chunks, fwd, 0, unroll=2)


def _retention(proj, log_gamma, gain, batch, seq):
    t = proj.shape[0]
    qk_blocks = RET_QK_WIDTH // RET_QK_DIM
    k_off = qk_blocks
    v_off = 2 * RET_QK_WIDTH // RET_V_DIM
    g_off = v_off + RET_V_WIDTH // RET_V_DIM
    return pl.pallas_call(
        _ret_kernel,
        grid=(batch, RET_HEADS),
        in_specs=[
            pl.BlockSpec(memory_space=pltpu.SMEM),
            pl.BlockSpec((seq, RET_QK_DIM), lambda b, h: (b, h)),
            pl.BlockSpec((seq, RET_QK_DIM), lambda b, h: (b, k_off + h)),
            pl.BlockSpec((seq, RET_V_DIM), lambda b, h: (b, v_off + h)),
            pl.BlockSpec((seq, RET_V_DIM), lambda b, h: (b, g_off + h)),
            pl.BlockSpec((1, RET_V_DIM), lambda b, h: (0, h)),
        ],
        out_specs=pl.BlockSpec((seq, RET_V_DIM), lambda b, h: (b, h)),
        out_shape=jax.ShapeDtypeStruct((t, RET_V_WIDTH), BF16),
        scratch_shapes=[
            pltpu.VMEM((seq // RET_CHUNK, RET_QK_DIM, RET_V_DIM), BF16),
            pltpu.VMEM((seq // RET_CHUNK, RET_QK_DIM, RET_V_DIM), BF16),
            pltpu.VMEM((RET_QK_DIM, RET_V_DIM), F32),
            pltpu.VMEM((RET_QK_DIM, RET_V_DIM), F32),
        ],
        compiler_params=_params("arbitrary", "arbitrary"),
        name="retention",
    )(log_gamma, proj, proj, proj, proj, gain)


def _ffn0_kernel(x_ref, z_ref, wo_ref, g_ref, wg_ref, wu_ref, wd_ref, o_ref):
    h1 = x_ref[...] + _dot(z_ref[...], wo_ref[...])
    hn = _rms(h1, g_ref[...]).astype(BF16)
    d_ff = wg_ref.shape[1]
    part = None
    for lo in range(0, d_ff, FFN0_SUB):
        cols = slice(lo, min(lo + FFN0_SUB, d_ff))
        hidden = (_silu(_dot(hn, wg_ref[:, cols])) * _dot(hn, wu_ref[:, cols])).astype(BF16)
        d = _dot(hidden, wd_ref[cols, :])
        part = d if part is None else part + d
    o_ref[...] = h1 + part


def _ffn0(x2d, z, w_out, g, w_gate, w_up, w_down):
    t = x2d.shape[0]
    d_ff = w_gate.shape[1]
    tm = FFN0_TM
    once = pl.Buffered(1)
    return pl.pallas_call(
        _ffn0_kernel,
        grid=(t // tm,),
        in_specs=[
            pl.BlockSpec((tm, D_MODEL), lambda i: (i, 0)),
            pl.BlockSpec((tm, RET_V_WIDTH), lambda i: (i, 0)),
            pl.BlockSpec((RET_V_WIDTH, D_MODEL), lambda i: (0, 0), pipeline_mode=once),
            pl.BlockSpec((1, D_MODEL), lambda i: (0, 0), pipeline_mode=once),
            pl.BlockSpec((D_MODEL, d_ff), lambda i: (0, 0), pipeline_mode=once),
            pl.BlockSpec((D_MODEL, d_ff), lambda i: (0, 0), pipeline_mode=once),
            pl.BlockSpec((d_ff, D_MODEL), lambda i: (0, 0), pipeline_mode=once),
        ],
        out_specs=pl.BlockSpec((tm, D_MODEL), lambda i: (i, 0)),
        out_shape=jax.ShapeDtypeStruct((t, D_MODEL), F32),
        compiler_params=_params("arbitrary"),
        name="ffn0",
    )(x2d, z, w_out, g, w_gate, w_up, w_down)


def _fnet_ch_kernel(h_ref, g_ref, cc_ref, sc_ref, xc_ref, xs_ref):
    hn = _rms(h_ref[...], g_ref[...]).astype(BF16)
    cc = cc_ref[...]
    sc = sc_ref[...]
    for grp in range(FNET_GROUPS):
        cols = slice(grp * FNET_GROUP_DIM, (grp + 1) * FNET_GROUP_DIM)
        xc_ref[:, cols] = _dot(hn[:, cols], cc).astype(BF16)
        xs_ref[:, cols] = _dot(hn[:, cols], sc).astype(BF16)


def _fnet_channels(h, g, cc, sc):
    t = h.shape[0]
    tm = ROW_TM
    row = pl.BlockSpec((tm, D_MODEL), lambda i: (i, 0))
    tab = pl.BlockSpec((FNET_GROUP_DIM, FNET_GROUP_DIM), lambda i: (0, 0))
    return pl.pallas_call(
        _fnet_ch_kernel,
        grid=(t // tm,),
        in_specs=[row, pl.BlockSpec((1, D_MODEL), lambda i: (0, 0)), tab, tab],
        out_specs=[row, row],
        out_shape=[jax.ShapeDtypeStruct((t, D_MODEL), BF16)] * 2,
        compiler_params=_params("arbitrary"),
        name="fnet_channels",
    )(h, g, cc, sc)


def _dft_across_blocks(xs):
    n = len(xs)
    if n == 1:
        return xs
    even = _dft_across_blocks(xs[0::2])
    odd = _dft_across_blocks(xs[1::2])
    out = [None] * n
    for k in range(n // 2):
        er, ei = even[k]
        o_r, o_i = odd[k]
        if k == 0:
            tr, ti = o_r, o_i
            out[k], out[k + n // 2] = (er + tr, ei + ti), (er - tr, ei - ti)
        elif 4 * k == n:
            out[k], out[k + n // 2] = (er + o_i, ei - o_r), (er - o_i, ei + o_r)
        else:
            c, s = math.cos(2.0 * math.pi * k / n), math.sin(2.0 * math.pi * k / n)
            tr = o_r * c + o_i * s
            ti = o_i * c - o_r * s
            out[k], out[k + n // 2] = (er + tr, ei + ti), (er - tr, ei - ti)
    return out


def _fnet_seq_kernel(zr_ref, zi_ref, gc_ref, gs_ref, o_ref, ar_ref, ai_ref, ys_ref):
    blk = FNET_SEQ_BLOCK
    n1 = zr_ref.shape[0] // blk
    rc = FNET_ROW_CHUNK
    lanes = 128

    def body(r, carry):
        r0 = pl.multiple_of(r * rc, rc)
        for l in range(zr_ref.shape[1] // lanes):
            cols = slice(l * lanes, (l + 1) * lanes)
            zs = [(zr_ref[pl.ds(s1 * blk + r0, rc), cols].astype(F32),
                   zi_ref[pl.ds(s1 * blk + r0, rc), cols].astype(F32)) for s1 in range(n1)]
            for k1, (a_r, a_i) in enumerate(_dft_across_blocks(zs)):
                ar_ref[pl.ds(k1 * blk + r0, rc), cols] = a_r.astype(BF16)
                ai_ref[pl.ds(k1 * blk + r0, rc), cols] = a_i.astype(BF16)
        return carry

    lax.fori_loop(0, blk // rc, body, 0)

    for k1 in range(n1):
        rows = slice(k1 * blk, (k1 + 1) * blk)
        y = _dot(gc_ref[rows, :], ar_ref[rows, :]) + _dot(gs_ref[rows, :], ai_ref[rows, :])
        for l in range(y.shape[1] // lanes):
            ys_ref[l, pl.ds(k1, blk, stride=n1), :] = y[:, l * lanes:(l + 1) * lanes]
    for l in range(o_ref.shape[1] // lanes):
        o_ref[:, l * lanes:(l + 1) * lanes] = ys_ref[l]


def _fnet_sequence(zr, zi, gc, gs, batch, seq):
    t = zr.shape[0]
    dt = FNET_DT
    blkspec = pl.BlockSpec((seq, dt), lambda b, c: (b, c))
    tab = pl.BlockSpec((seq, FNET_SEQ_BLOCK), lambda b, c: (0, 0))
    return pl.pallas_call(
        _fnet_seq_kernel,
        grid=(batch, D_MODEL // dt),
        in_specs=[blkspec, blkspec, tab, tab],
        out_specs=blkspec,
        out_shape=jax.ShapeDtypeStruct((t, D_MODEL), F32),
        scratch_shapes=[pltpu.VMEM((seq, dt), BF16), pltpu.VMEM((seq, dt), BF16),
                        pltpu.VMEM((dt // 128, seq, 128), F32)],
        compiler_params=_params("arbitrary", "arbitrary"),
        name="fnet_sequence",
    )(zr, zi, gc, gs)


def _router_kernel(h_ref, y_ref, w_ref, g_ref, rhi_ref, rlo_ref, h3_ref, hn_ref, idx_ref, wt_ref):
    h3 = h_ref[...] + _dot(y_ref[...].astype(BF16), w_ref[...])
    h3_ref[...] = h3
    hn = _rms(h3, g_ref[...])
    hn_ref[...] = hn
    hi = hn.astype(BF16)
    lo = (hn - hi.astype(F32)).astype(BF16)
    logits = _dot(hi, rhi_ref[...]) + (_dot(lo, rhi_ref[...]) + _dot(hi, rlo_ref[...]))
    col = lax.broadcasted_iota(jnp.int32, logits.shape, 1)
    m1 = jnp.max(logits, axis=-1, keepdims=True)
    i1 = jnp.min(jnp.where(logits == m1, col, N_EXPERTS), axis=-1, keepdims=True)
    rest = jnp.where(col == i1, -jnp.inf, logits)
    m2 = jnp.max(rest, axis=-1, keepdims=True)
    i2 = jnp.min(jnp.where(rest == m2, col, N_EXPERTS), axis=-1, keepdims=True)
    e2 = jnp.exp(m2 - m1)
    inv = 1.0 / (1.0 + e2)
    idx_ref[:, 0:1] = i1
    idx_ref[:, 1:2] = i2
    wt_ref[:, 0:1] = inv
    wt_ref[:, 1:2] = e2 * inv


def _router(h2, y, w_fnet, g, r_hi, r_lo):
    t = h2.shape[0]
    tm = ROW_TM
    row = pl.BlockSpec((tm, D_MODEL), lambda i: (i, 0))
    pair = pl.BlockSpec((tm, TOP_K), lambda i: (i, 0))
    rtr = pl.BlockSpec((D_MODEL, N_EXPERTS), lambda i: (0, 0))
    return pl.pallas_call(
        _router_kernel,
        grid=(t // tm,),
        in_specs=[row, row, pl.BlockSpec((D_MODEL, D_MODEL), lambda i: (0, 0)),
                  pl.BlockSpec((1, D_MODEL), lambda i: (0, 0)), rtr, rtr],
        out_specs=[row, row, pair, pair],
        out_shape=[jax.ShapeDtypeStruct((t, D_MODEL), F32), jax.ShapeDtypeStruct((t, D_MODEL), F32),
                   jax.ShapeDtypeStruct((t, TOP_K), jnp.int32), jax.ShapeDtypeStruct((t, TOP_K), F32)],
        compiler_params=_params("arbitrary"),
        name="router",
    )(h2, y, w_fnet, g, r_hi, r_lo)


def _routing_tables(top_idx, top_w, tm):
    t = top_idx.shape[0]
    n_pairs = TOP_K * t
    n_tiles = n_pairs // tm + N_EXPERTS
    e_flat = top_idx.T.reshape(n_pairs)
    w_flat = top_w.T.reshape(n_pairs)
    order = jnp.argsort(e_flat, stable=True).astype(jnp.int32)
    experts = jnp.arange(N_EXPERTS, dtype=jnp.int32)
    counts = jnp.sum((e_flat[:, None] == experts[None, :]).astype(jnp.int32), axis=0)
    off = jnp.cumsum(counts) - counts
    tiles_e = (counts + tm - 1) // tm
    tile_end = jnp.cumsum(tiles_e)
    tile_off = tile_end - tiles_e
    n_used = tile_end[-1]
    last_e = jnp.max(jnp.where(tiles_e > 0, experts, 0))
    tile_ids = jnp.arange(n_tiles, dtype=jnp.int32)
    tile_e = jnp.minimum(jnp.sum((tile_ids[:, None] >= tile_end[None, :]).astype(jnp.int32), axis=1), last_e)
    u = jnp.arange(tm, dtype=jnp.int32)
    rank = (tile_ids - tile_off[tile_e])[:, None] * tm + u[None, :]
    valid = (tile_ids < n_used)[:, None] & (rank < counts[tile_e][:, None])
    src = order[jnp.clip(off[tile_e][:, None] + rank, 0, n_pairs - 1)]
    tok = jnp.where(valid, src % t, 0).astype(jnp.int32)
    spare = n_pairs + jnp.broadcast_to(u[None, :], (1, tm))
    dst = jnp.where(valid, src, spare).astype(jnp.int32)
    w_row = jnp.where(valid, w_flat[src], 0.0).astype(F32)
    tok_next = jnp.concatenate([tok[1:], jnp.zeros((2, tm), jnp.int32)], axis=0)
    dst_prev = jnp.concatenate([spare.astype(jnp.int32), dst], axis=0)
    tile_e = jnp.concatenate([tile_e, tile_e[-1:]]).astype(jnp.int32)
    return (tile_e, n_used.reshape(1).astype(jnp.int32), tok[0].reshape(1, 1, tm),
            tok_next.reshape(n_tiles + 1, 1, tm), dst_prev.reshape(n_tiles + 1, 1, tm),
            w_row.reshape(n_tiles * tm, 1))


def _moe_kernel(te_ref, nu_ref, tok0_ref, tokn_ref, dstp_ref, wrow_ref, hn_hbm, wg_ref, wu_ref, wd_ref,
                y2_hbm, xbuf, xb, acc, ybuf, gsem, ssem, *, n_ff):
    del te_ref
    i = pl.program_id(0)
    j = pl.program_id(1)
    n_used = nu_ref[0]
    tm = xbuf.shape[0]
    fc = wg_ref.shape[2]
    rows_per_step = tm // n_ff
    spare0 = y2_hbm.shape[0] - tm
    unroll = 8

    def gather_copy(tok, u):
        return pltpu.make_async_copy(hn_hbm.at[pl.ds(tok, 1), :], xbuf.at[pl.ds(u, 1), :], gsem)

    def scatter_copy(u, row):
        return pltpu.make_async_copy(ybuf.at[pl.ds(u, 1), :], y2_hbm.at[pl.ds(row, 1), :], ssem)

    def start_gather(ref):
        def body(u, carry):
            gather_copy(ref[0, 0, u], u).start()
            return carry
        lax.fori_loop(0, tm, body, 0, unroll=unroll)

    def wait_gather():
        pltpu.make_async_copy(hn_hbm.at[pl.ds(0, tm), :], xbuf, gsem).wait()

    def start_scatter_all():
        def body(u, carry):
            scatter_copy(u, dstp_ref[0, 0, u]).start()
            return carry
        lax.fori_loop(0, tm, body, 0, unroll=unroll)

    def wait_scatter():
        pltpu.make_async_copy(ybuf, y2_hbm.at[pl.ds(0, tm), :], ssem).wait()

    @pl.when((i == 0) & (j == 0))
    def _():
        ybuf[...] = jnp.zeros_like(ybuf)
        fill = pltpu.make_async_copy(ybuf, y2_hbm.at[pl.ds(spare0, tm), :], ssem)
        fill.start()
        fill.wait()
        start_gather(tok0_ref)

    def ff_step(jj):
        if jj == 0:
            wait_gather()
            xb[...] = xbuf[...].astype(BF16)
            for u in range(tm):
                scatter_copy(u, dstp_ref[0, 0, u]).start(priority=1)
        if jj == n_ff - 1:
            wait_scatter()
        for u in range(jj * rows_per_step, (jj + 1) * rows_per_step):
            gather_copy(tokn_ref[0, 0, u], u).start()

        x = xb[...]
        part = None
        for lo in range(0, fc, MOE_SUB):
            cols = slice(lo, min(lo + MOE_SUB, fc))
            hidden = (_silu(_dot(x, wg_ref[0, :, cols])) * _dot(x, wu_ref[0, :, cols])).astype(BF16)
            d = _dot(hidden, wd_ref[0, cols, :])
            part = d if part is None else part + d

        if jj == 0:
            acc[...] = part
        elif jj < n_ff - 1:
            acc[...] += part
        else:
            ybuf[...] = (acc[...] + part) * wrow_ref[...]

    for jj in range(n_ff):
        pl.when((i < n_used) & (j == jj))(functools.partial(ff_step, jj))

    @pl.when((i == n_used) & (j == 0))
    def _():
        start_scatter_all()
        wait_scatter()
        wait_gather()


def _moe(hn, tile_e, n_used, tok0, tok_next, dst_prev, w_row, w_gate, w_up, w_down):
    t = hn.shape[0]
    d_ff = w_gate.shape[2]
    tm, fc = MOE_TM, MOE_FC
    n_steps = tok_next.shape[0]
    n_ff = d_ff // fc
    assert n_ff >= 2 and n_ff * fc == d_ff and tm % n_ff == 0
    last_tile = n_steps - 2

    def ff_step(i, j, nu):
        return jnp.where(i < nu[0], j, n_ff - 1)

    grid_spec = pltpu.PrefetchScalarGridSpec(
        num_scalar_prefetch=2,
        grid=(n_steps, n_ff),
        in_specs=[
            pl.BlockSpec((1, 1, tm), lambda i, j, te, nu: (0, 0, 0), memory_space=pltpu.SMEM),
            pl.BlockSpec((1, 1, tm), lambda i, j, te, nu: (i, 0, 0), memory_space=pltpu.SMEM),
            pl.BlockSpec((1, 1, tm), lambda i, j, te, nu: (i, 0, 0), memory_space=pltpu.SMEM),
            pl.BlockSpec((tm, 1), lambda i, j, te, nu: (jnp.minimum(i, last_tile), 0)),
            pl.BlockSpec(memory_space=pl.ANY),
            pl.BlockSpec((1, D_MODEL, fc), lambda i, j, te, nu: (te[i], 0, ff_step(i, j, nu))),
            pl.BlockSpec((1, D_MODEL, fc), lambda i, j, te, nu: (te[i], 0, ff_step(i, j, nu))),
            pl.BlockSpec((1, fc, D_MODEL), lambda i, j, te, nu: (te[i], ff_step(i, j, nu), 0)),
        ],
        out_specs=pl.BlockSpec(memory_space=pl.ANY),
        scratch_shapes=[
            pltpu.VMEM((tm, D_MODEL), F32),
            pltpu.VMEM((tm, D_MODEL), BF16),
            pltpu.VMEM((tm, D_MODEL), F32),
            pltpu.VMEM((tm, D_MODEL), F32),
            pltpu.SemaphoreType.DMA(()),
            pltpu.SemaphoreType.DMA(()),
        ],
    )
    return pl.pallas_call(
        functools.partial(_moe_kernel, n_ff=n_ff),
        grid_spec=grid_spec,
        out_shape=jax.ShapeDtypeStruct((TOP_K * t + tm, D_MODEL), F32),
        compiler_params=_params("arbitrary", "arbitrary"),
        name="moe",
    )(tile_e, n_used, tok0, tok_next, dst_prev, w_row, hn, w_gate, w_up, w_down)


def _final_kernel(h_ref, ya_ref, yb_ref, g_ref, o_ref):
    o_ref[...] = _rms(h_ref[...] + (ya_ref[...] + yb_ref[...]), g_ref[...])


def _final(h3, y2, g):
    t = h3.shape[0]
    tm = ROW_TM
    second = t // tm
    row = pl.BlockSpec((tm, D_MODEL), lambda i: (i, 0))
    return pl.pallas_call(
        _final_kernel,
        grid=(t // tm,),
        in_specs=[row, row, pl.BlockSpec((tm, D_MODEL), lambda i: (i + second, 0)),
                  pl.BlockSpec((1, D_MODEL), lambda i: (0, 0))],
        out_specs=row,
        out_shape=jax.ShapeDtypeStruct((t, D_MODEL), F32),
        compiler_params=_params("arbitrary"),
        name="final_norm",
    )(h3, y2, y2, g)


def _rotary_tables(seq):
    half = RET_QK_DIM // 2
    inv = ROPE_BASE ** (-jnp.arange(half, dtype=F32) / half)
    ang = jnp.arange(seq, dtype=F32)[:, None] * inv[None, :]
    return jnp.cos(ang), jnp.sin(ang)


def _channel_dft_tables(n, scale):
    k = jnp.arange(n, dtype=jnp.int32)
    ang = ((k[:, None] * k[None, :]) % n).astype(F32) * (2.0 * math.pi / n)
    return (jnp.cos(ang) * scale).astype(BF16), (-jnp.sin(ang) * scale).astype(BF16)


def _sequence_dft_tables(seq, scale):
    blk = FNET_SEQ_BLOCK
    n1 = seq // blk
    k1 = jnp.arange(n1, dtype=jnp.int32)[:, None, None]
    k2 = jnp.arange(blk, dtype=jnp.int32)[None, :, None]
    s2 = jnp.arange(blk, dtype=jnp.int32)[None, None, :]
    ang = (((k1 + n1 * k2) * s2) % seq).astype(F32) * (2.0 * math.pi / seq)
    gc = (jnp.cos(ang) * scale).astype(BF16).reshape(n1 * blk, blk)
    gs = (jnp.sin(ang) * scale).astype(BF16).reshape(n1 * blk, blk)
    return gc, gs


def kernel(x, mix_norm, ffn_norm, ret_w_in, ret_decay_logit, ret_gn_gain, ret_w_out,
           dense_w_gate, dense_w_up, dense_w_down, fnet_w_out, moe_router,
           moe_w_gate, moe_w_up, moe_w_down, final_norm):
    batch, seq, d = x.shape
    t = batch * seq
    n1 = seq // FNET_SEQ_BLOCK
    assert d == D_MODEL and seq % RET_CHUNK == 0 and seq % PROJ_TM == 0
    assert n1 * FNET_SEQ_BLOCK == seq and n1 & (n1 - 1) == 0
    x2d = x.reshape(t, d)

    cos, sin = _rotary_tables(seq)
    proj = _proj(x2d, mix_norm[0:1], cos, sin, ret_w_in[0].astype(BF16), seq)
    log_gamma = jax.nn.log_sigmoid(ret_decay_logit[0].astype(F32))
    z = _retention(proj, log_gamma, ret_gn_gain[0:1], batch, seq)
    h2 = _ffn0(x2d, z, ret_w_out[0].astype(BF16), ffn_norm[0:1],
               dense_w_gate[0].astype(BF16), dense_w_up[0].astype(BF16), dense_w_down[0].astype(BF16))

    ch_scale = FNET_GROUP_DIM ** -0.5
    cc, sc = _channel_dft_tables(FNET_GROUP_DIM, ch_scale)
    gc, gs = _sequence_dft_tables(seq, seq ** -0.5)
    zr, zi = _fnet_channels(h2, mix_norm[1:2], cc, sc)
    y = _fnet_sequence(zr, zi, gc, gs, batch, seq)
    r_hi = moe_router[0].astype(BF16)
    r_lo = (moe_router[0] - r_hi.astype(F32)).astype(BF16)
    h3, hn3, top_idx, top_w = _router(h2, y, fnet_w_out[0].astype(BF16), ffn_norm[1:2], r_hi, r_lo)

    tile_e, n_used, tok0, tok_next, dst_prev, w_row = _routing_tables(top_idx, top_w, MOE_TM)
    y2 = _moe(hn3, tile_e, n_used, tok0, tok_next, dst_prev, w_row,
              moe_w_gate[0].astype(BF16), moe_w_up[0].astype(BF16), moe_w_down[0].astype(BF16))
    out = _final(h3, y2, final_norm.reshape(1, d))
    return out.reshape(batch, seq, d)
```

```python
import functools
import math

import jax
import jax.numpy as jnp
from jax import lax
from jax.experimental import pallas as pl
from jax.experimental.pallas import tpu as pltpu

F32 = jnp.float32
BF16 = jnp.bfloat16

D_MODEL = 1024
RET_HEADS = 4
RET_QK_DIM = D_MODEL // RET_HEADS
RET_V_DIM = 2 * RET_QK_DIM
RET_QK_WIDTH = RET_HEADS * RET_QK_DIM
RET_V_WIDTH = RET_HEADS * RET_V_DIM
RET_IN_WIDTH = 2 * RET_QK_WIDTH + 2 * RET_V_WIDTH
ROPE_BASE = 10000.0
FNET_GROUPS = 4
FNET_GROUP_DIM = D_MODEL // FNET_GROUPS
N_EXPERTS = 8
TOP_K = 2
EPS = 1e-6

RET_CHUNK = 256
VMEM_LIMIT_BYTES = 56 * 1024 * 1024

PROJ_TM, PROJ_TN = 1024, 2048
FFN0_TM, FFN0_SUB = 512, 512
ROW_TM = 1024
FNET_SEQ_BLOCK = 256
FNET_DT = 256
FNET_ROW_CHUNK = 16
MOE_TM, MOE_FC, MOE_SUB = 1024, 1792, 512


def _params(*semantics):
    return pltpu.CompilerParams(dimension_semantics=semantics, vmem_limit_bytes=VMEM_LIMIT_BYTES)


def _dot(a, b):
    return jnp.dot(a, b, preferred_element_type=F32)


def _rms(x, g):
    return x * lax.rsqrt(jnp.mean(x * x, axis=-1, keepdims=True) + EPS) * g


def _silu(a):
    return a * (1.0 / (1.0 + jnp.exp(-a)))


def _proj_kernel(x_ref, g_ref, cos_ref, sin_ref, w_ref, o_ref, hn_ref):
    j = pl.program_id(1)

    @pl.when(j == 0)
    def _():
        hn_ref[...] = _rms(x_ref[...], g_ref[...]).astype(BF16)

    tn = o_ref.shape[1]
    qk_tiles = 2 * RET_QK_WIDTH // tn
    v_tiles = RET_V_WIDTH // tn
    hn = hn_ref[...]
    half = RET_QK_DIM // 2

    def head(h):
        return _dot(hn, w_ref[:, h * RET_QK_DIM:(h + 1) * RET_QK_DIM])

    @pl.when(j < qk_tiles)
    def _():
        for h in range(tn // RET_QK_DIM):
            lo = h * RET_QK_DIM
            is_k = j * tn + lo >= RET_QK_WIDTH
            scale = jnp.where(is_k, RET_QK_DIM ** -0.5, 1.0).astype(F32)
            cos = cos_ref[...] * scale
            sin = sin_ref[...] * scale
            y = head(h)
            t1 = y[:, :half]
            t2 = y[:, half:]
            o_ref[:, lo:lo + half] = (t1 * cos - t2 * sin).astype(BF16)
            o_ref[:, lo + half:lo + RET_QK_DIM] = (t1 * sin + t2 * cos).astype(BF16)

    @pl.when((j >= qk_tiles) & (j < qk_tiles + v_tiles))
    def _():
        for h in range(tn // RET_QK_DIM):
            o_ref[:, h * RET_QK_DIM:(h + 1) * RET_QK_DIM] = head(h).astype(BF16)

    @pl.when(j >= qk_tiles + v_tiles)
    def _():
        for h in range(tn // RET_QK_DIM):
            o_ref[:, h * RET_QK_DIM:(h + 1) * RET_QK_DIM] = _silu(head(h)).astype(BF16)


def _proj(x2d, g, cos, sin, w_in, seq):
    t = x2d.shape[0]
    tm, tn = PROJ_TM, PROJ_TN
    pos_tiles = seq // tm
    return pl.pallas_call(
        _proj_kernel,
        grid=(t // tm, RET_IN_WIDTH // tn),
        in_specs=[
            pl.BlockSpec((tm, D_MODEL), lambda i, j: (i, 0)),
            pl.BlockSpec((1, D_MODEL), lambda i, j: (0, 0)),
            pl.BlockSpec((tm, RET_QK_DIM // 2), lambda i, j: (i % pos_tiles, 0)),
            pl.BlockSpec((tm, RET_QK_DIM // 2), lambda i, j: (i % pos_tiles, 0)),
            pl.BlockSpec((D_MODEL, tn), lambda i, j: (0, j)),
        ],
        out_specs=pl.BlockSpec((tm, tn), lambda i, j: (i, j)),
        out_shape=jax.ShapeDtypeStruct((t, RET_IN_WIDTH), BF16),
        scratch_shapes=[pltpu.VMEM((tm, D_MODEL), BF16)],
        compiler_params=_params("arbitrary", "arbitrary"),
        name="proj",
    )(x2d, g, cos, sin, w_in)


def _ret_kernel(lg_ref, q_ref, k_ref, v_ref, g_ref, gain_ref, o_ref, sfa_ref, sba_ref, sf_ref, sb_ref):
    c = RET_CHUNK
    n_chunks = q_ref.shape[0] // c
    h = pl.program_id(1)
    lgf = lg_ref[0, h]
    lgb = lg_ref[1, h]

    row = lax.broadcasted_iota(jnp.int32, (c, RET_QK_DIM), 0).astype(F32)
    xi_f = jnp.exp((row + 1.0) * lgf).astype(BF16)
    zeta_f = jnp.exp((c - 1.0 - row) * lgf).astype(BF16)
    xi_b = jnp.exp((c - row) * lgb).astype(BF16)
    zeta_b = jnp.exp(row * lgb).astype(BF16)
    g_f = jnp.exp(jnp.full((1, RET_V_DIM), c, F32) * lgf)
    g_b = jnp.exp(jnp.full((1, RET_V_DIM), c, F32) * lgb)
    di = lax.broadcasted_iota(jnp.int32, (c, c), 0)
    dj = lax.broadcasted_iota(jnp.int32, (c, c), 1)
    diff = (di - dj).astype(F32)
    decay = jnp.exp(jnp.where(diff >= 0, diff * lgf, -diff * lgb))

    def chunk(n):
        r0 = pl.multiple_of(n * c, c)
        return pl.ds(r0, c)

    def scaled(t, s):
        return t * s

    def outer(kz, vn):
        return lax.dot_general(kz, vn, (((0,), (0,)), ((), ())), preferred_element_type=F32)

    sf_ref[...] = jnp.zeros_like(sf_ref)
    sb_ref[...] = jnp.zeros_like(sb_ref)

    def states(step, carry):
        nf = step
        nb = n_chunks - 1 - step
        sf = sf_ref[...]
        sb = sb_ref[...]
        sfa_ref[nf] = sf.astype(BF16)
        sba_ref[nb] = sb.astype(BF16)
        rf, rb = chunk(nf), chunk(nb)
        sf_ref[...] = sf * g_f + outer(scaled(k_ref[rf, :], zeta_f), v_ref[rf, :])
        sb_ref[...] = sb * g_b + outer(scaled(k_ref[rb, :], zeta_b), v_ref[rb, :])
        return carry

    lax.fori_loop(0, n_chunks, states, 0, unroll=2)

    gain = gain_ref[...]

    def fwd(n, carry):
        rows = chunk(n)
        qn, kn, vn = q_ref[rows, :], k_ref[rows, :], v_ref[rows, :]
        scores = lax.dot_general(qn, kn, (((1,), (1,)), ((), ())), preferred_element_type=F32) * decay
        y = (_dot(scores.astype(BF16), vn) + _dot(scaled(qn, xi_f), sfa_ref[n])
             + _dot(scaled(qn, xi_b), sba_ref[n]))
        mu = jnp.mean(y, axis=-1, keepdims=True)
        yc = y - mu
        var = jnp.mean(yc * yc, axis=-1, keepdims=True)
        yn = yc * lax.rsqrt(var + EPS) * gain
        o_ref[rows, :] = g_ref[rows, :] * yn.astype(BF16)
        return carry

    lax.fori_loop(0, n_chunks, fwd, 0, unroll=2)


def _retention(proj, log_gamma, gain, batch, seq):
    t = proj.shape[0]
    qk_blocks = RET_QK_WIDTH // RET_QK_DIM
    k_off = qk_blocks
    v_off = 2 * RET_QK_WIDTH // RET_V_DIM
    g_off = v_off + RET_V_WIDTH // RET_V_DIM
    return pl.pallas_call(
        _ret_kernel,
        grid=(batch, RET_HEADS),
        in_specs=[
            pl.BlockSpec(memory_space=pltpu.SMEM),
            pl.BlockSpec((seq, RET_QK_DIM), lambda b, h: (b, h)),
            pl.BlockSpec((seq, RET_QK_DIM), lambda b, h: (b, k_off + h)),
            pl.BlockSpec((seq, RET_V_DIM), lambda b, h: (b, v_off + h)),
            pl.BlockSpec((seq, RET_V_DIM), lambda b, h: (b, g_off + h)),
            pl.BlockSpec((1, RET_V_DIM), lambda b, h: (0, h)),
        ],
        out_specs=pl.BlockSpec((seq, RET_V_DIM), lambda b, h: (b, h)),
        out_shape=jax.ShapeDtypeStruct((t, RET_V_WIDTH), BF16),
        scratch_shapes=[
            pltpu.VMEM((seq // RET_CHUNK, RET_QK_DIM, RET_V_DIM), BF16),
            pltpu.VMEM((seq // RET_CHUNK, RET_QK_DIM, RET_V_DIM), BF16),
            pltpu.VMEM((RET_QK_DIM, RET_V_DIM), F32),
            pltpu.VMEM((RET_QK_DIM, RET_V_DIM), F32),
        ],
        compiler_params=_params("arbitrary", "arbitrary"),
        name="retention",
    )(log_gamma, proj, proj, proj, proj, gain)


def _ffn0_kernel(x_ref, z_ref, wo_ref, g_ref, wg_ref, wu_ref, wd_ref, g2_ref, cc_ref, sc_ref,
                 o_ref, zr_ref, zi_ref):
    h1 = x_ref[...] + _dot(z_ref[...], wo_ref[...])
    hn = _rms(h1, g_ref[...]).astype(BF16)
    d_ff = wg_ref.shape[1]
    part = None
    for lo in range(0, d_ff, FFN0_SUB):
        cols = slice(lo, min(lo + FFN0_SUB, d_ff))
        hidden = (_silu(_dot(hn, wg_ref[:, cols])) * _dot(hn, wu_ref[:, cols])).astype(BF16)
        d = _dot(hidden, wd_ref[cols, :])
        part = d if part is None else part + d
    h2 = h1 + part
    o_ref[...] = h2
    hn2 = _rms(h2, g2_ref[...]).astype(BF16)
    for grp in range(FNET_GROUPS):
        cols = slice(grp * FNET_GROUP_DIM, (grp + 1) * FNET_GROUP_DIM)
        zr_ref[:, cols] = _dot(hn2[:, cols], cc_ref[...]).astype(BF16)
        zi_ref[:, cols] = _dot(hn2[:, cols], sc_ref[...]).astype(BF16)


def _ffn0(x2d, z, w_out, g, w_gate, w_up, w_down, g2, cc, sc):
    t = x2d.shape[0]
    d_ff = w_gate.shape[1]
    tm = FFN0_TM
    once = pl.Buffered(1)
    row = pl.BlockSpec((tm, D_MODEL), lambda i: (i, 0))
    return pl.pallas_call(
        _ffn0_kernel,
        grid=(t // tm,),
        in_specs=[
            row,
            pl.BlockSpec((tm, RET_V_WIDTH), lambda i: (i, 0)),
            pl.BlockSpec((RET_V_WIDTH, D_MODEL), lambda i: (0, 0), pipeline_mode=once),
            pl.BlockSpec((1, D_MODEL), lambda i: (0, 0), pipeline_mode=once),
            pl.BlockSpec((D_MODEL, d_ff), lambda i: (0, 0), pipeline_mode=once),
            pl.BlockSpec((D_MODEL, d_ff), lambda i: (0, 0), pipeline_mode=once),
            pl.BlockSpec((d_ff, D_MODEL), lambda i: (0, 0), pipeline_mode=once),
            pl.BlockSpec((1, D_MODEL), lambda i: (0, 0), pipeline_mode=once),
            pl.BlockSpec((FNET_GROUP_DIM, FNET_GROUP_DIM), lambda i: (0, 0), pipeline_mode=once),
            pl.BlockSpec((FNET_GROUP_DIM, FNET_GROUP_DIM), lambda i: (0, 0), pipeline_mode=once),
        ],
        out_specs=[row, row, row],
        out_shape=[jax.ShapeDtypeStruct((t, D_MODEL), F32), jax.ShapeDtypeStruct((t, D_MODEL), BF16),
                   jax.ShapeDtypeStruct((t, D_MODEL), BF16)],
        compiler_params=_params("arbitrary"),
        name="ffn0",
    )(x2d, z, w_out, g, w_gate, w_up, w_down, g2, cc, sc)


def _dft_across_blocks(xs):
    n = len(xs)
    if n == 1:
        return xs
    even = _dft_across_blocks(xs[0::2])
    odd = _dft_across_blocks(xs[1::2])
    out = [None] * n
    for k in range(n // 2):
        er, ei = even[k]
        o_r, o_i = odd[k]
        if k == 0:
            tr, ti = o_r, o_i
            out[k], out[k + n // 2] = (er + tr, ei + ti), (er - tr, ei - ti)
        elif 4 * k == n:
            out[k], out[k + n // 2] = (er + o_i, ei - o_r), (er - o_i, ei + o_r)
        else:
            c, s = math.cos(2.0 * math.pi * k / n), math.sin(2.0 * math.pi * k / n)
            tr = o_r * c + o_i * s
            ti = o_i * c - o_r * s
            out[k], out[k + n // 2] = (er + tr, ei + ti), (er - tr, ei - ti)
    return out


def _fnet_seq_kernel(zr_ref, zi_ref, gc_ref, gs_ref, o_ref, ar_ref, ai_ref, ys_ref):
    blk = FNET_SEQ_BLOCK
    n1 = zr_ref.shape[0] // blk
    rc = FNET_ROW_CHUNK
    lanes = 128

    def body(r, carry):
        r0 = pl.multiple_of(r * rc, rc)
        for l in range(zr_ref.shape[1] // lanes):
            cols = slice(l * lanes, (l + 1) * lanes)
            zs = [(zr_ref[pl.ds(s1 * blk + r0, rc), cols].astype(F32),
                   zi_ref[pl.ds(s1 * blk + r0, rc), cols].astype(F32)) for s1 in range(n1)]
            for k1, (a_r, a_i) in enumerate(_dft_across_blocks(zs)):
                ar_ref[pl.ds(k1 * blk + r0, rc), cols] = a_r.astype(BF16)
                ai_ref[pl.ds(k1 * blk + r0, rc), cols] = a_i.astype(BF16)
        return carry

    lax.fori_loop(0, blk // rc, body, 0)

    for k1 in range(n1):
        rows = slice(k1 * blk, (k1 + 1) * blk)
        y = _dot(gc_ref[rows, :], ar_ref[rows, :]) + _dot(gs_ref[rows, :], ai_ref[rows, :])
        for l in range(y.shape[1] // lanes):
            ys_ref[l, pl.ds(k1, blk, stride=n1), :] = y[:, l * lanes:(l + 1) * lanes]
    for l in range(o_ref.shape[1] // lanes):
        o_ref[:, l * lanes:(l + 1) * lanes] = ys_ref[l]


def _fnet_sequence(zr, zi, gc, gs, batch, seq):
    t = zr.shape[0]
    dt = FNET_DT
    blkspec = pl.BlockSpec((seq, dt), lambda b, c: (b, c))
    tab = pl.BlockSpec((seq, FNET_SEQ_BLOCK), lambda b, c: (0, 0))
    return pl.pallas_call(
        _fnet_seq_kernel,
        grid=(batch, D_MODEL // dt),
        in_specs=[blkspec, blkspec, tab, tab],
        out_specs=blkspec,
        out_shape=jax.ShapeDtypeStruct((t, D_MODEL), F32),
        scratch_shapes=[pltpu.VMEM((seq, dt), BF16), pltpu.VMEM((seq, dt), BF16),
                        pltpu.VMEM((dt // 128, seq, 128), F32)],
        compiler_params=_params("arbitrary", "arbitrary"),
        name="fnet_sequence",
    )(zr, zi, gc, gs)


def _router_kernel(h_ref, y_ref, w_ref, g_ref, rhi_ref, rlo_ref, h3_ref, hn_ref, idx_ref, wt_ref):
    h3 = h_ref[...] + _dot(y_ref[...].astype(BF16), w_ref[...])
    h3_ref[...] = h3
    hn = _rms(h3, g_ref[...])
    hn_ref[...] = hn
    hi = hn.astype(BF16)
    lo = (hn - hi.astype(F32)).astype(BF16)
    logits = _dot(hi, rhi_ref[...]) + (_dot(lo, rhi_ref[...]) + _dot(hi, rlo_ref[...]))
    col = lax.broadcasted_iota(jnp.int32, logits.shape, 1)
    m1 = jnp.max(logits, axis=-1, keepdims=True)
    i1 = jnp.min(jnp.where(logits == m1, col, N_EXPERTS), axis=-1, keepdims=True)
    rest = jnp.where(col == i1, -jnp.inf, logits)
    m2 = jnp.max(rest, axis=-1, keepdims=True)
    i2 = jnp.min(jnp.where(rest == m2, col, N_EXPERTS), axis=-1, keepdims=True)
    e2 = jnp.exp(m2 - m1)
    inv = 1.0 / (1.0 + e2)
    idx_ref[:, 0:1] = i1
    idx_ref[:, 1:2] = i2
    wt_ref[:, 0:1] = inv
    wt_ref[:, 1:2] = e2 * inv


def _router(h2, y, w_fnet, g, r_hi, r_lo):
    t = h2.shape[0]
    tm = ROW_TM
    row = pl.BlockSpec((tm, D_MODEL), lambda i: (i, 0))
    pair = pl.BlockSpec((tm, TOP_K), lambda i: (i, 0))
    rtr = pl.BlockSpec((D_MODEL, N_EXPERTS), lambda i: (0, 0))
    return pl.pallas_call(
        _router_kernel,
        grid=(t // tm,),
        in_specs=[row, row, pl.BlockSpec((D_MODEL, D_MODEL), lambda i: (0, 0)),
                  pl.BlockSpec((1, D_MODEL), lambda i: (0, 0)), rtr, rtr],
        out_specs=[row, row, pair, pair],
        out_shape=[jax.ShapeDtypeStruct((t, D_MODEL), F32), jax.ShapeDtypeStruct((t, D_MODEL), F32),
                   jax.ShapeDtypeStruct((t, TOP_K), jnp.int32), jax.ShapeDtypeStruct((t, TOP_K), F32)],
        compiler_params=_params("arbitrary"),
        name="router",
    )(h2, y, w_fnet, g, r_hi, r_lo)


def _routing_tables(top_idx, top_w, tm):
    t = top_idx.shape[0]
    n_pairs = TOP_K * t
    n_tiles = n_pairs // tm + N_EXPERTS
    e_flat = top_idx.T.reshape(n_pairs)
    w_flat = top_w.T.reshape(n_pairs)
    order = jnp.argsort(e_flat, stable=True).astype(jnp.int32)
    experts = jnp.arange(N_EXPERTS, dtype=jnp.int32)
    counts = jnp.sum((e_flat[:, None] == experts[None, :]).astype(jnp.int32), axis=0)
    off = jnp.cumsum(counts) - counts
    tiles_e = (counts + tm - 1) // tm
    tile_end = jnp.cumsum(tiles_e)
    tile_off = tile_end - tiles_e
    n_used = tile_end[-1]
    last_e = jnp.max(jnp.where(tiles_e > 0, experts, 0))
    tile_ids = jnp.arange(n_tiles, dtype=jnp.int32)
    tile_e = jnp.minimum(jnp.sum((tile_ids[:, None] >= tile_end[None, :]).astype(jnp.int32), axis=1), last_e)
    u = jnp.arange(tm, dtype=jnp.int32)
    rank = (tile_ids - tile_off[tile_e])[:, None] * tm + u[None, :]
    valid = (tile_ids < n_used)[:, None] & (rank < counts[tile_e][:, None])
    src = order[jnp.clip(off[tile_e][:, None] + rank, 0, n_pairs - 1)]
    tok = jnp.where(valid, src % t, 0).astype(jnp.int32)
    spare = n_pairs + jnp.broadcast_to(u[None, :], (1, tm))
    dst = jnp.where(valid, src, spare).astype(jnp.int32)
    w_row = jnp.where(valid, w_flat[src], 0.0).astype(F32)
    tok_next = jnp.concatenate([tok[1:], jnp.zeros((2, tm), jnp.int32)], axis=0)
    dst_prev = jnp.concatenate([spare.astype(jnp.int32), dst], axis=0)
    tile_e = jnp.concatenate([tile_e, tile_e[-1:]]).astype(jnp.int32)
    return (tile_e, n_used.reshape(1).astype(jnp.int32), tok[0].reshape(1, 1, tm),
            tok_next.reshape(n_tiles + 1, 1, tm), dst_prev.reshape(n_tiles + 1, 1, tm),
            w_row.reshape(n_tiles * tm, 1))


def _moe_kernel(te_ref, nu_ref, tok0_ref, tokn_ref, dstp_ref, wrow_ref, hn_hbm, wg_ref, wu_ref, wd_ref,
                y2_hbm, xbuf, xb, acc, ybuf, gsem, ssem, *, n_ff):
    del te_ref
    i = pl.program_id(0)
    j = pl.program_id(1)
    n_used = nu_ref[0]
    tm = xbuf.shape[0]
    fc = wg_ref.shape[2]
    spare0 = y2_hbm.shape[0] - tm
    unroll = 8

    def gather_copy(tok, u):
        return pltpu.make_async_copy(hn_hbm.at[pl.ds(tok, 1), :], xbuf.at[pl.ds(u, 1), :], gsem)

    def scatter_copy(u, row):
        return pltpu.make_async_copy(ybuf.at[pl.ds(u, 1), :], y2_hbm.at[pl.ds(row, 1), :], ssem)

    def start_gather(ref):
        def body(u, carry):
            gather_copy(ref[0, 0, u], u).start()
            return carry
        lax.fori_loop(0, tm, body, 0, unroll=unroll)

    def wait_gather():
        pltpu.make_async_copy(hn_hbm.at[pl.ds(0, tm), :], xbuf, gsem).wait()

    def start_scatter_all():
        def body(u, carry):
            scatter_copy(u, dstp_ref[0, 0, u]).start()
            return carry
        lax.fori_loop(0, tm, body, 0, unroll=unroll)

    def wait_scatter():
        pltpu.make_async_copy(ybuf, y2_hbm.at[pl.ds(0, tm), :], ssem).wait()

    @pl.when((i == 0) & (j == 0))
    def _():
        ybuf[...] = jnp.zeros_like(ybuf)
        fill = pltpu.make_async_copy(ybuf, y2_hbm.at[pl.ds(spare0, tm), :], ssem)
        fill.start()
        fill.wait()
        start_gather(tok0_ref)

    def ff_step(jj):
        if jj == 0:
            wait_gather()
            xb[...] = xbuf[...].astype(BF16)
            for u in range(tm):
                scatter_copy(u, dstp_ref[0, 0, u]).start(priority=1)
                gather_copy(tokn_ref[0, 0, u], u).start()
        if jj == n_ff - 1:
            wait_scatter()

        x = xb[...]
        part = None
        for lo in range(0, fc, MOE_SUB):
            cols = slice(lo, min(lo + MOE_SUB, fc))
            hidden = (_silu(_dot(x, wg_ref[0, :, cols])) * _dot(x, wu_ref[0, :, cols])).astype(BF16)
            d = _dot(hidden, wd_ref[0, cols, :])
            part = d if part is None else part + d

        if jj == 0:
            acc[...] = part
        elif jj < n_ff - 1:
            acc[...] += part
        else:
            ybuf[...] = (acc[...] + part) * wrow_ref[...]

    for jj in range(n_ff):
        pl.when((i < n_used) & (j == jj))(functools.partial(ff_step, jj))

    @pl.when((i == n_used) & (j == 0))
    def _():
        start_scatter_all()
        wait_scatter()
        wait_gather()


def _moe(hn, tile_e, n_used, tok0, tok_next, dst_prev, w_row, w_gate, w_up, w_down):
    t = hn.shape[0]
    d_ff = w_gate.shape[2]
    tm, fc = MOE_TM, MOE_FC
    n_steps = tok_next.shape[0]
    n_ff = d_ff // fc
    assert n_ff >= 2 and n_ff * fc == d_ff and tm % n_ff == 0
    last_tile = n_steps - 2

    def ff_step(i, j, nu):
        return jnp.where(i < nu[0], j, n_ff - 1)

    grid_spec = pltpu.PrefetchScalarGridSpec(
        num_scalar_prefetch=2,
        grid=(n_steps, n_ff),
        in_specs=[
            pl.BlockSpec((1, 1, tm), lambda i, j, te, nu: (0, 0, 0), memory_space=pltpu.SMEM),
            pl.BlockSpec((1, 1, tm), lambda i, j, te, nu: (i, 0, 0), memory_space=pltpu.SMEM),
            pl.BlockSpec((1, 1, tm), lambda i, j, te, nu: (i, 0, 0), memory_space=pltpu.SMEM),
            pl.BlockSpec((tm, 1), lambda i, j, te, nu: (jnp.minimum(i, last_tile), 0)),
            pl.BlockSpec(memory_space=pl.ANY),
            pl.BlockSpec((1, D_MODEL, fc), lambda i, j, te, nu: (te[i], 0, ff_step(i, j, nu))),
            pl.BlockSpec((1, D_MODEL, fc), lambda i, j, te, nu: (te[i], 0, ff_step(i, j, nu))),
            pl.BlockSpec((1, fc, D_MODEL), lambda i, j, te, nu: (te[i], ff_step(i, j, nu), 0)),
        ],
        out_specs=pl.BlockSpec(memory_space=pl.ANY),
        scratch_shapes=[
            pltpu.VMEM((tm, D_MODEL), F32),
            pltpu.VMEM((tm, D_MODEL), BF16),
            pltpu.VMEM((tm, D_MODEL), F32),
            pltpu.VMEM((tm, D_MODEL), F32),
            pltpu.SemaphoreType.DMA(()),
            pltpu.SemaphoreType.DMA(()),
        ],
    )
    return pl.pallas_call(
        functools.partial(_moe_kernel, n_ff=n_ff),
        grid_spec=grid_spec,
        out_shape=jax.ShapeDtypeStruct((TOP_K * t + tm, D_MODEL), F32),
        compiler_params=_params("arbitrary", "arbitrary"),
        name="moe",
    )(tile_e, n_used, tok0, tok_next, dst_prev, w_row, hn, w_gate, w_up, w_down)


def _final_kernel(h_ref, ya_ref, yb_ref, g_ref, o_ref):
    o_ref[...] = _rms(h_ref[...] + (ya_ref[...] + yb_ref[...]), g_ref[...])


def _final(h3, y2, g):
    t = h3.shape[0]
    tm = ROW_TM
    second = t // tm
    row = pl.BlockSpec((tm, D_MODEL), lambda i: (i, 0))
    return pl.pallas_call(
        _final_kernel,
        grid=(t // tm,),
        in_specs=[row, row, pl.BlockSpec((tm, D_MODEL), lambda i: (i + second, 0)),
                  pl.BlockSpec((1, D_MODEL), lambda i: (0, 0))],
        out_specs=row,
        out_shape=jax.ShapeDtypeStruct((t, D_MODEL), F32),
        compiler_params=_params("arbitrary"),
        name="final_norm",
    )(h3, y2, y2, g)


def _rotary_tables(seq):
    half = RET_QK_DIM // 2
    inv = ROPE_BASE ** (-jnp.arange(half, dtype=F32) / half)
    ang = jnp.arange(seq, dtype=F32)[:, None] * inv[None, :]
    return jnp.cos(ang), jnp.sin(ang)


def _channel_dft_tables(n, scale):
    k = jnp.arange(n, dtype=jnp.int32)
    ang = ((k[:, None] * k[None, :]) % n).astype(F32) * (2.0 * math.pi / n)
    return (jnp.cos(ang) * scale).astype(BF16), (-jnp.sin(ang) * scale).astype(BF16)


def _sequence_dft_tables(seq, scale):
    blk = FNET_SEQ_BLOCK
    n1 = seq // blk
    k1 = jnp.arange(n1, dtype=jnp.int32)[:, None, None]
    k2 = jnp.arange(blk, dtype=jnp.int32)[None, :, None]
    s2 = jnp.arange(blk, dtype=jnp.int32)[None, None, :]
    ang = (((k1 + n1 * k2) * s2) % seq).astype(F32) * (2.0 * math.pi / seq)
    gc = (jnp.cos(ang) * scale).astype(BF16).reshape(n1 * blk, blk)
    gs = (jnp.sin(ang) * scale).astype(BF16).reshape(n1 * blk, blk)
    return gc, gs


def kernel(x, mix_norm, ffn_norm, ret_w_in, ret_decay_logit, ret_gn_gain, ret_w_out,
           dense_w_gate, dense_w_up, dense_w_down, fnet_w_out, moe_router,
           moe_w_gate, moe_w_up, moe_w_down, final_norm):
    batch, seq, d = x.shape
    t = batch * seq
    n1 = seq // FNET_SEQ_BLOCK
    assert d == D_MODEL and seq % RET_CHUNK == 0 and seq % PROJ_TM == 0
    assert n1 * FNET_SEQ_BLOCK == seq and n1 & (n1 - 1) == 0
    x2d = x.reshape(t, d)

    cos, sin = _rotary_tables(seq)
    proj = _proj(x2d, mix_norm[0:1], cos, sin, ret_w_in[0].astype(BF16), seq)
    log_gamma = jax.nn.log_sigmoid(ret_decay_logit[0].astype(F32))
    z = _retention(proj, log_gamma, ret_gn_gain[0:1], batch, seq)
    cc, sc = _channel_dft_tables(FNET_GROUP_DIM, FNET_GROUP_DIM ** -0.5)
    gc, gs = _sequence_dft_tables(seq, seq ** -0.5)
    h2, zr, zi = _ffn0(x2d, z, ret_w_out[0].astype(BF16), ffn_norm[0:1],
                       dense_w_gate[0].astype(BF16), dense_w_up[0].astype(BF16),
                       dense_w_down[0].astype(BF16), mix_norm[1:2], cc, sc)

    y = _fnet_sequence(zr, zi, gc, gs, batch, seq)
    r_hi = moe_router[0].astype(BF16)
    r_lo = (moe_router[0] - r_hi.astype(F32)).astype(BF16)
    h3, hn3, top_idx, top_w = _router(h2, y, fnet_w_out[0].astype(BF16), ffn_norm[1:2], r_hi, r_lo)

    tile_e, n_used, tok0, tok_next, dst_prev, w_row = _routing_tables(top_idx, top_w, MOE_TM)
    y2 = _moe(hn3, tile_e, n_used, tok0, tok_next, dst_prev, w_row,
              moe_w_gate[0].astype(BF16), moe_w_up[0].astype(BF16), moe_w_down[0].astype(BF16))
    out = _final(h3, y2, final_norm.reshape(1, d))
    return out.reshape(batch, seq, d)
```

```python
import functools
import math

import jax
import jax.numpy as jnp
from jax import lax
from jax.experimental import pallas as pl
from jax.experimental.pallas import tpu as pltpu

F32 = jnp.float32
BF16 = jnp.bfloat16

D_MODEL = 1024
RET_HEADS = 4
RET_QK_DIM = D_MODEL // RET_HEADS
RET_V_DIM = 2 * RET_QK_DIM
RET_QK_WIDTH = RET_HEADS * RET_QK_DIM
RET_V_WIDTH = RET_HEADS * RET_V_DIM
RET_IN_WIDTH = 2 * RET_QK_WIDTH + 2 * RET_V_WIDTH
ROPE_BASE = 10000.0
FNET_GROUPS = 4
FNET_GROUP_DIM = D_MODEL // FNET_GROUPS
N_EXPERTS = 8
TOP_K = 2
EPS = 1e-6

RET_CHUNK = 256
VMEM_LIMIT_BYTES = 56 * 1024 * 1024

PROJ_TM, PROJ_TN = 1024, 2048
FFN0_TM, FFN0_SUB = 512, 512
ROW_TM = 1024
FNET_SEQ_BLOCK = 256
FNET_DT = 256
FNET_ROW_CHUNK = 16
MOE_TM, MOE_FC, MOE_SUB = 1024, 1792, 512


def _params(*semantics):
    return pltpu.CompilerParams(dimension_semantics=semantics, vmem_limit_bytes=VMEM_LIMIT_BYTES)


def _dot(a, b):
    return jnp.dot(a, b, preferred_element_type=F32)


def _rms(x, g):
    return x * lax.rsqrt(jnp.mean(x * x, axis=-1, keepdims=True) + EPS) * g


def _silu(a):
    return a * (1.0 / (1.0 + jnp.exp(-a)))


def _proj_kernel(x_ref, g_ref, cos_ref, sin_ref, w_ref, o_ref, hn_ref):
    j = pl.program_id(1)

    @pl.when(j == 0)
    def _():
        hn_ref[...] = _rms(x_ref[...], g_ref[...]).astype(BF16)

    tn = o_ref.shape[1]
    qk_tiles = 2 * RET_QK_WIDTH // tn
    v_tiles = RET_V_WIDTH // tn
    hn = hn_ref[...]
    half = RET_QK_DIM // 2

    def head(h):
        return _dot(hn, w_ref[:, h * RET_QK_DIM:(h + 1) * RET_QK_DIM])

    @pl.when(j < qk_tiles)
    def _():
        for h in range(tn // RET_QK_DIM):
            lo = h * RET_QK_DIM
            is_k = j * tn + lo >= RET_QK_WIDTH
            scale = jnp.where(is_k, RET_QK_DIM ** -0.5, 1.0).astype(F32)
            cos = cos_ref[...] * scale
            sin = sin_ref[...] * scale
            y = head(h)
            t1 = y[:, :half]
            t2 = y[:, half:]
            o_ref[:, lo:lo + half] = (t1 * cos - t2 * sin).astype(BF16)
            o_ref[:, lo + half:lo + RET_QK_DIM] = (t1 * sin + t2 * cos).astype(BF16)

    @pl.when((j >= qk_tiles) & (j < qk_tiles + v_tiles))
    def _():
        for h in range(tn // RET_QK_DIM):
            o_ref[:, h * RET_QK_DIM:(h + 1) * RET_QK_DIM] = head(h).astype(BF16)

    @pl.when(j >= qk_tiles + v_tiles)
    def _():
        for h in range(tn // RET_QK_DIM):
            o_ref[:, h * RET_QK_DIM:(h + 1) * RET_QK_DIM] = _silu(head(h)).astype(BF16)


def _proj(x2d, g, cos, sin, w_in, seq):
    t = x2d.shape[0]
    tm, tn = PROJ_TM, PROJ_TN
    pos_tiles = seq // tm
    return pl.pallas_call(
        _proj_kernel,
        grid=(t // tm, RET_IN_WIDTH // tn),
        in_specs=[
            pl.BlockSpec((tm, D_MODEL), lambda i, j: (i, 0)),
            pl.BlockSpec((1, D_MODEL), lambda i, j: (0, 0)),
            pl.BlockSpec((tm, RET_QK_DIM // 2), lambda i, j: (i % pos_tiles, 0)),
            pl.BlockSpec((tm, RET_QK_DIM // 2), lambda i, j: (i % pos_tiles, 0)),
            pl.BlockSpec((D_MODEL, tn), lambda i, j: (0, j)),
        ],
        out_specs=pl.BlockSpec((tm, tn), lambda i, j: (i, j)),
        out_shape=jax.ShapeDtypeStruct((t, RET_IN_WIDTH), BF16),
        scratch_shapes=[pltpu.VMEM((tm, D_MODEL), BF16)],
        compiler_params=_params("arbitrary", "arbitrary"),
        name="proj",
    )(x2d, g, cos, sin, w_in)


def _ret_kernel(lg_ref, q_ref, k_ref, v_ref, g_ref, gain_ref, o_ref, sfa_ref, sba_ref, sf_ref, sb_ref):
    c = RET_CHUNK
    n_chunks = q_ref.shape[0] // c
    h = pl.program_id(1)
    lgf = lg_ref[0, h]
    lgb = lg_ref[1, h]

    row = lax.broadcasted_iota(jnp.int32, (c, RET_QK_DIM), 0).astype(F32)
    xi_f = jnp.exp((row + 1.0) * lgf).astype(BF16)
    zeta_f = jnp.exp((c - 1.0 - row) * lgf).astype(BF16)
    xi_b = jnp.exp((c - row) * lgb).astype(BF16)
    zeta_b = jnp.exp(row * lgb).astype(BF16)
    g_f = jnp.exp(jnp.full((1, RET_V_DIM), c, F32) * lgf)
    g_b = jnp.exp(jnp.full((1, RET_V_DIM), c, F32) * lgb)
    di = lax.broadcasted_iota(jnp.int32, (c, c), 0)
    dj = lax.broadcasted_iota(jnp.int32, (c, c), 1)
    diff = (di - dj).astype(F32)
    decay = jnp.exp(jnp.where(diff >= 0, diff * lgf, -diff * lgb))

    def chunk(n):
        r0 = pl.multiple_of(n * c, c)
        return pl.ds(r0, c)

    def scaled(t, s):
        return t * s

    def outer(kz, vn):
        return lax.dot_general(kz, vn, (((0,), (0,)), ((), ())), preferred_element_type=F32)

    sf_ref[...] = jnp.zeros_like(sf_ref)
    sb_ref[...] = jnp.zeros_like(sb_ref)

    def states(step, carry):
        nf = step
        nb = n_chunks - 1 - step
        sf = sf_ref[...]
        sb = sb_ref[...]
        sfa_ref[nf] = sf.astype(BF16)
        sba_ref[nb] = sb.astype(BF16)
        rf, rb = chunk(nf), chunk(nb)
        sf_ref[...] = sf * g_f + outer(scaled(k_ref[rf, :], zeta_f), v_ref[rf, :])
        sb_ref[...] = sb * g_b + outer(scaled(k_ref[rb, :], zeta_b), v_ref[rb, :])
        return carry

    lax.fori_loop(0, n_chunks, states, 0, unroll=2)

    gain = gain_ref[...]

    def fwd(n, carry):
        rows = chunk(n)
        qn, kn, vn = q_ref[rows, :], k_ref[rows, :], v_ref[rows, :]
        scores = lax.dot_general(qn, kn, (((1,), (1,)), ((), ())), preferred_element_type=F32) * decay
        y = (_dot(scores.astype(BF16), vn) + _dot(scaled(qn, xi_f), sfa_ref[n])
             + _dot(scaled(qn, xi_b), sba_ref[n]))
        mu = jnp.mean(y, axis=-1, keepdims=True)
        yc = y - mu
        var = jnp.mean(yc * yc, axis=-1, keepdims=True)
        yn = yc * lax.rsqrt(var + EPS) * gain
        o_ref[rows, :] = g_ref[rows, :] * yn.astype(BF16)
        return carry

    lax.fori_loop(0, n_chunks, fwd, 0, unroll=2)


def _retention(proj, log_gamma, gain, batch, seq):
    t = proj.shape[0]
    qk_blocks = RET_QK_WIDTH // RET_QK_DIM
    k_off = qk_blocks
    v_off = 2 * RET_QK_WIDTH // RET_V_DIM
    g_off = v_off + RET_V_WIDTH // RET_V_DIM
    return pl.pallas_call(
        _ret_kernel,
        grid=(batch, RET_HEADS),
        in_specs=[
            pl.BlockSpec(memory_space=pltpu.SMEM),
            pl.BlockSpec((seq, RET_QK_DIM), lambda b, h: (b, h)),
            pl.BlockSpec((seq, RET_QK_DIM), lambda b, h: (b, k_off + h)),
            pl.BlockSpec((seq, RET_V_DIM), lambda b, h: (b, v_off + h)),
            pl.BlockSpec((seq, RET_V_DIM), lambda b, h: (b, g_off + h)),
            pl.BlockSpec((1, RET_V_DIM), lambda b, h: (0, h)),
        ],
        out_specs=pl.BlockSpec((seq, RET_V_DIM), lambda b, h: (b, h)),
        out_shape=jax.ShapeDtypeStruct((t, RET_V_WIDTH), BF16),
        scratch_shapes=[
            pltpu.VMEM((seq // RET_CHUNK, RET_QK_DIM, RET_V_DIM), BF16),
            pltpu.VMEM((seq // RET_CHUNK, RET_QK_DIM, RET_V_DIM), BF16),
            pltpu.VMEM((RET_QK_DIM, RET_V_DIM), F32),
            pltpu.VMEM((RET_QK_DIM, RET_V_DIM), F32),
        ],
        compiler_params=_params("arbitrary", "arbitrary"),
        name="retention",
    )(log_gamma, proj, proj, proj, proj, gain)


def _ffn0_kernel(x_ref, z_ref, wo_ref, g_ref, wg_ref, wu_ref, wd_ref, g2_ref, cc_ref, sc_ref,
                 o_ref, zr_ref, zi_ref):
    h1 = x_ref[...] + _dot(z_ref[...], wo_ref[...])
    hn = _rms(h1, g_ref[...]).astype(BF16)
    d_ff = wg_ref.shape[1]
    part = None
    for lo in range(0, d_ff, FFN0_SUB):
        cols = slice(lo, min(lo + FFN0_SUB, d_ff))
        hidden = (_silu(_dot(hn, wg_ref[:, cols])) * _dot(hn, wu_ref[:, cols])).astype(BF16)
        d = _dot(hidden, wd_ref[cols, :])
        part = d if part is None else part + d
    h2 = h1 + part
    o_ref[...] = h2
    hn2 = _rms(h2, g2_ref[...]).astype(BF16)
    for grp in range(FNET_GROUPS):
        cols = slice(grp * FNET_GROUP_DIM, (grp + 1) * FNET_GROUP_DIM)
        zr_ref[:, cols] = _dot(hn2[:, cols], cc_ref[...]).astype(BF16)
        zi_ref[:, cols] = _dot(hn2[:, cols], sc_ref[...]).astype(BF16)


def _ffn0(x2d, z, w_out, g, w_gate, w_up, w_down, g2, cc, sc):
    t = x2d.shape[0]
    d_ff = w_gate.shape[1]
    tm = FFN0_TM
    once = pl.Buffered(1)
    row = pl.BlockSpec((tm, D_MODEL), lambda i: (i, 0))
    return pl.pallas_call(
        _ffn0_kernel,
        grid=(t // tm,),
        in_specs=[
            row,
            pl.BlockSpec((tm, RET_V_WIDTH), lambda i: (i, 0)),
            pl.BlockSpec((RET_V_WIDTH, D_MODEL), lambda i: (0, 0), pipeline_mode=once),
            pl.BlockSpec((1, D_MODEL), lambda i: (0, 0), pipeline_mode=once),
            pl.BlockSpec((D_MODEL, d_ff), lambda i: (0, 0), pipeline_mode=once),
            pl.BlockSpec((D_MODEL, d_ff), lambda i: (0, 0), pipeline_mode=once),
            pl.BlockSpec((d_ff, D_MODEL), lambda i: (0, 0), pipeline_mode=once),
            pl.BlockSpec((1, D_MODEL), lambda i: (0, 0), pipeline_mode=once),
            pl.BlockSpec((FNET_GROUP_DIM, FNET_GROUP_DIM), lambda i: (0, 0), pipeline_mode=once),
            pl.BlockSpec((FNET_GROUP_DIM, FNET_GROUP_DIM), lambda i: (0, 0), pipeline_mode=once),
        ],
        out_specs=[row, row, row],
        out_shape=[jax.ShapeDtypeStruct((t, D_MODEL), F32), jax.ShapeDtypeStruct((t, D_MODEL), BF16),
                   jax.ShapeDtypeStruct((t, D_MODEL), BF16)],
        compiler_params=_params("arbitrary"),
        name="ffn0",
    )(x2d, z, w_out, g, w_gate, w_up, w_down, g2, cc, sc)


def _dft_across_blocks(xs):
    n = len(xs)
    if n == 1:
        return xs
    even = _dft_across_blocks(xs[0::2])
    odd = _dft_across_blocks(xs[1::2])
    out = [None] * n
    for k in range(n // 2):
        er, ei = even[k]
        o_r, o_i = odd[k]
        if k == 0:
            tr, ti = o_r, o_i
            out[k], out[k + n // 2] = (er + tr, ei + ti), (er - tr, ei - ti)
        elif 4 * k == n:
            out[k], out[k + n // 2] = (er + o_i, ei - o_r), (er - o_i, ei + o_r)
        else:
            c, s = math.cos(2.0 * math.pi * k / n), math.sin(2.0 * math.pi * k / n)
            tr = o_r * c + o_i * s
            ti = o_i * c - o_r * s
            out[k], out[k + n // 2] = (er + tr, ei + ti), (er - tr, ei - ti)
    return out


def _fnet_seq_kernel(zr_ref, zi_ref, gc_ref, gs_ref, o_ref, ar_ref, ai_ref, ys_ref):
    blk = FNET_SEQ_BLOCK
    n1 = zr_ref.shape[0] // blk
    rc = FNET_ROW_CHUNK
    lanes = 128

    def body(r, carry):
        r0 = pl.multiple_of(r * rc, rc)
        for l in range(zr_ref.shape[1] // lanes):
            cols = slice(l * lanes, (l + 1) * lanes)
            zs = [(zr_ref[pl.ds(s1 * blk + r0, rc), cols].astype(F32),
                   zi_ref[pl.ds(s1 * blk + r0, rc), cols].astype(F32)) for s1 in range(n1)]
            for k1, (a_r, a_i) in enumerate(_dft_across_blocks(zs)):
                ar_ref[pl.ds(k1 * blk + r0, rc), cols] = a_r.astype(BF16)
                ai_ref[pl.ds(k1 * blk + r0, rc), cols] = a_i.astype(BF16)
        return carry

    lax.fori_loop(0, blk // rc, body, 0)

    for k1 in range(n1):
        rows = slice(k1 * blk, (k1 + 1) * blk)
        y = _dot(gc_ref[rows, :], ar_ref[rows, :]) + _dot(gs_ref[rows, :], ai_ref[rows, :])
        for l in range(y.shape[1] // lanes):
            ys_ref[l, pl.ds(k1, blk, stride=n1), :] = y[:, l * lanes:(l + 1) * lanes]
    for l in range(o_ref.shape[1] // lanes):
        o_ref[:, l * lanes:(l + 1) * lanes] = ys_ref[l]


def _fnet_sequence(zr, zi, gc, gs, batch, seq):
    t = zr.shape[0]
    dt = FNET_DT
    blkspec = pl.BlockSpec((seq, dt), lambda b, c: (b, c))
    tab = pl.BlockSpec((seq, FNET_SEQ_BLOCK), lambda b, c: (0, 0))
    return pl.pallas_call(
        _fnet_seq_kernel,
        grid=(batch, D_MODEL // dt),
        in_specs=[blkspec, blkspec, tab, tab],
        out_specs=blkspec,
        out_shape=jax.ShapeDtypeStruct((t, D_MODEL), F32),
        scratch_shapes=[pltpu.VMEM((seq, dt), BF16), pltpu.VMEM((seq, dt), BF16),
                        pltpu.VMEM((dt // 128, seq, 128), F32)],
        compiler_params=_params("arbitrary", "arbitrary"),
        name="fnet_sequence",
    )(zr, zi, gc, gs)


def _router_kernel(h_ref, y_ref, w_ref, g_ref, rhi_ref, rlo_ref, h3_ref, hn_ref, idx_ref, wt_ref):
    h3 = h_ref[...] + _dot(y_ref[...].astype(BF16), w_ref[...])
    h3_ref[...] = h3
    hn = _rms(h3, g_ref[...])
    hn_ref[...] = hn
    hi = hn.astype(BF16)
    lo = (hn - hi.astype(F32)).astype(BF16)

    def nt(a, b):
        return lax.dot_general(a, b, (((1,), (1,)), ((), ())), preferred_element_type=F32)

    logits = nt(rhi_ref[...], hi) + (nt(rhi_ref[...], lo) + nt(rlo_ref[...], hi))
    row = lax.broadcasted_iota(jnp.int32, logits.shape, 0)
    m1 = jnp.max(logits, axis=0, keepdims=True)
    i1 = jnp.min(jnp.where(logits == m1, row, N_EXPERTS), axis=0, keepdims=True)
    rest = jnp.where(row == i1, -jnp.inf, logits)
    m2 = jnp.max(rest, axis=0, keepdims=True)
    i2 = jnp.min(jnp.where(rest == m2, row, N_EXPERTS), axis=0, keepdims=True)
    e2 = jnp.exp(m2 - m1)
    inv = 1.0 / (1.0 + e2)
    idx_ref[0:1, :] = i1
    idx_ref[1:2, :] = i2
    wt_ref[0:1, :] = inv
    wt_ref[1:2, :] = e2 * inv


def _router(h2, y, w_fnet, g, r_hi, r_lo):
    t = h2.shape[0]
    tm = ROW_TM
    row = pl.BlockSpec((tm, D_MODEL), lambda i: (i, 0))
    pair = pl.BlockSpec((TOP_K, tm), lambda i: (0, i))
    rtr = pl.BlockSpec((N_EXPERTS, D_MODEL), lambda i: (0, 0))
    return pl.pallas_call(
        _router_kernel,
        grid=(t // tm,),
        in_specs=[row, row, pl.BlockSpec((D_MODEL, D_MODEL), lambda i: (0, 0)),
                  pl.BlockSpec((1, D_MODEL), lambda i: (0, 0)), rtr, rtr],
        out_specs=[row, row, pair, pair],
        out_shape=[jax.ShapeDtypeStruct((t, D_MODEL), F32), jax.ShapeDtypeStruct((t, D_MODEL), F32),
                   jax.ShapeDtypeStruct((TOP_K, t), jnp.int32), jax.ShapeDtypeStruct((TOP_K, t), F32)],
        compiler_params=_params("arbitrary"),
        name="router",
    )(h2, y, w_fnet, g, r_hi, r_lo)


def _routing_tables(top_idx, tm):
    t = top_idx.shape[1]
    n_pairs = TOP_K * t
    n_tiles = n_pairs // tm + N_EXPERTS
    e_flat = top_idx.reshape(n_pairs)
    order = jnp.argsort(e_flat, stable=True).astype(jnp.int32)
    experts = jnp.arange(N_EXPERTS, dtype=jnp.int32)
    counts = jnp.sum((e_flat[:, None] == experts[None, :]).astype(jnp.int32), axis=0)
    off = jnp.cumsum(counts) - counts
    tiles_e = (counts + tm - 1) // tm
    tile_end = jnp.cumsum(tiles_e)
    tile_off = tile_end - tiles_e
    n_used = tile_end[-1]
    last_e = jnp.max(jnp.where(tiles_e > 0, experts, 0))
    tile_ids = jnp.arange(n_tiles, dtype=jnp.int32)
    tile_e = jnp.minimum(jnp.sum((tile_ids[:, None] >= tile_end[None, :]).astype(jnp.int32), axis=1), last_e)
    u = jnp.arange(tm, dtype=jnp.int32)
    rank = (tile_ids - tile_off[tile_e])[:, None] * tm + u[None, :]
    valid = (tile_ids < n_used)[:, None] & (rank < counts[tile_e][:, None])
    src = order[jnp.clip(off[tile_e][:, None] + rank, 0, n_pairs - 1)]
    tok = jnp.where(valid, src % t, 0).astype(jnp.int32)
    spare = n_pairs + jnp.broadcast_to(u[None, :], (1, tm))
    dst = jnp.where(valid, src, spare).astype(jnp.int32)
    tok_next = jnp.concatenate([tok[1:], jnp.zeros((2, tm), jnp.int32)], axis=0)
    dst_prev = jnp.concatenate([spare.astype(jnp.int32), dst], axis=0)
    tile_e = jnp.concatenate([tile_e, tile_e[-1:]]).astype(jnp.int32)
    return (tile_e, n_used.reshape(1).astype(jnp.int32), tok[0].reshape(1, 1, tm),
            tok_next.reshape(n_tiles + 1, 1, tm), dst_prev.reshape(n_tiles + 1, 1, tm))


def _moe_kernel(te_ref, nu_ref, tok0_ref, tokn_ref, dstp_ref, hn_hbm, wg_ref, wu_ref, wd_ref,
                y2_hbm, xbuf, xb, acc, ybuf, gsem, ssem, *, n_ff):
    del te_ref
    i = pl.program_id(0)
    j = pl.program_id(1)
    n_used = nu_ref[0]
    tm = xbuf.shape[0]
    fc = wg_ref.shape[2]
    rows_per_step = tm // n_ff
    spare0 = y2_hbm.shape[0] - tm
    unroll = 8

    def gather_copy(tok, u):
        return pltpu.make_async_copy(hn_hbm.at[pl.ds(tok, 1), :], xbuf.at[pl.ds(u, 1), :], gsem)

    def scatter_copy(u, row):
        return pltpu.make_async_copy(ybuf.at[pl.ds(u, 1), :], y2_hbm.at[pl.ds(row, 1), :], ssem)

    def start_gather(ref):
        def body(u, carry):
            gather_copy(ref[0, 0, u], u).start()
            return carry
        lax.fori_loop(0, tm, body, 0, unroll=unroll)

    def wait_gather():
        pltpu.make_async_copy(hn_hbm.at[pl.ds(0, tm), :], xbuf, gsem).wait()

    def start_scatter_all():
        def body(u, carry):
            scatter_copy(u, dstp_ref[0, 0, u]).start()
            return carry
        lax.fori_loop(0, tm, body, 0, unroll=unroll)

    def wait_scatter():
        pltpu.make_async_copy(ybuf, y2_hbm.at[pl.ds(0, tm), :], ssem).wait()

    @pl.when((i == 0) & (j == 0))
    def _():
        ybuf[...] = jnp.zeros_like(ybuf)
        fill = pltpu.make_async_copy(ybuf, y2_hbm.at[pl.ds(spare0, tm), :], ssem)
        fill.start()
        fill.wait()
        start_gather(tok0_ref)

    def ff_step(jj):
        if jj == 0:
            wait_gather()
            xb[...] = xbuf[...].astype(BF16)
            for u in range(tm):
                scatter_copy(u, dstp_ref[0, 0, u]).start(priority=1)
        if jj == n_ff - 1:
            wait_scatter()
        for u in range(jj * rows_per_step, (jj + 1) * rows_per_step):
            gather_copy(tokn_ref[0, 0, u], u).start()

        x = xb[...]
        part = None
        for lo in range(0, fc, MOE_SUB):
            cols = slice(lo, min(lo + MOE_SUB, fc))
            hidden = (_silu(_dot(x, wg_ref[0, :, cols])) * _dot(x, wu_ref[0, :, cols])).astype(BF16)
            d = _dot(hidden, wd_ref[0, cols, :])
            part = d if part is None else part + d

        if jj == 0:
            acc[...] = part
        elif jj < n_ff - 1:
            acc[...] += part
        else:
            ybuf[...] = acc[...] + part

    for jj in range(n_ff):
        pl.when((i < n_used) & (j == jj))(functools.partial(ff_step, jj))

    @pl.when((i == n_used) & (j == 0))
    def _():
        start_scatter_all()
        wait_scatter()
        wait_gather()


def _moe(hn, tile_e, n_used, tok0, tok_next, dst_prev, w_gate, w_up, w_down):
    t = hn.shape[0]
    d_ff = w_gate.shape[2]
    tm, fc = MOE_TM, MOE_FC
    n_steps = tok_next.shape[0]
    n_ff = d_ff // fc
    assert n_ff >= 2 and n_ff * fc == d_ff and tm % n_ff == 0

    def ff_step(i, j, nu):
        return jnp.where(i < nu[0], j, n_ff - 1)

    grid_spec = pltpu.PrefetchScalarGridSpec(
        num_scalar_prefetch=2,
        grid=(n_steps, n_ff),
        in_specs=[
            pl.BlockSpec((1, 1, tm), lambda i, j, te, nu: (0, 0, 0), memory_space=pltpu.SMEM),
            pl.BlockSpec((1, 1, tm), lambda i, j, te, nu: (i, 0, 0), memory_space=pltpu.SMEM),
            pl.BlockSpec((1, 1, tm), lambda i, j, te, nu: (i, 0, 0), memory_space=pltpu.SMEM),
            pl.BlockSpec(memory_space=pl.ANY),
            pl.BlockSpec((1, D_MODEL, fc), lambda i, j, te, nu: (te[i], 0, ff_step(i, j, nu))),
            pl.BlockSpec((1, D_MODEL, fc), lambda i, j, te, nu: (te[i], 0, ff_step(i, j, nu))),
            pl.BlockSpec((1, fc, D_MODEL), lambda i, j, te, nu: (te[i], ff_step(i, j, nu), 0)),
        ],
        out_specs=pl.BlockSpec(memory_space=pl.ANY),
        scratch_shapes=[
            pltpu.VMEM((tm, D_MODEL), F32),
            pltpu.VMEM((tm, D_MODEL), BF16),
            pltpu.VMEM((tm, D_MODEL), F32),
            pltpu.VMEM((tm, D_MODEL), F32),
            pltpu.SemaphoreType.DMA(()),
            pltpu.SemaphoreType.DMA(()),
        ],
    )
    return pl.pallas_call(
        functools.partial(_moe_kernel, n_ff=n_ff),
        grid_spec=grid_spec,
        out_shape=jax.ShapeDtypeStruct((TOP_K * t + tm, D_MODEL), F32),
        compiler_params=_params("arbitrary", "arbitrary"),
        name="moe",
    )(tile_e, n_used, tok0, tok_next, dst_prev, hn, w_gate, w_up, w_down)


def _final_kernel(h_ref, ya_ref, yb_ref, wt_ref, g_ref, o_ref):
    w = wt_ref[...]
    moe = ya_ref[...] * w[:, 0:1] + yb_ref[...] * w[:, 1:2]
    o_ref[...] = _rms(h_ref[...] + moe, g_ref[...])


def _final(h3, y2, top_w, g):
    t = h3.shape[0]
    tm = ROW_TM
    second = t // tm
    row = pl.BlockSpec((tm, D_MODEL), lambda i: (i, 0))
    return pl.pallas_call(
        _final_kernel,
        grid=(t // tm,),
        in_specs=[row, row, pl.BlockSpec((tm, D_MODEL), lambda i: (i + second, 0)),
                  pl.BlockSpec((tm, TOP_K), lambda i: (i, 0)), pl.BlockSpec((1, D_MODEL), lambda i: (0, 0))],
        out_specs=row,
        out_shape=jax.ShapeDtypeStruct((t, D_MODEL), F32),
        compiler_params=_params("arbitrary"),
        name="final_norm",
    )(h3, y2, y2, top_w, g)


def _rotary_tables(seq):
    half = RET_QK_DIM // 2
    inv = ROPE_BASE ** (-jnp.arange(half, dtype=F32) / half)
    ang = jnp.arange(seq, dtype=F32)[:, None] * inv[None, :]
    return jnp.cos(ang), jnp.sin(ang)


def _channel_dft_tables(n, scale):
    k = jnp.arange(n, dtype=jnp.int32)
    ang = ((k[:, None] * k[None, :]) % n).astype(F32) * (2.0 * math.pi / n)
    return (jnp.cos(ang) * scale).astype(BF16), (-jnp.sin(ang) * scale).astype(BF16)


def _sequence_dft_tables(seq, scale):
    blk = FNET_SEQ_BLOCK
    n1 = seq // blk
    k1 = jnp.arange(n1, dtype=jnp.int32)[:, None, None]
    k2 = jnp.arange(blk, dtype=jnp.int32)[None, :, None]
    s2 = jnp.arange(blk, dtype=jnp.int32)[None, None, :]
    ang = (((k1 + n1 * k2) * s2) % seq).astype(F32) * (2.0 * math.pi / seq)
    gc = (jnp.cos(ang) * scale).astype(BF16).reshape(n1 * blk, blk)
    gs = (jnp.sin(ang) * scale).astype(BF16).reshape(n1 * blk, blk)
    return gc, gs


def kernel(x, mix_norm, ffn_norm, ret_w_in, ret_decay_logit, ret_gn_gain, ret_w_out,
           dense_w_gate, dense_w_up, dense_w_down, fnet_w_out, moe_router,
           moe_w_gate, moe_w_up, moe_w_down, final_norm):
    batch, seq, d = x.shape
    t = batch * seq
    n1 = seq // FNET_SEQ_BLOCK
    assert d == D_MODEL and seq % RET_CHUNK == 0 and seq % PROJ_TM == 0
    assert n1 * FNET_SEQ_BLOCK == seq and n1 & (n1 - 1) == 0
    x2d = x.reshape(t, d)

    cos, sin = _rotary_tables(seq)
    proj = _proj(x2d, mix_norm[0:1], cos, sin, ret_w_in[0].astype(BF16), seq)
    log_gamma = jax.nn.log_sigmoid(ret_decay_logit[0].astype(F32))
    z = _retention(proj, log_gamma, ret_gn_gain[0:1], batch, seq)
    cc, sc = _channel_dft_tables(FNET_GROUP_DIM, FNET_GROUP_DIM ** -0.5)
    gc, gs = _sequence_dft_tables(seq, seq ** -0.5)
    h2, zr, zi = _ffn0(x2d, z, ret_w_out[0].astype(BF16), ffn_norm[0:1],
                       dense_w_gate[0].astype(BF16), dense_w_up[0].astype(BF16),
                       dense_w_down[0].astype(BF16), mix_norm[1:2], cc, sc)

    y = _fnet_sequence(zr, zi, gc, gs, batch, seq)
    router_t = moe_router[0].T
    r_hi = router_t.astype(BF16)
    r_lo = (router_t - r_hi.astype(F32)).astype(BF16)
    h3, hn3, top_idx, top_w = _router(h2, y, fnet_w_out[0].astype(BF16), ffn_norm[1:2], r_hi, r_lo)

    tile_e, n_used, tok0, tok_next, dst_prev = _routing_tables(top_idx, MOE_TM)
    y2 = _moe(hn3, tile_e, n_used, tok0, tok_next, dst_prev,
              moe_w_gate[0].astype(BF16), moe_w_up[0].astype(BF16), moe_w_down[0].astype(BF16))
    out = _final(h3, y2, top_w.T, final_norm.reshape(1, d))
    return out.reshape(batch, seq, d)
```

```python
import functools
import math

import jax
import jax.numpy as jnp
from jax import lax
from jax.experimental import pallas as pl
from jax.experimental.pallas import tpu as pltpu

F32 = jnp.float32
BF16 = jnp.bfloat16

D_MODEL = 1024
RET_HEADS = 4
RET_QK_DIM = D_MODEL // RET_HEADS
RET_V_DIM = 2 * RET_QK_DIM
RET_QK_WIDTH = RET_HEADS * RET_QK_DIM
RET_V_WIDTH = RET_HEADS * RET_V_DIM
RET_IN_WIDTH = 2 * RET_QK_WIDTH + 2 * RET_V_WIDTH
ROPE_BASE = 10000.0
FNET_GROUPS = 4
FNET_GROUP_DIM = D_MODEL // FNET_GROUPS
N_EXPERTS = 8
TOP_K = 2
EPS = 1e-6

RET_CHUNK = 256
VMEM_LIMIT_BYTES = 56 * 1024 * 1024

PROJ_TM, PROJ_TN = 1024, 2048
FFN0_TM, FFN0_SUB = 512, 512
ROW_TM = 1024
FNET_SEQ_BLOCK = 256
FNET_DT = 256
FNET_ROW_CHUNK = 16
MOE_TM, MOE_FC, MOE_SUB = 1024, 1792, 512


def _params(*semantics):
    return pltpu.CompilerParams(dimension_semantics=semantics, vmem_limit_bytes=VMEM_LIMIT_BYTES)


def _dot(a, b):
    return jnp.dot(a, b, preferred_element_type=F32)


def _lane_mean(x):
    lanes = 128
    n = x.shape[-1]
    acc = x[:, 0:lanes]
    for lo in range(lanes, n, lanes):
        acc = acc + x[:, lo:lo + lanes]
    return jnp.sum(acc, axis=-1, keepdims=True) * (1.0 / n)


def _rms(x, g):
    return x * lax.rsqrt(_lane_mean(x * x) + EPS) * g


def _silu(a):
    return a * (1.0 / (1.0 + jnp.exp(-a)))


def _proj_kernel(x_ref, g_ref, cos_ref, sin_ref, w_ref, o_ref, hn_ref):
    j = pl.program_id(1)

    @pl.when(j == 0)
    def _():
        hn_ref[...] = _rms(x_ref[...], g_ref[...]).astype(BF16)

    tn = o_ref.shape[1]
    qk_tiles = 2 * RET_QK_WIDTH // tn
    v_tiles = RET_V_WIDTH // tn
    hn = hn_ref[...]
    half = RET_QK_DIM // 2

    def head(h):
        return _dot(hn, w_ref[:, h * RET_QK_DIM:(h + 1) * RET_QK_DIM])

    @pl.when(j < qk_tiles)
    def _():
        for h in range(tn // RET_QK_DIM):
            lo = h * RET_QK_DIM
            is_k = j * tn + lo >= RET_QK_WIDTH
            scale = jnp.where(is_k, RET_QK_DIM ** -0.5, 1.0).astype(F32)
            cos = cos_ref[...] * scale
            sin = sin_ref[...] * scale
            y = head(h)
            t1 = y[:, :half]
            t2 = y[:, half:]
            o_ref[:, lo:lo + half] = (t1 * cos - t2 * sin).astype(BF16)
            o_ref[:, lo + half:lo + RET_QK_DIM] = (t1 * sin + t2 * cos).astype(BF16)

    @pl.when((j >= qk_tiles) & (j < qk_tiles + v_tiles))
    def _():
        for h in range(tn // RET_QK_DIM):
            o_ref[:, h * RET_QK_DIM:(h + 1) * RET_QK_DIM] = head(h).astype(BF16)

    @pl.when(j >= qk_tiles + v_tiles)
    def _():
        for h in range(tn // RET_QK_DIM):
            o_ref[:, h * RET_QK_DIM:(h + 1) * RET_QK_DIM] = _silu(head(h)).astype(BF16)


def _proj(x2d, g, cos, sin, w_in, seq):
    t = x2d.shape[0]
    tm, tn = PROJ_TM, PROJ_TN
    pos_tiles = seq // tm
    return pl.pallas_call(
        _proj_kernel,
        grid=(t // tm, RET_IN_WIDTH // tn),
        in_specs=[
            pl.BlockSpec((tm, D_MODEL), lambda i, j: (i, 0)),
            pl.BlockSpec((1, D_MODEL), lambda i, j: (0, 0)),
            pl.BlockSpec((tm, RET_QK_DIM // 2), lambda i, j: (i % pos_tiles, 0)),
            pl.BlockSpec((tm, RET_QK_DIM // 2), lambda i, j: (i % pos_tiles, 0)),
            pl.BlockSpec((D_MODEL, tn), lambda i, j: (0, j)),
        ],
        out_specs=pl.BlockSpec((tm, tn), lambda i, j: (i, j)),
        out_shape=jax.ShapeDtypeStruct((t, RET_IN_WIDTH), BF16),
        scratch_shapes=[pltpu.VMEM((tm, D_MODEL), BF16)],
        compiler_params=_params("arbitrary", "arbitrary"),
        name="proj",
    )(x2d, g, cos, sin, w_in)


def _ret_kernel(lg_ref, q_ref, k_ref, v_ref, g_ref, gain_ref, o_ref, sfa_ref, sba_ref, sf_ref, sb_ref):
    c = RET_CHUNK
    n_chunks = q_ref.shape[0] // c
    h = pl.program_id(1)
    lgf = lg_ref[0, h]
    lgb = lg_ref[1, h]

    row = lax.broadcasted_iota(jnp.int32, (c, RET_QK_DIM), 0).astype(F32)
    xi_f = jnp.exp((row + 1.0) * lgf).astype(BF16)
    zeta_f = jnp.exp((c - 1.0 - row) * lgf).astype(BF16)
    xi_b = jnp.exp((c - row) * lgb).astype(BF16)
    zeta_b = jnp.exp(row * lgb).astype(BF16)
    g_f = jnp.exp(jnp.full((1, RET_V_DIM), c, F32) * lgf)
    g_b = jnp.exp(jnp.full((1, RET_V_DIM), c, F32) * lgb)
    di = lax.broadcasted_iota(jnp.int32, (c, c), 0)
    dj = lax.broadcasted_iota(jnp.int32, (c, c), 1)
    diff = (di - dj).astype(F32)
    decay = jnp.exp(jnp.where(diff >= 0, diff * lgf, -diff * lgb))

    def chunk(n):
        r0 = pl.multiple_of(n * c, c)
        return pl.ds(r0, c)

    def scaled(t, s):
        return t * s

    def outer(kz, vn):
        return lax.dot_general(kz, vn, (((0,), (0,)), ((), ())), preferred_element_type=F32)

    sf_ref[...] = jnp.zeros_like(sf_ref)
    sb_ref[...] = jnp.zeros_like(sb_ref)

    def states(step, carry):
        nf = step
        nb = n_chunks - 1 - step
        sf = sf_ref[...]
        sb = sb_ref[...]
        sfa_ref[nf] = sf.astype(BF16)
        sba_ref[nb] = sb.astype(BF16)
        rf, rb = chunk(nf), chunk(nb)
        sf_ref[...] = sf * g_f + outer(scaled(k_ref[rf, :], zeta_f), v_ref[rf, :])
        sb_ref[...] = sb * g_b + outer(scaled(k_ref[rb, :], zeta_b), v_ref[rb, :])
        return carry

    lax.fori_loop(0, n_chunks, states, 0, unroll=2)

    gain = gain_ref[...]

    def fwd(n, carry):
        rows = chunk(n)
        qn, kn, vn = q_ref[rows, :], k_ref[rows, :], v_ref[rows, :]
        scores = lax.dot_general(qn, kn, (((1,), (1,)), ((), ())), preferred_element_type=F32) * decay
        y = (_dot(scores.astype(BF16), vn) + _dot(scaled(qn, xi_f), sfa_ref[n])
             + _dot(scaled(qn, xi_b), sba_ref[n]))
        mu = _lane_mean(y)
        yc = y - mu
        var = _lane_mean(yc * yc)
        yn = yc * lax.rsqrt(var + EPS) * gain
        o_ref[rows, :] = g_ref[rows, :] * yn.astype(BF16)
        return carry

    lax.fori_loop(0, n_chunks, fwd, 0, unroll=2)


def _retention(proj, log_gamma, gain, batch, seq):
    t = proj.shape[0]
    qk_blocks = RET_QK_WIDTH // RET_QK_DIM
    k_off = qk_blocks
    v_off = 2 * RET_QK_WIDTH // RET_V_DIM
    g_off = v_off + RET_V_WIDTH // RET_V_DIM
    return pl.pallas_call(
        _ret_kernel,
        grid=(batch, RET_HEADS),
        in_specs=[
            pl.BlockSpec(memory_space=pltpu.SMEM),
            pl.BlockSpec((seq, RET_QK_DIM), lambda b, h: (b, h)),
            pl.BlockSpec((seq, RET_QK_DIM), lambda b, h: (b, k_off + h)),
            pl.BlockSpec((seq, RET_V_DIM), lambda b, h: (b, v_off + h)),
            pl.BlockSpec((seq, RET_V_DIM), lambda b, h: (b, g_off + h)),
            pl.BlockSpec((1, RET_V_DIM), lambda b, h: (0, h)),
        ],
        out_specs=pl.BlockSpec((seq, RET_V_DIM), lambda b, h: (b, h)),
        out_shape=jax.ShapeDtypeStruct((t, RET_V_WIDTH), BF16),
        scratch_shapes=[
            pltpu.VMEM((seq // RET_CHUNK, RET_QK_DIM, RET_V_DIM), BF16),
            pltpu.VMEM((seq // RET_CHUNK, RET_QK_DIM, RET_V_DIM), BF16),
            pltpu.VMEM((RET_QK_DIM, RET_V_DIM), F32),
            pltpu.VMEM((RET_QK_DIM, RET_V_DIM), F32),
        ],
        compiler_params=_params("arbitrary", "arbitrary"),
        name="retention",
    )(log_gamma, proj, proj, proj, proj, gain)


def _ffn0_kernel(x_ref, z_ref, wo_ref, g_ref, wg_ref, wu_ref, wd_ref, g2_ref, cc_ref, sc_ref,
                 o_ref, zr_ref, zi_ref):
    h1 = x_ref[...] + _dot(z_ref[...], wo_ref[...])
    hn = _rms(h1, g_ref[...]).astype(BF16)
    d_ff = wg_ref.shape[1]
    part = None
    for lo in range(0, d_ff, FFN0_SUB):
        cols = slice(lo, min(lo + FFN0_SUB, d_ff))
        hidden = (_silu(_dot(hn, wg_ref[:, cols])) * _dot(hn, wu_ref[:, cols])).astype(BF16)
        d = _dot(hidden, wd_ref[cols, :])
        part = d if part is None else part + d
    h2 = h1 + part
    o_ref[...] = h2
    hn2 = _rms(h2, g2_ref[...]).astype(BF16)
    for grp in range(FNET_GROUPS):
        cols = slice(grp * FNET_GROUP_DIM, (grp + 1) * FNET_GROUP_DIM)
        zr_ref[:, cols] = _dot(hn2[:, cols], cc_ref[...]).astype(BF16)
        zi_ref[:, cols] = _dot(hn2[:, cols], sc_ref[...]).astype(BF16)


def _ffn0(x2d, z, w_out, g, w_gate, w_up, w_down, g2, cc, sc):
    t = x2d.shape[0]
    d_ff = w_gate.shape[1]
    tm = FFN0_TM
    once = pl.Buffered(1)
    row = pl.BlockSpec((tm, D_MODEL), lambda i: (i, 0))
    return pl.pallas_call(
        _ffn0_kernel,
        grid=(t // tm,),
        in_specs=[
            row,
            pl.BlockSpec((tm, RET_V_WIDTH), lambda i: (i, 0)),
            pl.BlockSpec((RET_V_WIDTH, D_MODEL), lambda i: (0, 0), pipeline_mode=once),
            pl.BlockSpec((1, D_MODEL), lambda i: (0, 0), pipeline_mode=once),
            pl.BlockSpec((D_MODEL, d_ff), lambda i: (0, 0), pipeline_mode=once),
            pl.BlockSpec((D_MODEL, d_ff), lambda i: (0, 0), pipeline_mode=once),
            pl.BlockSpec((d_ff, D_MODEL), lambda i: (0, 0), pipeline_mode=once),
            pl.BlockSpec((1, D_MODEL), lambda i: (0, 0), pipeline_mode=once),
            pl.BlockSpec((FNET_GROUP_DIM, FNET_GROUP_DIM), lambda i: (0, 0), pipeline_mode=once),
            pl.BlockSpec((FNET_GROUP_DIM, FNET_GROUP_DIM), lambda i: (0, 0), pipeline_mode=once),
        ],
        out_specs=[row, row, row],
        out_shape=[jax.ShapeDtypeStruct((t, D_MODEL), F32), jax.ShapeDtypeStruct((t, D_MODEL), BF16),
                   jax.ShapeDtypeStruct((t, D_MODEL), BF16)],
        compiler_params=_params("arbitrary"),
        name="ffn0",
    )(x2d, z, w_out, g, w_gate, w_up, w_down, g2, cc, sc)


def _dft_across_blocks(xs):
    n = len(xs)
    if n == 1:
        return xs
    even = _dft_across_blocks(xs[0::2])
    odd = _dft_across_blocks(xs[1::2])
    out = [None] * n
    for k in range(n // 2):
        er, ei = even[k]
        o_r, o_i = odd[k]
        if k == 0:
            tr, ti = o_r, o_i
            out[k], out[k + n // 2] = (er + tr, ei + ti), (er - tr, ei - ti)
        elif 4 * k == n:
            out[k], out[k + n // 2] = (er + o_i, ei - o_r), (er - o_i, ei + o_r)
        else:
            c, s = math.cos(2.0 * math.pi * k / n), math.sin(2.0 * math.pi * k / n)
            tr = o_r * c + o_i * s
            ti = o_i * c - o_r * s
            out[k], out[k + n // 2] = (er + tr, ei + ti), (er - tr, ei - ti)
    return out


def _fnet_seq_kernel(zr_ref, zi_ref, gc_ref, gs_ref, o_ref, ar_ref, ai_ref, ys_ref):
    blk = FNET_SEQ_BLOCK
    n1 = zr_ref.shape[0] // blk
    rc = FNET_ROW_CHUNK
    lanes = 128

    def body(r, carry):
        r0 = pl.multiple_of(r * rc, rc)
        for l in range(zr_ref.shape[1] // lanes):
            cols = slice(l * lanes, (l + 1) * lanes)
            zs = [(zr_ref[pl.ds(s1 * blk + r0, rc), cols].astype(F32),
                   zi_ref[pl.ds(s1 * blk + r0, rc), cols].astype(F32)) for s1 in range(n1)]
            for k1, (a_r, a_i) in enumerate(_dft_across_blocks(zs)):
                ar_ref[pl.ds(k1 * blk + r0, rc), cols] = a_r.astype(BF16)
                ai_ref[pl.ds(k1 * blk + r0, rc), cols] = a_i.astype(BF16)
        return carry

    lax.fori_loop(0, blk // rc, body, 0)

    for k1 in range(n1):
        rows = slice(k1 * blk, (k1 + 1) * blk)
        y = _dot(gc_ref[rows, :], ar_ref[rows, :]) + _dot(gs_ref[rows, :], ai_ref[rows, :])
        for l in range(y.shape[1] // lanes):
            ys_ref[l, pl.ds(k1, blk, stride=n1), :] = y[:, l * lanes:(l + 1) * lanes]
    for l in range(o_ref.shape[1] // lanes):
        o_ref[:, l * lanes:(l + 1) * lanes] = ys_ref[l]


def _fnet_sequence(zr, zi, gc, gs, batch, seq):
    t = zr.shape[0]
    dt = FNET_DT
    blkspec = pl.BlockSpec((seq, dt), lambda b, c: (b, c))
    tab = pl.BlockSpec((seq, FNET_SEQ_BLOCK), lambda b, c: (0, 0))
    return pl.pallas_call(
        _fnet_seq_kernel,
        grid=(batch, D_MODEL // dt),
        in_specs=[blkspec, blkspec, tab, tab],
        out_specs=blkspec,
        out_shape=jax.ShapeDtypeStruct((t, D_MODEL), F32),
        scratch_shapes=[pltpu.VMEM((seq, dt), BF16), pltpu.VMEM((seq, dt), BF16),
                        pltpu.VMEM((dt // 128, seq, 128), F32)],
        compiler_params=_params("arbitrary", "arbitrary"),
        name="fnet_sequence",
    )(zr, zi, gc, gs)


def _router_kernel(h_ref, y_ref, w_ref, g_ref, rhi_ref, rlo_ref, h3_ref, hn_ref, idx_ref, wt_ref):
    h3 = h_ref[...] + _dot(y_ref[...].astype(BF16), w_ref[...])
    h3_ref[...] = h3
    hn = _rms(h3, g_ref[...])
    hn_ref[...] = hn
    hi = hn.astype(BF16)
    lo = (hn - hi.astype(F32)).astype(BF16)

    def nt(a, b):
        return lax.dot_general(a, b, (((1,), (1,)), ((), ())), preferred_element_type=F32)

    logits = nt(rhi_ref[...], hi) + (nt(rhi_ref[...], lo) + nt(rlo_ref[...], hi))
    row = lax.broadcasted_iota(jnp.int32, logits.shape, 0)
    m1 = jnp.max(logits, axis=0, keepdims=True)
    i1 = jnp.min(jnp.where(logits == m1, row, N_EXPERTS), axis=0, keepdims=True)
    rest = jnp.where(row == i1, -jnp.inf, logits)
    m2 = jnp.max(rest, axis=0, keepdims=True)
    i2 = jnp.min(jnp.where(rest == m2, row, N_EXPERTS), axis=0, keepdims=True)
    e2 = jnp.exp(m2 - m1)
    inv = 1.0 / (1.0 + e2)
    idx_ref[0:1, :] = i1
    idx_ref[1:2, :] = i2
    wt_ref[0:1, :] = inv
    wt_ref[1:2, :] = e2 * inv


def _router(h2, y, w_fnet, g, r_hi, r_lo):
    t = h2.shape[0]
    tm = ROW_TM
    row = pl.BlockSpec((tm, D_MODEL), lambda i: (i, 0))
    pair = pl.BlockSpec((TOP_K, tm), lambda i: (0, i))
    rtr = pl.BlockSpec((N_EXPERTS, D_MODEL), lambda i: (0, 0))
    return pl.pallas_call(
        _router_kernel,
        grid=(t // tm,),
        in_specs=[row, row, pl.BlockSpec((D_MODEL, D_MODEL), lambda i: (0, 0)),
                  pl.BlockSpec((1, D_MODEL), lambda i: (0, 0)), rtr, rtr],
        out_specs=[row, row, pair, pair],
        out_shape=[jax.ShapeDtypeStruct((t, D_MODEL), F32), jax.ShapeDtypeStruct((t, D_MODEL), F32),
                   jax.ShapeDtypeStruct((TOP_K, t), jnp.int32), jax.ShapeDtypeStruct((TOP_K, t), F32)],
        compiler_params=_params("arbitrary"),
        name="router",
    )(h2, y, w_fnet, g, r_hi, r_lo)


def _routing_tables(top_idx, tm):
    t = top_idx.shape[1]
    n_pairs = TOP_K * t
    n_tiles = n_pairs // tm + N_EXPERTS
    e_flat = top_idx.reshape(n_pairs)
    order = jnp.argsort(e_flat, stable=True).astype(jnp.int32)
    experts = jnp.arange(N_EXPERTS, dtype=jnp.int32)
    counts = jnp.sum((e_flat[:, None] == experts[None, :]).astype(jnp.int32), axis=0)
    off = jnp.cumsum(counts) - counts
    tiles_e = (counts + tm - 1) // tm
    tile_end = jnp.cumsum(tiles_e)
    tile_off = tile_end - tiles_e
    n_used = tile_end[-1]
    last_e = jnp.max(jnp.where(tiles_e > 0, experts, 0))
    tile_ids = jnp.arange(n_tiles, dtype=jnp.int32)
    tile_e = jnp.minimum(jnp.sum((tile_ids[:, None] >= tile_end[None, :]).astype(jnp.int32), axis=1), last_e)
    u = jnp.arange(tm, dtype=jnp.int32)
    rank = (tile_ids - tile_off[tile_e])[:, None] * tm + u[None, :]
    valid = (tile_ids < n_used)[:, None] & (rank < counts[tile_e][:, None])
    src = order[jnp.clip(off[tile_e][:, None] + rank, 0, n_pairs - 1)]
    tok = jnp.where(valid, src % t, 0).astype(jnp.int32)
    spare = n_pairs + jnp.broadcast_to(u[None, :], (1, tm))
    dst = jnp.where(valid, src, spare).astype(jnp.int32)
    tok_next = jnp.concatenate([tok[1:], jnp.zeros((2, tm), jnp.int32)], axis=0)
    dst_prev = jnp.concatenate([spare.astype(jnp.int32), dst], axis=0)
    tile_e = jnp.concatenate([tile_e, tile_e[-1:]]).astype(jnp.int32)
    return (tile_e, n_used.reshape(1).astype(jnp.int32), tok[0].reshape(1, 1, tm),
            tok_next.reshape(n_tiles + 1, 1, tm), dst_prev.reshape(n_tiles + 1, 1, tm))


def _moe_kernel(te_ref, nu_ref, tok0_ref, tokn_ref, dstp_ref, hn_hbm, wg_ref, wu_ref, wd_ref,
                y2_hbm, xbuf, xb, acc, ybuf, gsem, ssem, *, n_ff):
    del te_ref
    i = pl.program_id(0)
    j = pl.program_id(1)
    n_used = nu_ref[0]
    tm = xbuf.shape[0]
    fc = wg_ref.shape[2]
    rows_per_step = tm // n_ff
    spare0 = y2_hbm.shape[0] - tm
    unroll = 8

    def gather_copy(tok, u):
        return pltpu.make_async_copy(hn_hbm.at[pl.ds(tok, 1), :], xbuf.at[pl.ds(u, 1), :], gsem)

    def scatter_copy(u, row):
        return pltpu.make_async_copy(ybuf.at[pl.ds(u, 1), :], y2_hbm.at[pl.ds(row, 1), :], ssem)

    def start_gather(ref):
        def body(u, carry):
            gather_copy(ref[0, 0, u], u).start()
            return carry
        lax.fori_loop(0, tm, body, 0, unroll=unroll)

    def wait_gather():
        pltpu.make_async_copy(hn_hbm.at[pl.ds(0, tm), :], xbuf, gsem).wait()

    def start_scatter_all():
        def body(u, carry):
            scatter_copy(u, dstp_ref[0, 0, u]).start()
            return carry
        lax.fori_loop(0, tm, body, 0, unroll=unroll)

    def wait_scatter():
        pltpu.make_async_copy(ybuf, y2_hbm.at[pl.ds(0, tm), :], ssem).wait()

    @pl.when((i == 0) & (j == 0))
    def _():
        ybuf[...] = jnp.zeros_like(ybuf)
        fill = pltpu.make_async_copy(ybuf, y2_hbm.at[pl.ds(spare0, tm), :], ssem)
        fill.start()
        fill.wait()
        start_gather(tok0_ref)

    def issue_gathers(jj):
        for u in range(jj * rows_per_step, (jj + 1) * rows_per_step):
            gather_copy(tokn_ref[0, 0, u], u).start()

    def ff_step(jj):
        if jj == 0:
            wait_gather()
            xb[...] = xbuf[...].astype(BF16)
            for u in range(tm):
                scatter_copy(u, dstp_ref[0, 0, u]).start(priority=1)
            issue_gathers(0)
        if jj == n_ff - 1:
            wait_scatter()

        x = xb[...]
        part = None
        for lo in range(0, fc, MOE_SUB):
            cols = slice(lo, min(lo + MOE_SUB, fc))
            hidden = (_silu(_dot(x, wg_ref[0, :, cols])) * _dot(x, wu_ref[0, :, cols])).astype(BF16)
            d = _dot(hidden, wd_ref[0, cols, :])
            part = d if part is None else part + d

        if jj == 0:
            acc[...] = part
        elif jj < n_ff - 1:
            acc[...] += part
        else:
            ybuf[...] = acc[...] + part

    for jj in range(n_ff):
        if jj > 0:
            pl.when((i < n_used) & (j == jj))(functools.partial(issue_gathers, jj))
        pl.when((i < n_used) & (j == jj))(functools.partial(ff_step, jj))

    @pl.when((i == n_used) & (j == 0))
    def _():
        start_scatter_all()
        wait_scatter()
        wait_gather()


def _moe(hn, tile_e, n_used, tok0, tok_next, dst_prev, w_gate, w_up, w_down):
    t = hn.shape[0]
    d_ff = w_gate.shape[2]
    tm, fc = MOE_TM, MOE_FC
    n_steps = tok_next.shape[0]
    n_ff = d_ff // fc
    assert n_ff >= 2 and n_ff * fc == d_ff and tm % n_ff == 0

    def ff_step(i, j, nu):
        return jnp.where(i < nu[0], j, n_ff - 1)

    grid_spec = pltpu.PrefetchScalarGridSpec(
        num_scalar_prefetch=2,
        grid=(n_steps, n_ff),
        in_specs=[
            pl.BlockSpec((1, 1, tm), lambda i, j, te, nu: (0, 0, 0), memory_space=pltpu.SMEM),
            pl.BlockSpec((1, 1, tm), lambda i, j, te, nu: (i, 0, 0), memory_space=pltpu.SMEM),
            pl.BlockSpec((1, 1, tm), lambda i, j, te, nu: (i, 0, 0), memory_space=pltpu.SMEM),
            pl.BlockSpec(memory_space=pl.ANY),
            pl.BlockSpec((1, D_MODEL, fc), lambda i, j, te, nu: (te[i], 0, ff_step(i, j, nu))),
            pl.BlockSpec((1, D_MODEL, fc), lambda i, j, te, nu: (te[i], 0, ff_step(i, j, nu))),
            pl.BlockSpec((1, fc, D_MODEL), lambda i, j, te, nu: (te[i], ff_step(i, j, nu), 0)),
        ],
        out_specs=pl.BlockSpec(memory_space=pl.ANY),
        scratch_shapes=[
            pltpu.VMEM((tm, D_MODEL), F32),
            pltpu.VMEM((tm, D_MODEL), BF16),
            pltpu.VMEM((tm, D_MODEL), F32),
            pltpu.VMEM((tm, D_MODEL), F32),
            pltpu.SemaphoreType.DMA(()),
            pltpu.SemaphoreType.DMA(()),
        ],
    )
    return pl.pallas_call(
        functools.partial(_moe_kernel, n_ff=n_ff),
        grid_spec=grid_spec,
        out_shape=jax.ShapeDtypeStruct((TOP_K * t + tm, D_MODEL), F32),
        compiler_params=_params("arbitrary", "arbitrary"),
        name="moe",
    )(tile_e, n_used, tok0, tok_next, dst_prev, hn, w_gate, w_up, w_down)


def _final_kernel(h_ref, ya_ref, yb_ref, wt_ref, g_ref, o_ref):
    w = wt_ref[...]
    moe = ya_ref[...] * w[:, 0:1] + yb_ref[...] * w[:, 1:2]
    o_ref[...] = _rms(h_ref[...] + moe, g_ref[...])


def _final(h3, y2, top_w, g):
    t = h3.shape[0]
    tm = ROW_TM
    second = t // tm
    row = pl.BlockSpec((tm, D_MODEL), lambda i: (i, 0))
    return pl.pallas_call(
        _final_kernel,
        grid=(t // tm,),
        in_specs=[row, row, pl.BlockSpec((tm, D_MODEL), lambda i: (i + second, 0)),
                  pl.BlockSpec((tm, TOP_K), lambda i: (i, 0)), pl.BlockSpec((1, D_MODEL), lambda i: (0, 0))],
        out_specs=row,
        out_shape=jax.ShapeDtypeStruct((t, D_MODEL), F32),
        compiler_params=_params("arbitrary"),
        name="final_norm",
    )(h3, y2, y2, top_w, g)


def _rotary_tables(seq):
    half = RET_QK_DIM // 2
    inv = ROPE_BASE ** (-jnp.arange(half, dtype=F32) / half)
    ang = jnp.arange(seq, dtype=F32)[:, None] * inv[None, :]
    return jnp.cos(ang), jnp.sin(ang)


def _channel_dft_tables(n, scale):
    k = jnp.arange(n, dtype=jnp.int32)
    ang = ((k[:, None] * k[None, :]) % n).astype(F32) * (2.0 * math.pi / n)
    return (jnp.cos(ang) * scale).astype(BF16), (-jnp.sin(ang) * scale).astype(BF16)


def _sequence_dft_tables(seq, scale):
    blk = FNET_SEQ_BLOCK
    n1 = seq // blk
    k1 = jnp.arange(n1, dtype=jnp.int32)[:, None, None]
    k2 = jnp.arange(blk, dtype=jnp.int32)[None, :, None]
    s2 = jnp.arange(blk, dtype=jnp.int32)[None, None, :]
    ang = (((k1 + n1 * k2) * s2) % seq).astype(F32) * (2.0 * math.pi / seq)
    gc = (jnp.cos(ang) * scale).astype(BF16).reshape(n1 * blk, blk)
    gs = (jnp.sin(ang) * scale).astype(BF16).reshape(n1 * blk, blk)
    return gc, gs


def kernel(x, mix_norm, ffn_norm, ret_w_in, ret_decay_logit, ret_gn_gain, ret_w_out,
           dense_w_gate, dense_w_up, dense_w_down, fnet_w_out, moe_router,
           moe_w_gate, moe_w_up, moe_w_down, final_norm):
    batch, seq, d = x.shape
    t = batch * seq
    n1 = seq // FNET_SEQ_BLOCK
    assert d == D_MODEL and seq % RET_CHUNK == 0 and seq % PROJ_TM == 0
    assert n1 * FNET_SEQ_BLOCK == seq and n1 & (n1 - 1) == 0
    x2d = x.reshape(t, d)

    cos, sin = _rotary_tables(seq)
    proj = _proj(x2d, mix_norm[0:1], cos, sin, ret_w_in[0].astype(BF16), seq)
    log_gamma = jax.nn.log_sigmoid(ret_decay_logit[0].astype(F32))
    z = _retention(proj, log_gamma, ret_gn_gain[0:1], batch, seq)
    cc, sc = _channel_dft_tables(FNET_GROUP_DIM, FNET_GROUP_DIM ** -0.5)
    gc, gs = _sequence_dft_tables(seq, seq ** -0.5)
    h2, zr, zi = _ffn0(x2d, z, ret_w_out[0].astype(BF16), ffn_norm[0:1],
                       dense_w_gate[0].astype(BF16), dense_w_up[0].astype(BF16),
                       dense_w_down[0].astype(BF16), mix_norm[1:2], cc, sc)

    y = _fnet_sequence(zr, zi, gc, gs, batch, seq)
    router_t = moe_router[0].T
    r_hi = router_t.astype(BF16)
    r_lo = (router_t - r_hi.astype(F32)).astype(BF16)
    h3, hn3, top_idx, top_w = _router(h2, y, fnet_w_out[0].astype(BF16), ffn_norm[1:2], r_hi, r_lo)

    tile_e, n_used, tok0, tok_next, dst_prev = _routing_tables(top_idx, MOE_TM)
    y2 = _moe(hn3, tile_e, n_used, tok0, tok_next, dst_prev,
              moe_w_gate[0].astype(BF16), moe_w_up[0].astype(BF16), moe_w_down[0].astype(BF16))
    out = _final(h3, y2, top_w.T, final_norm.reshape(1, d))
    return out.reshape(batch, seq, d)
```

```python
import functools
import math

import jax
import jax.numpy as jnp
from jax import lax
from jax.experimental import pallas as pl
from jax.experimental.pallas import tpu as pltpu

F32 = jnp.float32
BF16 = jnp.bfloat16

D_MODEL = 1024
RET_HEADS = 4
RET_QK_DIM = D_MODEL // RET_HEADS
RET_V_DIM = 2 * RET_QK_DIM
RET_QK_WIDTH = RET_HEADS * RET_QK_DIM
RET_V_WIDTH = RET_HEADS * RET_V_DIM
RET_IN_WIDTH = 2 * RET_QK_WIDTH + 2 * RET_V_WIDTH
ROPE_BASE = 10000.0
FNET_GROUPS = 4
FNET_GROUP_DIM = D_MODEL // FNET_GROUPS
N_EXPERTS = 8
TOP_K = 2
EPS = 1e-6

RET_CHUNK = 256
VMEM_LIMIT_BYTES = 56 * 1024 * 1024

PROJ_TM, PROJ_TN = 1024, 2048
FFN0_TM, FFN0_SUB = 512, 512
ROW_TM = 1024
FNET_SEQ_BLOCK = 256
FNET_DT = 256
FNET_ROW_CHUNK = 16
MOE_TM, MOE_FC, MOE_SUB = 1024, 1792, 512


def _params(*semantics):
    return pltpu.CompilerParams(dimension_semantics=semantics, vmem_limit_bytes=VMEM_LIMIT_BYTES)


def _dot(a, b):
    return jnp.dot(a, b, preferred_element_type=F32)


def _lane_mean(x):
    lanes = 128
    n = x.shape[-1]
    acc = x[:, 0:lanes]
    for lo in range(lanes, n, lanes):
        acc = acc + x[:, lo:lo + lanes]
    return jnp.sum(acc, axis=-1, keepdims=True) * (1.0 / n)


def _rms(x, g):
    return x * lax.rsqrt(_lane_mean(x * x) + EPS) * g


def _silu(a):
    return a * (1.0 / (1.0 + jnp.exp(-a)))


BF16_ROW_TILE = 16


def _cast_specs(arrays, n_steps, step_of):
    specs, shapes = [], []
    for a in arrays:
        rows, cols = a.shape
        n_blocks = n_steps
        while rows % (n_blocks * BF16_ROW_TILE):
            n_blocks //= 2
        assert n_blocks >= 1

        def index(*ids, n_blocks=n_blocks):
            return (jnp.minimum(step_of(*ids), n_blocks - 1), 0)

        specs.append(pl.BlockSpec((rows // n_blocks, cols), index))
        shapes.append(jax.ShapeDtypeStruct(a.shape, BF16))
    return specs, shapes


def _with_casts(body, n_in, n_out, n_cast):
    def kernel(*refs):
        ins, refs = refs[:n_in], refs[n_in:]
        srcs, refs = refs[:n_cast], refs[n_cast:]
        outs, refs = refs[:n_out], refs[n_out:]
        dsts, scratch = refs[:n_cast], refs[n_cast:]
        for src, dst in zip(srcs, dsts):
            dst[...] = src[...].astype(BF16)
        body(*ins, *outs, *scratch)
    return kernel


def _proj_kernel(x_ref, g_ref, cos_ref, sin_ref, w_ref, o_ref, hn_ref):
    j = pl.program_id(1)

    @pl.when(j == 0)
    def _():
        hn_ref[...] = _rms(x_ref[...], g_ref[...]).astype(BF16)

    tn = o_ref.shape[1]
    qk_tiles = 2 * RET_QK_WIDTH // tn
    v_tiles = RET_V_WIDTH // tn
    hn = hn_ref[...]
    half = RET_QK_DIM // 2

    def head(h):
        return _dot(hn, w_ref[:, h * RET_QK_DIM:(h + 1) * RET_QK_DIM])

    @pl.when(j < qk_tiles)
    def _():
        for h in range(tn // RET_QK_DIM):
            lo = h * RET_QK_DIM
            is_k = j * tn + lo >= RET_QK_WIDTH
            scale = jnp.where(is_k, RET_QK_DIM ** -0.5, 1.0).astype(F32)
            cos = cos_ref[...] * scale
            sin = sin_ref[...] * scale
            y = head(h)
            t1 = y[:, :half]
            t2 = y[:, half:]
            o_ref[:, lo:lo + half] = (t1 * cos - t2 * sin).astype(BF16)
            o_ref[:, lo + half:lo + RET_QK_DIM] = (t1 * sin + t2 * cos).astype(BF16)

    @pl.when((j >= qk_tiles) & (j < qk_tiles + v_tiles))
    def _():
        for h in range(tn // RET_QK_DIM):
            o_ref[:, h * RET_QK_DIM:(h + 1) * RET_QK_DIM] = head(h).astype(BF16)

    @pl.when(j >= qk_tiles + v_tiles)
    def _():
        for h in range(tn // RET_QK_DIM):
            o_ref[:, h * RET_QK_DIM:(h + 1) * RET_QK_DIM] = _silu(head(h)).astype(BF16)


def _proj(x2d, g, cos, sin, w_in, seq, casts):
    t = x2d.shape[0]
    tm, tn = PROJ_TM, PROJ_TN
    pos_tiles = seq // tm
    cast_specs, cast_shapes = _cast_specs(casts, t // tm, lambda i, j: i)
    outs = pl.pallas_call(
        _with_casts(_proj_kernel, 5, 1, len(casts)),
        grid=(t // tm, RET_IN_WIDTH // tn),
        in_specs=[
            pl.BlockSpec((tm, D_MODEL), lambda i, j: (i, 0)),
            pl.BlockSpec((1, D_MODEL), lambda i, j: (0, 0)),
            pl.BlockSpec((tm, RET_QK_DIM // 2), lambda i, j: (i % pos_tiles, 0)),
            pl.BlockSpec((tm, RET_QK_DIM // 2), lambda i, j: (i % pos_tiles, 0)),
            pl.BlockSpec((D_MODEL, tn), lambda i, j: (0, j)),
        ] + cast_specs,
        out_specs=[pl.BlockSpec((tm, tn), lambda i, j: (i, j))] + cast_specs,
        out_shape=[jax.ShapeDtypeStruct((t, RET_IN_WIDTH), BF16)] + cast_shapes,
        scratch_shapes=[pltpu.VMEM((tm, D_MODEL), BF16)],
        compiler_params=_params("arbitrary", "arbitrary"),
        name="proj",
    )(x2d, g, cos, sin, w_in, *casts)
    return outs[0], outs[1:]


def _ret_kernel(lg_ref, q_ref, k_ref, v_ref, g_ref, gain_ref, o_ref, sfa_ref, sba_ref, sf_ref, sb_ref):
    c = RET_CHUNK
    n_chunks = q_ref.shape[0] // c
    h = pl.program_id(1)
    lgf = lg_ref[0, h]
    lgb = lg_ref[1, h]

    row = lax.broadcasted_iota(jnp.int32, (c, RET_QK_DIM), 0).astype(F32)
    xi_f = jnp.exp((row + 1.0) * lgf).astype(BF16)
    zeta_f = jnp.exp((c - 1.0 - row) * lgf).astype(BF16)
    xi_b = jnp.exp((c - row) * lgb).astype(BF16)
    zeta_b = jnp.exp(row * lgb).astype(BF16)
    g_f = jnp.exp(jnp.full((1, RET_V_DIM), c, F32) * lgf)
    g_b = jnp.exp(jnp.full((1, RET_V_DIM), c, F32) * lgb)
    di = lax.broadcasted_iota(jnp.int32, (c, c), 0)
    dj = lax.broadcasted_iota(jnp.int32, (c, c), 1)
    diff = (di - dj).astype(F32)
    decay = jnp.exp(jnp.where(diff >= 0, diff * lgf, -diff * lgb))

    def chunk(n):
        r0 = pl.multiple_of(n * c, c)
        return pl.ds(r0, c)

    def scaled(t, s):
        return t * s

    def outer(kz, vn):
        return lax.dot_general(kz, vn, (((0,), (0,)), ((), ())), preferred_element_type=F32)

    sf_ref[...] = jnp.zeros_like(sf_ref)
    sb_ref[...] = jnp.zeros_like(sb_ref)

    def states(step, carry):
        nf = step
        nb = n_chunks - 1 - step
        sf = sf_ref[...]
        sb = sb_ref[...]
        sfa_ref[nf] = sf.astype(BF16)
        sba_ref[nb] = sb.astype(BF16)
        rf, rb = chunk(nf), chunk(nb)
        sf_ref[...] = sf * g_f + outer(scaled(k_ref[rf, :], zeta_f), v_ref[rf, :])
        sb_ref[...] = sb * g_b + outer(scaled(k_ref[rb, :], zeta_b), v_ref[rb, :])
        return carry

    lax.fori_loop(0, n_chunks, states, 0, unroll=2)

    gain = gain_ref[...]

    def fwd(n, carry):
        rows = chunk(n)
        qn, kn, vn = q_ref[rows, :], k_ref[rows, :], v_ref[rows, :]
        scores = lax.dot_general(qn, kn, (((1,), (1,)), ((), ())), preferred_element_type=F32) * decay
        y = (_dot(scores.astype(BF16), vn) + _dot(scaled(qn, xi_f), sfa_ref[n])
             + _dot(scaled(qn, xi_b), sba_ref[n]))
        mu = _lane_mean(y)
        yc = y - mu
        var = _lane_mean(yc * yc)
        yn = yc * lax.rsqrt(var + EPS) * gain
        o_ref[rows, :] = g_ref[rows, :] * yn.astype(BF16)
        return carry

    lax.fori_loop(0, n_chunks, fwd, 0, unroll=2)


def _retention(proj, log_gamma, gain, batch, seq, casts):
    t = proj.shape[0]
    qk_blocks = RET_QK_WIDTH // RET_QK_DIM
    k_off = qk_blocks
    v_off = 2 * RET_QK_WIDTH // RET_V_DIM
    g_off = v_off + RET_V_WIDTH // RET_V_DIM
    cast_specs, cast_shapes = _cast_specs(casts, batch * RET_HEADS, lambda b, h: b * RET_HEADS + h)
    outs = pl.pallas_call(
        _with_casts(_ret_kernel, 6, 1, len(casts)),
        grid=(batch, RET_HEADS),
        in_specs=[
            pl.BlockSpec(memory_space=pltpu.SMEM),
            pl.BlockSpec((seq, RET_QK_DIM), lambda b, h: (b, h)),
            pl.BlockSpec((seq, RET_QK_DIM), lambda b, h: (b, k_off + h)),
            pl.BlockSpec((seq, RET_V_DIM), lambda b, h: (b, v_off + h)),
            pl.BlockSpec((seq, RET_V_DIM), lambda b, h: (b, g_off + h)),
            pl.BlockSpec((1, RET_V_DIM), lambda b, h: (0, h)),
        ] + cast_specs,
        out_specs=[pl.BlockSpec((seq, RET_V_DIM), lambda b, h: (b, h))] + cast_specs,
        out_shape=[jax.ShapeDtypeStruct((t, RET_V_WIDTH), BF16)] + cast_shapes,
        scratch_shapes=[
            pltpu.VMEM((seq // RET_CHUNK, RET_QK_DIM, RET_V_DIM), BF16),
            pltpu.VMEM((seq // RET_CHUNK, RET_QK_DIM, RET_V_DIM), BF16),
            pltpu.VMEM((RET_QK_DIM, RET_V_DIM), F32),
            pltpu.VMEM((RET_QK_DIM, RET_V_DIM), F32),
        ],
        compiler_params=_params("arbitrary", "arbitrary"),
        name="retention",
    )(log_gamma, proj, proj, proj, proj, gain, *casts)
    return outs[0], outs[1:]


def _ffn0_kernel(x_ref, z_ref, wo_ref, g_ref, wg_ref, wu_ref, wd_ref, g2_ref, cc_ref, sc_ref,
                 o_ref, zr_ref, zi_ref):
    h1 = x_ref[...] + _dot(z_ref[...], wo_ref[...])
    hn = _rms(h1, g_ref[...]).astype(BF16)
    d_ff = wg_ref.shape[1]
    part = None
    for lo in range(0, d_ff, FFN0_SUB):
        cols = slice(lo, min(lo + FFN0_SUB, d_ff))
        hidden = (_silu(_dot(hn, wg_ref[:, cols])) * _dot(hn, wu_ref[:, cols])).astype(BF16)
        d = _dot(hidden, wd_ref[cols, :])
        part = d if part is None else part + d
    h2 = h1 + part
    o_ref[...] = h2
    hn2 = _rms(h2, g2_ref[...]).astype(BF16)
    for grp in range(FNET_GROUPS):
        cols = slice(grp * FNET_GROUP_DIM, (grp + 1) * FNET_GROUP_DIM)
        zr_ref[:, cols] = _dot(hn2[:, cols], cc_ref[...]).astype(BF16)
        zi_ref[:, cols] = _dot(hn2[:, cols], sc_ref[...]).astype(BF16)


def _ffn0(x2d, z, w_out, g, w_gate, w_up, w_down, g2, cc, sc, casts):
    t = x2d.shape[0]
    d_ff = w_gate.shape[1]
    tm = FFN0_TM
    once = pl.Buffered(1)
    row = pl.BlockSpec((tm, D_MODEL), lambda i: (i, 0))
    cast_specs, cast_shapes = _cast_specs(casts, t // tm, lambda i: i)
    outs = pl.pallas_call(
        _with_casts(_ffn0_kernel, 10, 3, len(casts)),
        grid=(t // tm,),
        in_specs=[
            row,
            pl.BlockSpec((tm, RET_V_WIDTH), lambda i: (i, 0)),
            pl.BlockSpec((RET_V_WIDTH, D_MODEL), lambda i: (0, 0), pipeline_mode=once),
            pl.BlockSpec((1, D_MODEL), lambda i: (0, 0), pipeline_mode=once),
            pl.BlockSpec((D_MODEL, d_ff), lambda i: (0, 0), pipeline_mode=once),
            pl.BlockSpec((D_MODEL, d_ff), lambda i: (0, 0), pipeline_mode=once),
            pl.BlockSpec((d_ff, D_MODEL), lambda i: (0, 0), pipeline_mode=once),
            pl.BlockSpec((1, D_MODEL), lambda i: (0, 0), pipeline_mode=once),
            pl.BlockSpec((FNET_GROUP_DIM, FNET_GROUP_DIM), lambda i: (0, 0), pipeline_mode=once),
            pl.BlockSpec((FNET_GROUP_DIM, FNET_GROUP_DIM), lambda i: (0, 0), pipeline_mode=once),
        ] + cast_specs,
        out_specs=[row, row, row] + cast_specs,
        out_shape=[jax.ShapeDtypeStruct((t, D_MODEL), F32), jax.ShapeDtypeStruct((t, D_MODEL), BF16),
                   jax.ShapeDtypeStruct((t, D_MODEL), BF16)] + cast_shapes,
        compiler_params=_params("arbitrary"),
        name="ffn0",
    )(x2d, z, w_out, g, w_gate, w_up, w_down, g2, cc, sc, *casts)
    return outs[:3], outs[3:]


def _dft_across_blocks(xs):
    n = len(xs)
    if n == 1:
        return xs
    even = _dft_across_blocks(xs[0::2])
    odd = _dft_across_blocks(xs[1::2])
    out = [None] * n
    for k in range(n // 2):
        er, ei = even[k]
        o_r, o_i = odd[k]
        if k == 0:
            tr, ti = o_r, o_i
            out[k], out[k + n // 2] = (er + tr, ei + ti), (er - tr, ei - ti)
        elif 4 * k == n:
            out[k], out[k + n // 2] = (er + o_i, ei - o_r), (er - o_i, ei + o_r)
        else:
            c, s = math.cos(2.0 * math.pi * k / n), math.sin(2.0 * math.pi * k / n)
            tr = o_r * c + o_i * s
            ti = o_i * c - o_r * s
            out[k], out[k + n // 2] = (er + tr, ei + ti), (er - tr, ei - ti)
    return out


def _fnet_seq_kernel(zr_ref, zi_ref, gc_ref, gs_ref, o_ref, ar_ref, ai_ref, ys_ref):
    blk = FNET_SEQ_BLOCK
    n1 = zr_ref.shape[0] // blk
    rc = FNET_ROW_CHUNK
    lanes = 128

    def body(r, carry):
        r0 = pl.multiple_of(r * rc, rc)
        for l in range(zr_ref.shape[1] // lanes):
            cols = slice(l * lanes, (l + 1) * lanes)
            zs = [(zr_ref[pl.ds(s1 * blk + r0, rc), cols].astype(F32),
                   zi_ref[pl.ds(s1 * blk + r0, rc), cols].astype(F32)) for s1 in range(n1)]
            for k1, (a_r, a_i) in enumerate(_dft_across_blocks(zs)):
                ar_ref[pl.ds(k1 * blk + r0, rc), cols] = a_r.astype(BF16)
                ai_ref[pl.ds(k1 * blk + r0, rc), cols] = a_i.astype(BF16)
        return carry

    lax.fori_loop(0, blk // rc, body, 0)

    for k1 in range(n1):
        rows = slice(k1 * blk, (k1 + 1) * blk)
        y = _dot(gc_ref[rows, :], ar_ref[rows, :]) + _dot(gs_ref[rows, :], ai_ref[rows, :])
        for l in range(y.shape[1] // lanes):
            ys_ref[l, pl.ds(k1, blk, stride=n1), :] = y[:, l * lanes:(l + 1) * lanes]
    for l in range(o_ref.shape[1] // lanes):
        o_ref[:, l * lanes:(l + 1) * lanes] = ys_ref[l]


def _fnet_sequence(zr, zi, gc, gs, batch, seq, casts):
    t = zr.shape[0]
    dt = FNET_DT
    blkspec = pl.BlockSpec((seq, dt), lambda b, c: (b, c))
    tab = pl.BlockSpec((seq, FNET_SEQ_BLOCK), lambda b, c: (0, 0))
    col_tiles = D_MODEL // dt
    cast_specs, cast_shapes = _cast_specs(casts, batch * col_tiles, lambda b, c: b * col_tiles + c)
    outs = pl.pallas_call(
        _with_casts(_fnet_seq_kernel, 4, 1, len(casts)),
        grid=(batch, col_tiles),
        in_specs=[blkspec, blkspec, tab, tab] + cast_specs,
        out_specs=[blkspec] + cast_specs,
        out_shape=[jax.ShapeDtypeStruct((t, D_MODEL), F32)] + cast_shapes,
        scratch_shapes=[pltpu.VMEM((seq, dt), BF16), pltpu.VMEM((seq, dt), BF16),
                        pltpu.VMEM((dt // 128, seq, 128), F32)],
        compiler_params=_params("arbitrary", "arbitrary"),
        name="fnet_sequence",
    )(zr, zi, gc, gs, *casts)
    return outs[0], outs[1:]


def _router_kernel(h_ref, y_ref, w_ref, g_ref, rhi_ref, rlo_ref, h3_ref, hn_ref, idx_ref, wt_ref):
    h3 = h_ref[...] + _dot(y_ref[...].astype(BF16), w_ref[...])
    h3_ref[...] = h3
    hn = _rms(h3, g_ref[...])
    hn_ref[...] = hn
    hi = hn.astype(BF16)
    lo = (hn - hi.astype(F32)).astype(BF16)

    def nt(a, b):
        return lax.dot_general(a, b, (((1,), (1,)), ((), ())), preferred_element_type=F32)

    logits = nt(rhi_ref[...], hi) + (nt(rhi_ref[...], lo) + nt(rlo_ref[...], hi))
    row = lax.broadcasted_iota(jnp.int32, logits.shape, 0)
    m1 = jnp.max(logits, axis=0, keepdims=True)
    i1 = jnp.min(jnp.where(logits == m1, row, N_EXPERTS), axis=0, keepdims=True)
    rest = jnp.where(row == i1, -jnp.inf, logits)
    m2 = jnp.max(rest, axis=0, keepdims=True)
    i2 = jnp.min(jnp.where(rest == m2, row, N_EXPERTS), axis=0, keepdims=True)
    e2 = jnp.exp(m2 - m1)
    inv = 1.0 / (1.0 + e2)
    idx_ref[0:1, :] = i1
    idx_ref[1:2, :] = i2
    wt_ref[0:1, :] = inv
    wt_ref[1:2, :] = e2 * inv


def _router(h2, y, w_fnet, g, r_hi, r_lo):
    t = h2.shape[0]
    tm = ROW_TM
    row = pl.BlockSpec((tm, D_MODEL), lambda i: (i, 0))
    pair = pl.BlockSpec((TOP_K, tm), lambda i: (0, i))
    rtr = pl.BlockSpec((N_EXPERTS, D_MODEL), lambda i: (0, 0))
    return pl.pallas_call(
        _router_kernel,
        grid=(t // tm,),
        in_specs=[row, row, pl.BlockSpec((D_MODEL, D_MODEL), lambda i: (0, 0)),
                  pl.BlockSpec((1, D_MODEL), lambda i: (0, 0)), rtr, rtr],
        out_specs=[row, row, pair, pair],
        out_shape=[jax.ShapeDtypeStruct((t, D_MODEL), F32), jax.ShapeDtypeStruct((t, D_MODEL), F32),
                   jax.ShapeDtypeStruct((TOP_K, t), jnp.int32), jax.ShapeDtypeStruct((TOP_K, t), F32)],
        compiler_params=_params("arbitrary"),
        name="router",
    )(h2, y, w_fnet, g, r_hi, r_lo)


def _routing_tables(top_idx, tm):
    t = top_idx.shape[1]
    n_pairs = TOP_K * t
    n_tiles = n_pairs // tm + N_EXPERTS
    e_flat = top_idx.reshape(n_pairs)
    order = jnp.argsort(e_flat, stable=True).astype(jnp.int32)
    experts = jnp.arange(N_EXPERTS, dtype=jnp.int32)
    counts = jnp.sum((e_flat[:, None] == experts[None, :]).astype(jnp.int32), axis=0)
    off = jnp.cumsum(counts) - counts
    tiles_e = (counts + tm - 1) // tm
    tile_end = jnp.cumsum(tiles_e)
    tile_off = tile_end - tiles_e
    n_used = tile_end[-1]
    last_e = jnp.max(jnp.where(tiles_e > 0, experts, 0))
    tile_ids = jnp.arange(n_tiles, dtype=jnp.int32)
    tile_e = jnp.minimum(jnp.sum((tile_ids[:, None] >= tile_end[None, :]).astype(jnp.int32), axis=1), last_e)
    u = jnp.arange(tm, dtype=jnp.int32)
    rank = (tile_ids - tile_off[tile_e])[:, None] * tm + u[None, :]
    valid = (tile_ids < n_used)[:, None] & (rank < counts[tile_e][:, None])
    src = order[jnp.clip(off[tile_e][:, None] + rank, 0, n_pairs - 1)]
    tok = jnp.where(valid, src % t, 0).astype(jnp.int32)
    spare = n_pairs + jnp.broadcast_to(u[None, :], (1, tm))
    dst = jnp.where(valid, src, spare).astype(jnp.int32)
    tok_next = jnp.concatenate([tok[1:], jnp.zeros((2, tm), jnp.int32)], axis=0)
    dst_prev = jnp.concatenate([spare.astype(jnp.int32), dst], axis=0)
    tile_e = jnp.concatenate([tile_e, tile_e[-1:]]).astype(jnp.int32)
    return (tile_e, n_used.reshape(1).astype(jnp.int32), tok[0].reshape(1, 1, tm),
            tok_next.reshape(n_tiles + 1, 1, tm), dst_prev.reshape(n_tiles + 1, 1, tm))


def _moe_kernel(te_ref, nu_ref, tok0_ref, tokn_ref, dstp_ref, hn_hbm, wg_ref, wu_ref, wd_ref,
                y2_hbm, xbuf, xb, acc, ybuf, gsem, ssem, *, n_ff):
    del te_ref
    i = pl.program_id(0)
    j = pl.program_id(1)
    n_used = nu_ref[0]
    tm = xbuf.shape[0]
    fc = wg_ref.shape[2]
    rows_per_step = tm // n_ff
    spare0 = y2_hbm.shape[0] - tm
    unroll = 8

    def gather_copy(tok, u):
        return pltpu.make_async_copy(hn_hbm.at[pl.ds(tok, 1), :], xbuf.at[pl.ds(u, 1), :], gsem)

    def scatter_copy(u, row):
        return pltpu.make_async_copy(ybuf.at[pl.ds(u, 1), :], y2_hbm.at[pl.ds(row, 1), :], ssem)

    def start_gather(ref):
        def body(u, carry):
            gather_copy(ref[0, 0, u], u).start()
            return carry
        lax.fori_loop(0, tm, body, 0, unroll=unroll)

    def wait_gather():
        pltpu.make_async_copy(hn_hbm.at[pl.ds(0, tm), :], xbuf, gsem).wait()

    def start_scatter_all():
        def body(u, carry):
            scatter_copy(u, dstp_ref[0, 0, u]).start()
            return carry
        lax.fori_loop(0, tm, body, 0, unroll=unroll)

    def wait_scatter():
        pltpu.make_async_copy(ybuf, y2_hbm.at[pl.ds(0, tm), :], ssem).wait()

    @pl.when((i == 0) & (j == 0))
    def _():
        ybuf[...] = jnp.zeros_like(ybuf)
        fill = pltpu.make_async_copy(ybuf, y2_hbm.at[pl.ds(spare0, tm), :], ssem)
        fill.start()
        fill.wait()
        start_gather(tok0_ref)

    def issue_gathers(jj):
        for u in range(jj * rows_per_step, (jj + 1) * rows_per_step):
            gather_copy(tokn_ref[0, 0, u], u).start()

    def ff_step(jj):
        if jj == 0:
            wait_gather()
            xb[...] = xbuf[...].astype(BF16)
            for u in range(tm):
                scatter_copy(u, dstp_ref[0, 0, u]).start(priority=1)
            issue_gathers(0)
        if jj == n_ff - 1:
            wait_scatter()

        x = xb[...]
        part = None
        for lo in range(0, fc, MOE_SUB):
            cols = slice(lo, min(lo + MOE_SUB, fc))
            hidden = (_silu(_dot(x, wg_ref[0, :, cols])) * _dot(x, wu_ref[0, :, cols])).astype(BF16)
            d = _dot(hidden, wd_ref[0, cols, :])
            part = d if part is None else part + d

        if jj == 0:
            acc[...] = part
        elif jj < n_ff - 1:
            acc[...] += part
        else:
            ybuf[...] = acc[...] + part

    for jj in range(n_ff):
        if jj > 0:
            pl.when((i < n_used) & (j == jj))(functools.partial(issue_gathers, jj))
        pl.when((i < n_used) & (j == jj))(functools.partial(ff_step, jj))

    @pl.when((i == n_used) & (j == 0))
    def _():
        start_scatter_all()
        wait_scatter()
        wait_gather()


def _moe(hn, tile_e, n_used, tok0, tok_next, dst_prev, w_gate, w_up, w_down):
    t = hn.shape[0]
    d_ff = w_gate.shape[2]
    tm, fc = MOE_TM, MOE_FC
    n_steps = tok_next.shape[0]
    n_ff = d_ff // fc
    assert n_ff >= 2 and n_ff * fc == d_ff and tm % n_ff == 0

    def ff_step(i, j, nu):
        return jnp.where(i < nu[0], j, n_ff - 1)

    grid_spec = pltpu.PrefetchScalarGridSpec(
        num_scalar_prefetch=2,
        grid=(n_steps, n_ff),
        in_specs=[
            pl.BlockSpec((1, 1, tm), lambda i, j, te, nu: (0, 0, 0), memory_space=pltpu.SMEM),
            pl.BlockSpec((1, 1, tm), lambda i, j, te, nu: (i, 0, 0), memory_space=pltpu.SMEM),
            pl.BlockSpec((1, 1, tm), lambda i, j, te, nu: (i, 0, 0), memory_space=pltpu.SMEM),
            pl.BlockSpec(memory_space=pl.ANY),
            pl.BlockSpec((1, D_MODEL, fc), lambda i, j, te, nu: (te[i], 0, ff_step(i, j, nu))),
            pl.BlockSpec((1, D_MODEL, fc), lambda i, j, te, nu: (te[i], 0, ff_step(i, j, nu))),
            pl.BlockSpec((1, fc, D_MODEL), lambda i, j, te, nu: (te[i], ff_step(i, j, nu), 0)),
        ],
        out_specs=pl.BlockSpec(memory_space=pl.ANY),
        scratch_shapes=[
            pltpu.VMEM((tm, D_MODEL), F32),
            pltpu.VMEM((tm, D_MODEL), BF16),
            pltpu.VMEM((tm, D_MODEL), F32),
            pltpu.VMEM((tm, D_MODEL), F32),
            pltpu.SemaphoreType.DMA(()),
            pltpu.SemaphoreType.DMA(()),
        ],
    )
    return pl.pallas_call(
        functools.partial(_moe_kernel, n_ff=n_ff),
        grid_spec=grid_spec,
        out_shape=jax.ShapeDtypeStruct((TOP_K * t + tm, D_MODEL), F32),
        compiler_params=_params("arbitrary", "arbitrary"),
        name="moe",
    )(tile_e, n_used, tok0, tok_next, dst_prev, hn, w_gate, w_up, w_down)


def _final_kernel(h_ref, ya_ref, yb_ref, wt_ref, g_ref, o_ref):
    w = wt_ref[...]
    moe = ya_ref[...] * w[:, 0:1] + yb_ref[...] * w[:, 1:2]
    o_ref[...] = _rms(h_ref[...] + moe, g_ref[...])


def _final(h3, y2, top_w, g):
    t = h3.shape[0]
    tm = ROW_TM
    second = t // tm
    row = pl.BlockSpec((tm, D_MODEL), lambda i: (i, 0))
    return pl.pallas_call(
        _final_kernel,
        grid=(t // tm,),
        in_specs=[row, row, pl.BlockSpec((tm, D_MODEL), lambda i: (i + second, 0)),
                  pl.BlockSpec((tm, TOP_K), lambda i: (i, 0)), pl.BlockSpec((1, D_MODEL), lambda i: (0, 0))],
        out_specs=row,
        out_shape=jax.ShapeDtypeStruct((t, D_MODEL), F32),
        compiler_params=_params("arbitrary"),
        name="final_norm",
    )(h3, y2, y2, top_w, g)


def _rotary_tables(seq):
    half = RET_QK_DIM // 2
    inv = ROPE_BASE ** (-jnp.arange(half, dtype=F32) / half)
    ang = jnp.arange(seq, dtype=F32)[:, None] * inv[None, :]
    return jnp.cos(ang), jnp.sin(ang)


def _channel_dft_tables(n, scale):
    k = jnp.arange(n, dtype=jnp.int32)
    ang = ((k[:, None] * k[None, :]) % n).astype(F32) * (2.0 * math.pi / n)
    return (jnp.cos(ang) * scale).astype(BF16), (-jnp.sin(ang) * scale).astype(BF16)


def _sequence_dft_tables(seq, scale):
    blk = FNET_SEQ_BLOCK
    n1 = seq // blk
    k1 = jnp.arange(n1, dtype=jnp.int32)[:, None, None]
    k2 = jnp.arange(blk, dtype=jnp.int32)[None, :, None]
    s2 = jnp.arange(blk, dtype=jnp.int32)[None, None, :]
    ang = (((k1 + n1 * k2) * s2) % seq).astype(F32) * (2.0 * math.pi / seq)
    gc = (jnp.cos(ang) * scale).astype(BF16).reshape(n1 * blk, blk)
    gs = (jnp.sin(ang) * scale).astype(BF16).reshape(n1 * blk, blk)
    return gc, gs


def kernel(x, mix_norm, ffn_norm, ret_w_in, ret_decay_logit, ret_gn_gain, ret_w_out,
           dense_w_gate, dense_w_up, dense_w_down, fnet_w_out, moe_router,
           moe_w_gate, moe_w_up, moe_w_down, final_norm):
    batch, seq, d = x.shape
    t = batch * seq
    n1 = seq // FNET_SEQ_BLOCK
    assert d == D_MODEL and seq % RET_CHUNK == 0 and seq % PROJ_TM == 0
    assert n1 * FNET_SEQ_BLOCK == seq and n1 & (n1 - 1) == 0
    x2d = x.reshape(t, d)

    cos, sin = _rotary_tables(seq)
    n_exp, _, d_ffe = moe_w_gate[0].shape
    proj, (moe_wd,) = _proj(x2d, mix_norm[0:1], cos, sin, ret_w_in[0].astype(BF16), seq,
                            [moe_w_down[0].reshape(n_exp * d_ffe, d)])
    log_gamma = jax.nn.log_sigmoid(ret_decay_logit[0].astype(F32))
    z, (w_out, d_wg, d_wu, d_wd) = _retention(
        proj, log_gamma, ret_gn_gain[0:1], batch, seq,
        [ret_w_out[0], dense_w_gate[0], dense_w_up[0], dense_w_down[0]])
    cc, sc = _channel_dft_tables(FNET_GROUP_DIM, FNET_GROUP_DIM ** -0.5)
    gc, gs = _sequence_dft_tables(seq, seq ** -0.5)
    (h2, zr, zi), (moe_wu, w_fnet) = _ffn0(
        x2d, z, w_out, ffn_norm[0:1], d_wg, d_wu, d_wd, mix_norm[1:2], cc, sc,
        [moe_w_up[0].reshape(n_exp * d, d_ffe), fnet_w_out[0]])

    y, (moe_wg,) = _fnet_sequence(zr, zi, gc, gs, batch, seq, [moe_w_gate[0].reshape(n_exp * d, d_ffe)])
    router_t = moe_router[0].T
    r_hi = router_t.astype(BF16)
    r_lo = (router_t - r_hi.astype(F32)).astype(BF16)
    h3, hn3, top_idx, top_w = _router(h2, y, w_fnet, ffn_norm[1:2], r_hi, r_lo)

    tile_e, n_used, tok0, tok_next, dst_prev = _routing_tables(top_idx, MOE_TM)
    y2 = _moe(hn3, tile_e, n_used, tok0, tok_next, dst_prev,
              moe_wg.reshape(n_exp, d, d_ffe), moe_wu.reshape(n_exp, d, d_ffe),
              moe_wd.reshape(n_exp, d_ffe, d))
    out = _final(h3, y2, top_w.T, final_norm.reshape(1, d))
    return out.reshape(batch, seq, d)
```

```python
import functools
import math

import jax
import jax.numpy as jnp
import numpy as np
from jax import lax
from jax.experimental import pallas as pl
from jax.experimental.pallas import tpu as pltpu

F32 = jnp.float32
BF16 = jnp.bfloat16

D_MODEL = 1024
RET_HEADS = 4
RET_QK_DIM = D_MODEL // RET_HEADS
RET_V_DIM = 2 * RET_QK_DIM
RET_QK_WIDTH = RET_HEADS * RET_QK_DIM
RET_V_WIDTH = RET_HEADS * RET_V_DIM
RET_IN_WIDTH = 2 * RET_QK_WIDTH + 2 * RET_V_WIDTH
ROPE_BASE = 10000.0
FNET_GROUPS = 4
FNET_GROUP_DIM = D_MODEL // FNET_GROUPS
N_EXPERTS = 8
TOP_K = 2
EPS = 1e-6

RET_CHUNK = 256
VMEM_LIMIT_BYTES = 56 * 1024 * 1024

PROJ_TM, PROJ_TN = 1024, 2048
FFN0_TM, FFN0_SUB = 512, 512
ROW_TM = 1024
FNET_SEQ_BLOCK = 256
FNET_DT = 256
FNET_ROW_CHUNK = 16
MOE_TM, MOE_FC, MOE_SUB = 1024, 1792, 512


def _params(*semantics):
    return pltpu.CompilerParams(dimension_semantics=semantics, vmem_limit_bytes=VMEM_LIMIT_BYTES)


def _dot(a, b):
    return jnp.dot(a, b, preferred_element_type=F32)


def _lane_mean(x):
    lanes = 128
    n = x.shape[-1]
    acc = x[:, 0:lanes]
    for lo in range(lanes, n, lanes):
        acc = acc + x[:, lo:lo + lanes]
    return jnp.sum(acc, axis=-1, keepdims=True) * (1.0 / n)


def _rms(x, g):
    return x * lax.rsqrt(_lane_mean(x * x) + EPS) * g


def _silu(a):
    return a * (1.0 / (1.0 + jnp.exp(-a)))


BF16_ROW_TILE = 16


def _cast_specs(arrays, n_steps, step_of):
    specs, shapes = [], []
    for a in arrays:
        rows, cols = a.shape
        n_blocks = n_steps
        while rows % (n_blocks * BF16_ROW_TILE):
            n_blocks //= 2
        assert n_blocks >= 1

        def index(*ids, n_blocks=n_blocks):
            return (jnp.minimum(step_of(*ids), n_blocks - 1), 0)

        specs.append(pl.BlockSpec((rows // n_blocks, cols), index))
        shapes.append(jax.ShapeDtypeStruct(a.shape, BF16))
    return specs, shapes


def _with_casts(body, n_in, n_out, n_cast):
    def kernel(*refs):
        ins, refs = refs[:n_in], refs[n_in:]
        srcs, refs = refs[:n_cast], refs[n_cast:]
        outs, refs = refs[:n_out], refs[n_out:]
        dsts, scratch = refs[:n_cast], refs[n_cast:]
        for src, dst in zip(srcs, dsts):
            dst[...] = src[...].astype(BF16)
        body(*ins, *outs, *scratch)
    return kernel


def _proj_kernel(x_ref, g_ref, cos_ref, sin_ref, w_ref, o_ref, hn_ref):
    j = pl.program_id(1)

    @pl.when(j == 0)
    def _():
        hn_ref[...] = _rms(x_ref[...], g_ref[...]).astype(BF16)

    tn = o_ref.shape[1]
    half = RET_QK_DIM // 2
    k_start, v_start, g_start = RET_QK_WIDTH, 2 * RET_QK_WIDTH, 2 * RET_QK_WIDTH + RET_V_WIDTH

    def column_tile(jj):
        hn = hn_ref[...]
        for h in range(tn // RET_QK_DIM):
            lo = h * RET_QK_DIM
            col = jj * tn + lo
            y = _dot(hn, w_ref[:, col:col + RET_QK_DIM])
            if col < v_start:
                scale = RET_QK_DIM ** -0.5 if col >= k_start else 1.0
                cos = cos_ref[...] * scale
                sin = sin_ref[...] * scale
                t1 = y[:, :half]
                t2 = y[:, half:]
                o_ref[:, lo:lo + half] = (t1 * cos - t2 * sin).astype(BF16)
                o_ref[:, lo + half:lo + RET_QK_DIM] = (t1 * sin + t2 * cos).astype(BF16)
            elif col < g_start:
                o_ref[:, lo:lo + RET_QK_DIM] = y.astype(BF16)
            else:
                o_ref[:, lo:lo + RET_QK_DIM] = _silu(y).astype(BF16)

    for jj in range(w_ref.shape[1] // tn):
        pl.when(j == jj)(functools.partial(column_tile, jj))


def _proj(x2d, g, cos, sin, w_in, seq, casts):
    t = x2d.shape[0]
    tm, tn = PROJ_TM, PROJ_TN
    pos_tiles = seq // tm
    cast_specs, cast_shapes = _cast_specs(casts, t // tm, lambda i, j: i)
    outs = pl.pallas_call(
        _with_casts(_proj_kernel, 5, 1, len(casts)),
        grid=(t // tm, RET_IN_WIDTH // tn),
        in_specs=[
            pl.BlockSpec((tm, D_MODEL), lambda i, j: (i, 0)),
            pl.BlockSpec((1, D_MODEL), lambda i, j: (0, 0)),
            pl.BlockSpec((tm, RET_QK_DIM // 2), lambda i, j: (i % pos_tiles, 0)),
            pl.BlockSpec((tm, RET_QK_DIM // 2), lambda i, j: (i % pos_tiles, 0)),
            pl.BlockSpec((D_MODEL, RET_IN_WIDTH), lambda i, j: (0, 0), pipeline_mode=pl.Buffered(1)),
        ] + cast_specs,
        out_specs=[pl.BlockSpec((tm, tn), lambda i, j: (i, j))] + cast_specs,
        out_shape=[jax.ShapeDtypeStruct((t, RET_IN_WIDTH), BF16)] + cast_shapes,
        scratch_shapes=[pltpu.VMEM((tm, D_MODEL), BF16)],
        compiler_params=_params("arbitrary", "arbitrary"),
        name="proj",
    )(x2d, g, cos, sin, w_in, *casts)
    return outs[0], outs[1:]


def _ret_kernel(lg_ref, q_ref, k_ref, v_ref, g_ref, gain_ref, o_ref, sfa_ref, sba_ref, sf_ref, sb_ref):
    c = RET_CHUNK
    n_chunks = q_ref.shape[0] // c
    h = pl.program_id(1)
    lgf = lg_ref[0, h]
    lgb = lg_ref[1, h]

    row = lax.broadcasted_iota(jnp.int32, (c, RET_QK_DIM), 0).astype(F32)
    xi_f = jnp.exp((row + 1.0) * lgf).astype(BF16)
    zeta_f = jnp.exp((c - 1.0 - row) * lgf).astype(BF16)
    xi_b = jnp.exp((c - row) * lgb).astype(BF16)
    zeta_b = jnp.exp(row * lgb).astype(BF16)
    g_f = jnp.exp(jnp.full((1, RET_V_DIM), c, F32) * lgf)
    g_b = jnp.exp(jnp.full((1, RET_V_DIM), c, F32) * lgb)
    di = lax.broadcasted_iota(jnp.int32, (c, c), 0)
    dj = lax.broadcasted_iota(jnp.int32, (c, c), 1)
    diff = (di - dj).astype(F32)
    decay = jnp.exp(jnp.where(diff >= 0, diff * lgf, -diff * lgb))

    def chunk(n):
        r0 = pl.multiple_of(n * c, c)
        return pl.ds(r0, c)

    def scaled(t, s):
        return t * s

    def outer(kz, vn):
        return lax.dot_general(kz, vn, (((0,), (0,)), ((), ())), preferred_element_type=F32)

    sf_ref[...] = jnp.zeros_like(sf_ref)
    sb_ref[...] = jnp.zeros_like(sb_ref)

    def states(step, carry):
        nf = step
        nb = n_chunks - 1 - step
        sf = sf_ref[...]
        sb = sb_ref[...]
        sfa_ref[nf] = sf.astype(BF16)
        sba_ref[nb] = sb.astype(BF16)
        rf, rb = chunk(nf), chunk(nb)
        sf_ref[...] = sf * g_f + outer(scaled(k_ref[rf, :], zeta_f), v_ref[rf, :])
        sb_ref[...] = sb * g_b + outer(scaled(k_ref[rb, :], zeta_b), v_ref[rb, :])
        return carry

    lax.fori_loop(0, n_chunks, states, 0, unroll=2)

    gain = gain_ref[...]

    def fwd(n, carry):
        rows = chunk(n)
        qn, kn, vn = q_ref[rows, :], k_ref[rows, :], v_ref[rows, :]
        scores = lax.dot_general(qn, kn, (((1,), (1,)), ((), ())), preferred_element_type=F32) * decay
        y = (_dot(scores.astype(BF16), vn) + _dot(scaled(qn, xi_f), sfa_ref[n])
             + _dot(scaled(qn, xi_b), sba_ref[n]))
        mu = _lane_mean(y)
        yc = y - mu
        var = _lane_mean(yc * yc)
        yn = yc * lax.rsqrt(var + EPS) * gain
        o_ref[rows, :] = g_ref[rows, :] * yn.astype(BF16)
        return carry

    lax.fori_loop(0, n_chunks, fwd, 0, unroll=2)


def _retention(proj, log_gamma, gain, batch, seq, casts):
    t = proj.shape[0]
    qk_blocks = RET_QK_WIDTH // RET_QK_DIM
    k_off = qk_blocks
    v_off = 2 * RET_QK_WIDTH // RET_V_DIM
    g_off = v_off + RET_V_WIDTH // RET_V_DIM
    cast_specs, cast_shapes = _cast_specs(casts, batch * RET_HEADS, lambda b, h: b * RET_HEADS + h)
    outs = pl.pallas_call(
        _with_casts(_ret_kernel, 6, 1, len(casts)),
        grid=(batch, RET_HEADS),
        in_specs=[
            pl.BlockSpec(memory_space=pltpu.SMEM),
            pl.BlockSpec((seq, RET_QK_DIM), lambda b, h: (b, h)),
            pl.BlockSpec((seq, RET_QK_DIM), lambda b, h: (b, k_off + h)),
            pl.BlockSpec((seq, RET_V_DIM), lambda b, h: (b, v_off + h)),
            pl.BlockSpec((seq, RET_V_DIM), lambda b, h: (b, g_off + h)),
            pl.BlockSpec((1, RET_V_DIM), lambda b, h: (0, h)),
        ] + cast_specs,
        out_specs=[pl.BlockSpec((seq, RET_V_DIM), lambda b, h: (b, h))] + cast_specs,
        out_shape=[jax.ShapeDtypeStruct((t, RET_V_WIDTH), BF16)] + cast_shapes,
        scratch_shapes=[
            pltpu.VMEM((seq // RET_CHUNK, RET_QK_DIM, RET_V_DIM), BF16),
            pltpu.VMEM((seq // RET_CHUNK, RET_QK_DIM, RET_V_DIM), BF16),
            pltpu.VMEM((RET_QK_DIM, RET_V_DIM), F32),
            pltpu.VMEM((RET_QK_DIM, RET_V_DIM), F32),
        ],
        compiler_params=_params("arbitrary", "arbitrary"),
        name="retention",
    )(log_gamma, proj, proj, proj, proj, gain, *casts)
    return outs[0], outs[1:]


def _ffn0_kernel(x_ref, z_ref, wo_ref, g_ref, wg_ref, wu_ref, wd_ref, g2_ref, cc_ref, sc_ref,
                 o_ref, zr_ref, zi_ref):
    h1 = x_ref[...] + _dot(z_ref[...], wo_ref[...])
    hn = _rms(h1, g_ref[...]).astype(BF16)
    d_ff = wg_ref.shape[1]
    part = None
    for lo in range(0, d_ff, FFN0_SUB):
        cols = slice(lo, min(lo + FFN0_SUB, d_ff))
        hidden = (_silu(_dot(hn, wg_ref[:, cols])) * _dot(hn, wu_ref[:, cols])).astype(BF16)
        d = _dot(hidden, wd_ref[cols, :])
        part = d if part is None else part + d
    h2 = h1 + part
    o_ref[...] = h2
    hn2 = _rms(h2, g2_ref[...]).astype(BF16)
    for grp in range(FNET_GROUPS):
        cols = slice(grp * FNET_GROUP_DIM, (grp + 1) * FNET_GROUP_DIM)
        zr_ref[:, cols] = _dot(hn2[:, cols], cc_ref[...]).astype(BF16)
        zi_ref[:, cols] = _dot(hn2[:, cols], sc_ref[...]).astype(BF16)


def _ffn0(x2d, z, w_out, g, w_gate, w_up, w_down, g2, cc, sc, casts):
    t = x2d.shape[0]
    d_ff = w_gate.shape[1]
    tm = FFN0_TM
    once = pl.Buffered(1)
    row = pl.BlockSpec((tm, D_MODEL), lambda i: (i, 0))
    cast_specs, cast_shapes = _cast_specs(casts, t // tm, lambda i: i)
    outs = pl.pallas_call(
        _with_casts(_ffn0_kernel, 10, 3, len(casts)),
        grid=(t // tm,),
        in_specs=[
            row,
            pl.BlockSpec((tm, RET_V_WIDTH), lambda i: (i, 0)),
            pl.BlockSpec((RET_V_WIDTH, D_MODEL), lambda i: (0, 0), pipeline_mode=once),
            pl.BlockSpec((1, D_MODEL), lambda i: (0, 0), pipeline_mode=once),
            pl.BlockSpec((D_MODEL, d_ff), lambda i: (0, 0), pipeline_mode=once),
            pl.BlockSpec((D_MODEL, d_ff), lambda i: (0, 0), pipeline_mode=once),
            pl.BlockSpec((d_ff, D_MODEL), lambda i: (0, 0), pipeline_mode=once),
            pl.BlockSpec((1, D_MODEL), lambda i: (0, 0), pipeline_mode=once),
            pl.BlockSpec((FNET_GROUP_DIM, FNET_GROUP_DIM), lambda i: (0, 0), pipeline_mode=once),
            pl.BlockSpec((FNET_GROUP_DIM, FNET_GROUP_DIM), lambda i: (0, 0), pipeline_mode=once),
        ] + cast_specs,
        out_specs=[row, row, row] + cast_specs,
        out_shape=[jax.ShapeDtypeStruct((t, D_MODEL), F32), jax.ShapeDtypeStruct((t, D_MODEL), BF16),
                   jax.ShapeDtypeStruct((t, D_MODEL), BF16)] + cast_shapes,
        compiler_params=_params("arbitrary"),
        name="ffn0",
    )(x2d, z, w_out, g, w_gate, w_up, w_down, g2, cc, sc, *casts)
    return outs[:3], outs[3:]


def _dft_across_blocks(xs):
    n = len(xs)
    if n == 1:
        return xs
    even = _dft_across_blocks(xs[0::2])
    odd = _dft_across_blocks(xs[1::2])
    out = [None] * n
    for k in range(n // 2):
        er, ei = even[k]
        o_r, o_i = odd[k]
        if k == 0:
            tr, ti = o_r, o_i
            out[k], out[k + n // 2] = (er + tr, ei + ti), (er - tr, ei - ti)
        elif 4 * k == n:
            out[k], out[k + n // 2] = (er + o_i, ei - o_r), (er - o_i, ei + o_r)
        else:
            c, s = math.cos(2.0 * math.pi * k / n), math.sin(2.0 * math.pi * k / n)
            tr = o_r * c + o_i * s
            ti = o_i * c - o_r * s
            out[k], out[k + n // 2] = (er + tr, ei + ti), (er - tr, ei - ti)
    return out


def _fnet_seq_kernel(zr_ref, zi_ref, gc_ref, gs_ref, o_ref, ar_ref, ai_ref, ys_ref):
    blk = FNET_SEQ_BLOCK
    n1 = zr_ref.shape[0] // blk
    rc = FNET_ROW_CHUNK
    lanes = 128

    def body(r, carry):
        r0 = pl.multiple_of(r * rc, rc)
        for l in range(zr_ref.shape[1] // lanes):
            cols = slice(l * lanes, (l + 1) * lanes)
            zs = [(zr_ref[pl.ds(s1 * blk + r0, rc), cols].astype(F32),
                   zi_ref[pl.ds(s1 * blk + r0, rc), cols].astype(F32)) for s1 in range(n1)]
            for k1, (a_r, a_i) in enumerate(_dft_across_blocks(zs)):
                ar_ref[pl.ds(k1 * blk + r0, rc), cols] = a_r.astype(BF16)
                ai_ref[pl.ds(k1 * blk + r0, rc), cols] = a_i.astype(BF16)
        return carry

    lax.fori_loop(0, blk // rc, body, 0)

    for k1 in range(n1):
        rows = slice(k1 * blk, (k1 + 1) * blk)
        y = _dot(gc_ref[rows, :], ar_ref[rows, :]) + _dot(gs_ref[rows, :], ai_ref[rows, :])
        for l in range(y.shape[1] // lanes):
            ys_ref[l, pl.ds(k1, blk, stride=n1), :] = y[:, l * lanes:(l + 1) * lanes]
    for l in range(o_ref.shape[1] // lanes):
        o_ref[:, l * lanes:(l + 1) * lanes] = ys_ref[l]


def _fnet_sequence(zr, zi, gc, gs, batch, seq, casts):
    t = zr.shape[0]
    dt = FNET_DT
    blkspec = pl.BlockSpec((seq, dt), lambda b, c: (b, c))
    tab = pl.BlockSpec((seq, FNET_SEQ_BLOCK), lambda b, c: (0, 0))
    col_tiles = D_MODEL // dt
    cast_specs, cast_shapes = _cast_specs(casts, batch * col_tiles, lambda b, c: b * col_tiles + c)
    outs = pl.pallas_call(
        _with_casts(_fnet_seq_kernel, 4, 1, len(casts)),
        grid=(batch, col_tiles),
        in_specs=[blkspec, blkspec, tab, tab] + cast_specs,
        out_specs=[blkspec] + cast_specs,
        out_shape=[jax.ShapeDtypeStruct((t, D_MODEL), F32)] + cast_shapes,
        scratch_shapes=[pltpu.VMEM((seq, dt), BF16), pltpu.VMEM((seq, dt), BF16),
                        pltpu.VMEM((dt // 128, seq, 128), F32)],
        compiler_params=_params("arbitrary", "arbitrary"),
        name="fnet_sequence",
    )(zr, zi, gc, gs, *casts)
    return outs[0], outs[1:]


def _router_kernel(h_ref, y_ref, w_ref, g_ref, rhi_ref, rlo_ref, h3_ref, hn_ref, idx_ref, wt_ref):
    h3 = h_ref[...] + _dot(y_ref[...].astype(BF16), w_ref[...])
    h3_ref[...] = h3
    hn = _rms(h3, g_ref[...])
    hn_ref[...] = hn
    hi = hn.astype(BF16)
    lo = (hn - hi.astype(F32)).astype(BF16)

    def nt(a, b):
        return lax.dot_general(a, b, (((1,), (1,)), ((), ())), preferred_element_type=F32)

    logits = nt(rhi_ref[...], hi) + (nt(rhi_ref[...], lo) + nt(rlo_ref[...], hi))
    row = lax.broadcasted_iota(jnp.int32, logits.shape, 0)
    m1 = jnp.max(logits, axis=0, keepdims=True)
    i1 = jnp.min(jnp.where(logits == m1, row, N_EXPERTS), axis=0, keepdims=True)
    rest = jnp.where(row == i1, -jnp.inf, logits)
    m2 = jnp.max(rest, axis=0, keepdims=True)
    i2 = jnp.min(jnp.where(rest == m2, row, N_EXPERTS), axis=0, keepdims=True)
    e2 = jnp.exp(m2 - m1)
    inv = 1.0 / (1.0 + e2)
    idx_ref[0:1, :] = i1
    idx_ref[1:2, :] = i2
    wt_ref[0:1, :] = inv
    wt_ref[1:2, :] = e2 * inv


def _router(h2, y, w_fnet, g, r_hi, r_lo):
    t = h2.shape[0]
    tm = ROW_TM
    row = pl.BlockSpec((tm, D_MODEL), lambda i: (i, 0))
    pair = pl.BlockSpec((TOP_K, tm), lambda i: (0, i))
    rtr = pl.BlockSpec((N_EXPERTS, D_MODEL), lambda i: (0, 0))
    return pl.pallas_call(
        _router_kernel,
        grid=(t // tm,),
        in_specs=[row, row, pl.BlockSpec((D_MODEL, D_MODEL), lambda i: (0, 0)),
                  pl.BlockSpec((1, D_MODEL), lambda i: (0, 0)), rtr, rtr],
        out_specs=[row, row, pair, pair],
        out_shape=[jax.ShapeDtypeStruct((t, D_MODEL), F32), jax.ShapeDtypeStruct((t, D_MODEL), F32),
                   jax.ShapeDtypeStruct((TOP_K, t), jnp.int32), jax.ShapeDtypeStruct((TOP_K, t), F32)],
        compiler_params=_params("arbitrary"),
        name="router",
    )(h2, y, w_fnet, g, r_hi, r_lo)


def _routing_tables(top_idx, tm):
    t = top_idx.shape[1]
    n_pairs = TOP_K * t
    n_tiles = n_pairs // tm + N_EXPERTS
    e_flat = top_idx.reshape(n_pairs)
    order = jnp.argsort(e_flat, stable=True).astype(jnp.int32)
    experts = jnp.arange(N_EXPERTS, dtype=jnp.int32)
    counts = jnp.sum((e_flat[:, None] == experts[None, :]).astype(jnp.int32), axis=0)
    off = jnp.cumsum(counts) - counts
    tiles_e = (counts + tm - 1) // tm
    tile_end = jnp.cumsum(tiles_e)
    tile_off = tile_end - tiles_e
    n_used = tile_end[-1]
    last_e = jnp.max(jnp.where(tiles_e > 0, experts, 0))
    tile_ids = jnp.arange(n_tiles, dtype=jnp.int32)
    tile_e = jnp.minimum(jnp.sum((tile_ids[:, None] >= tile_end[None, :]).astype(jnp.int32), axis=1), last_e)
    u = jnp.arange(tm, dtype=jnp.int32)
    rank = (tile_ids - tile_off[tile_e])[:, None] * tm + u[None, :]
    valid = (tile_ids < n_used)[:, None] & (rank < counts[tile_e][:, None])
    src = order[jnp.clip(off[tile_e][:, None] + rank, 0, n_pairs - 1)]
    tok = jnp.where(valid, src % t, 0).astype(jnp.int32)
    spare = n_pairs + jnp.broadcast_to(u[None, :], (1, tm))
    dst = jnp.where(valid, src, spare).astype(jnp.int32)
    tok_next = jnp.concatenate([tok[1:], jnp.zeros((2, tm), jnp.int32)], axis=0)
    dst_prev = jnp.concatenate([spare.astype(jnp.int32), dst], axis=0)
    tile_e = jnp.concatenate([tile_e, tile_e[-1:]]).astype(jnp.int32)
    return (tile_e, n_used.reshape(1).astype(jnp.int32), tok[0].reshape(1, 1, tm),
            tok_next.reshape(n_tiles + 1, 1, tm), dst_prev.reshape(n_tiles + 1, 1, tm))


def _moe_kernel(te_ref, nu_ref, tok0_ref, tokn_ref, dstp_ref, hn_hbm, wg_ref, wu_ref, wd_ref,
                y2_hbm, xbuf, xb, acc, ybuf, gsem, ssem, *, n_ff):
    del te_ref
    i = pl.program_id(0)
    j = pl.program_id(1)
    n_used = nu_ref[0]
    tm = xbuf.shape[0]
    fc = wg_ref.shape[2]
    rows_per_step = tm // n_ff
    spare0 = y2_hbm.shape[0] - tm
    unroll = 8

    def gather_copy(tok, u):
        return pltpu.make_async_copy(hn_hbm.at[pl.ds(tok, 1), :], xbuf.at[pl.ds(u, 1), :], gsem)

    def scatter_copy(u, row):
        return pltpu.make_async_copy(ybuf.at[pl.ds(u, 1), :], y2_hbm.at[pl.ds(row, 1), :], ssem)

    def start_gather(ref):
        def body(u, carry):
            gather_copy(ref[0, 0, u], u).start()
            return carry
        lax.fori_loop(0, tm, body, 0, unroll=unroll)

    def wait_gather():
        pltpu.make_async_copy(hn_hbm.at[pl.ds(0, tm), :], xbuf, gsem).wait()

    def start_scatter_all():
        def body(u, carry):
            scatter_copy(u, dstp_ref[0, 0, u]).start()
            return carry
        lax.fori_loop(0, tm, body, 0, unroll=unroll)

    def wait_scatter():
        pltpu.make_async_copy(ybuf, y2_hbm.at[pl.ds(0, tm), :], ssem).wait()

    @pl.when((i == 0) & (j == 0))
    def _():
        ybuf[...] = jnp.zeros_like(ybuf)
        fill = pltpu.make_async_copy(ybuf, y2_hbm.at[pl.ds(spare0, tm), :], ssem)
        fill.start()
        fill.wait()
        start_gather(tok0_ref)

    def issue_gathers(jj):
        for u in range(jj * rows_per_step, (jj + 1) * rows_per_step):
            gather_copy(tokn_ref[0, 0, u], u).start()

    def ff_step(jj):
        if jj == 0:
            wait_gather()
            xb[...] = xbuf[...].astype(BF16)
            for u in range(tm):
                scatter_copy(u, dstp_ref[0, 0, u]).start(priority=1)
            issue_gathers(0)
        if jj == n_ff - 1:
            wait_scatter()

        x = xb[...]
        part = None
        for lo in range(0, fc, MOE_SUB):
            cols = slice(lo, min(lo + MOE_SUB, fc))
            hidden = (_silu(_dot(x, wg_ref[0, :, cols])) * _dot(x, wu_ref[0, :, cols])).astype(BF16)
            d = _dot(hidden, wd_ref[0, cols, :])
            part = d if part is None else part + d

        if jj == 0:
            acc[...] = part
        elif jj < n_ff - 1:
            acc[...] += part
        else:
            ybuf[...] = acc[...] + part

    for jj in range(n_ff):
        if jj > 0:
            pl.when((i < n_used) & (j == jj))(functools.partial(issue_gathers, jj))
        pl.when((i < n_used) & (j == jj))(functools.partial(ff_step, jj))

    @pl.when((i == n_used) & (j == 0))
    def _():
        start_scatter_all()
        wait_scatter()
        wait_gather()


def _moe(hn, tile_e, n_used, tok0, tok_next, dst_prev, w_gate, w_up, w_down):
    t = hn.shape[0]
    d_ff = w_gate.shape[2]
    tm, fc = MOE_TM, MOE_FC
    n_steps = tok_next.shape[0]
    n_ff = d_ff // fc
    assert n_ff >= 2 and n_ff * fc == d_ff and tm % n_ff == 0

    def ff_step(i, j, nu):
        return jnp.where(i < nu[0], j, n_ff - 1)

    grid_spec = pltpu.PrefetchScalarGridSpec(
        num_scalar_prefetch=2,
        grid=(n_steps, n_ff),
        in_specs=[
            pl.BlockSpec((1, 1, tm), lambda i, j, te, nu: (0, 0, 0), memory_space=pltpu.SMEM),
            pl.BlockSpec((1, 1, tm), lambda i, j, te, nu: (i, 0, 0), memory_space=pltpu.SMEM),
            pl.BlockSpec((1, 1, tm), lambda i, j, te, nu: (i, 0, 0), memory_space=pltpu.SMEM),
            pl.BlockSpec(memory_space=pl.ANY),
            pl.BlockSpec((1, D_MODEL, fc), lambda i, j, te, nu: (te[i], 0, ff_step(i, j, nu))),
            pl.BlockSpec((1, D_MODEL, fc), lambda i, j, te, nu: (te[i], 0, ff_step(i, j, nu))),
            pl.BlockSpec((1, fc, D_MODEL), lambda i, j, te, nu: (te[i], ff_step(i, j, nu), 0)),
        ],
        out_specs=pl.BlockSpec(memory_space=pl.ANY),
        scratch_shapes=[
            pltpu.VMEM((tm, D_MODEL), F32),
            pltpu.VMEM((tm, D_MODEL), BF16),
            pltpu.VMEM((tm, D_MODEL), F32),
            pltpu.VMEM((tm, D_MODEL), F32),
            pltpu.SemaphoreType.DMA(()),
            pltpu.SemaphoreType.DMA(()),
        ],
    )
    return pl.pallas_call(
        functools.partial(_moe_kernel, n_ff=n_ff),
        grid_spec=grid_spec,
        out_shape=jax.ShapeDtypeStruct((TOP_K * t + tm, D_MODEL), F32),
        compiler_params=_params("arbitrary", "arbitrary"),
        name="moe",
    )(tile_e, n_used, tok0, tok_next, dst_prev, hn, w_gate, w_up, w_down)


def _final_kernel(h_ref, ya_ref, yb_ref, wt_ref, g_ref, o_ref):
    w = wt_ref[...]
    moe = ya_ref[...] * w[:, 0:1] + yb_ref[...] * w[:, 1:2]
    o_ref[...] = _rms(h_ref[...] + moe, g_ref[...])


def _final(h3, y2, top_w, g):
    t = h3.shape[0]
    tm = ROW_TM
    second = t // tm
    row = pl.BlockSpec((tm, D_MODEL), lambda i: (i, 0))
    return pl.pallas_call(
        _final_kernel,
        grid=(t // tm,),
        in_specs=[row, row, pl.BlockSpec((tm, D_MODEL), lambda i: (i + second, 0)),
                  pl.BlockSpec((tm, TOP_K), lambda i: (i, 0)), pl.BlockSpec((1, D_MODEL), lambda i: (0, 0))],
        out_specs=row,
        out_shape=jax.ShapeDtypeStruct((t, D_MODEL), F32),
        compiler_params=_params("arbitrary"),
        name="final_norm",
    )(h3, y2, y2, top_w, g)


def _rotary_tables(seq):
    half = RET_QK_DIM // 2
    inv = ROPE_BASE ** (-np.arange(half, dtype=np.float64) / half)
    ang = np.arange(seq, dtype=np.float64)[:, None] * inv[None, :]
    return jnp.asarray(np.cos(ang), F32), jnp.asarray(np.sin(ang), F32)


def _channel_dft_tables(n, scale):
    k = np.arange(n, dtype=np.int64)
    ang = ((k[:, None] * k[None, :]) % n) * (2.0 * math.pi / n)
    return (jnp.asarray(np.cos(ang) * scale, F32).astype(BF16),
            jnp.asarray(-np.sin(ang) * scale, F32).astype(BF16))


def _sequence_dft_tables(seq, scale):
    blk = FNET_SEQ_BLOCK
    n1 = seq // blk
    k1 = np.arange(n1, dtype=np.int64)[:, None, None]
    k2 = np.arange(blk, dtype=np.int64)[None, :, None]
    s2 = np.arange(blk, dtype=np.int64)[None, None, :]
    ang = (((k1 + n1 * k2) * s2) % seq) * (2.0 * math.pi / seq)
    gc = jnp.asarray((np.cos(ang) * scale).reshape(n1 * blk, blk), F32).astype(BF16)
    gs = jnp.asarray((np.sin(ang) * scale).reshape(n1 * blk, blk), F32).astype(BF16)
    return gc, gs


def kernel(x, mix_norm, ffn_norm, ret_w_in, ret_decay_logit, ret_gn_gain, ret_w_out,
           dense_w_gate, dense_w_up, dense_w_down, fnet_w_out, moe_router,
           moe_w_gate, moe_w_up, moe_w_down, final_norm):
    batch, seq, d = x.shape
    t = batch * seq
    n1 = seq // FNET_SEQ_BLOCK
    assert d == D_MODEL and seq % RET_CHUNK == 0 and seq % PROJ_TM == 0
    assert n1 * FNET_SEQ_BLOCK == seq and n1 & (n1 - 1) == 0
    x2d = x.reshape(t, d)

    cos, sin = _rotary_tables(seq)
    n_exp, _, d_ffe = moe_w_gate[0].shape
    proj, (moe_wd,) = _proj(x2d, mix_norm[0:1], cos, sin, ret_w_in[0].astype(BF16), seq,
                            [moe_w_down[0].reshape(n_exp * d_ffe, d)])
    log_gamma = jax.nn.log_sigmoid(ret_decay_logit[0].astype(F32))
    z, (w_out, d_wg, d_wu, d_wd) = _retention(
        proj, log_gamma, ret_gn_gain[0:1], batch, seq,
        [ret_w_out[0], dense_w_gate[0], dense_w_up[0], dense_w_down[0]])
    cc, sc = _channel_dft_tables(FNET_GROUP_DIM, FNET_GROUP_DIM ** -0.5)
    gc, gs = _sequence_dft_tables(seq, seq ** -0.5)
    (h2, zr, zi), (moe_wu, w_fnet) = _ffn0(
        x2d, z, w_out, ffn_norm[0:1], d_wg, d_wu, d_wd, mix_norm[1:2], cc, sc,
        [moe_w_up[0].reshape(n_exp * d, d_ffe), fnet_w_out[0]])

    y, (moe_wg,) = _fnet_sequence(zr, zi, gc, gs, batch, seq, [moe_w_gate[0].reshape(n_exp * d, d_ffe)])
    router_t = moe_router[0].T
    r_hi = router_t.astype(BF16)
    r_lo = (router_t - r_hi.astype(F32)).astype(BF16)
    h3, hn3, top_idx, top_w = _router(h2, y, w_fnet, ffn_norm[1:2], r_hi, r_lo)

    tile_e, n_used, tok0, tok_next, dst_prev = _routing_tables(top_idx, MOE_TM)
    y2 = _moe(hn3, tile_e, n_used, tok0, tok_next, dst_prev,
              moe_wg.reshape(n_exp, d, d_ffe), moe_wu.reshape(n_exp, d, d_ffe),
              moe_wd.reshape(n_exp, d_ffe, d))
    out = _final(h3, y2, top_w.T, final_norm.reshape(1, d))
    return out.reshape(batch, seq, d)
```

```python
import functools
import math

import jax
import jax.numpy as jnp
import numpy as np
from jax import lax
from jax.experimental import pallas as pl
from jax.experimental.pallas import tpu as pltpu

F32 = jnp.float32
BF16 = jnp.bfloat16

D_MODEL = 1024
RET_HEADS = 4
RET_QK_DIM = D_MODEL // RET_HEADS
RET_V_DIM = 2 * RET_QK_DIM
RET_QK_WIDTH = RET_HEADS * RET_QK_DIM
RET_V_WIDTH = RET_HEADS * RET_V_DIM
RET_IN_WIDTH = 2 * RET_QK_WIDTH + 2 * RET_V_WIDTH
ROPE_BASE = 10000.0
FNET_GROUPS = 4
FNET_GROUP_DIM = D_MODEL // FNET_GROUPS
N_EXPERTS = 8
TOP_K = 2
EPS = 1e-6

RET_CHUNK = 256
VMEM_LIMIT_BYTES = 56 * 1024 * 1024

PROJ_TM, PROJ_TN = 1024, 2048
FFN0_TM, FFN0_SUB = 512, 512
ROW_TM = 1024
FNET_SEQ_BLOCK = 256
FNET_DT = 256
FNET_ROW_CHUNK = 16
MOE_TM, MOE_FC, MOE_SUB = 1024, 1792, 512


def _params(*semantics):
    return pltpu.CompilerParams(dimension_semantics=semantics, vmem_limit_bytes=VMEM_LIMIT_BYTES)


def _dot(a, b):
    return jnp.dot(a, b, preferred_element_type=F32)


def _lane_mean(x):
    lanes = 128
    n = x.shape[-1]
    acc = x[:, 0:lanes]
    for lo in range(lanes, n, lanes):
        acc = acc + x[:, lo:lo + lanes]
    return jnp.sum(acc, axis=-1, keepdims=True) * (1.0 / n)


def _rms(x, g):
    return x * lax.rsqrt(_lane_mean(x * x) + EPS) * g


def _silu(a):
    return a * (1.0 / (1.0 + jnp.exp(-a)))


BF16_ROW_TILE = 16


def _cast_specs(arrays, n_steps, step_of):
    specs, shapes = [], []
    for a in arrays:
        rows, cols = a.shape
        n_blocks = n_steps
        while rows % (n_blocks * BF16_ROW_TILE):
            n_blocks //= 2
        assert n_blocks >= 1

        def index(*ids, n_blocks=n_blocks):
            return (jnp.minimum(step_of(*ids), n_blocks - 1), 0)

        specs.append(pl.BlockSpec((rows // n_blocks, cols), index))
        shapes.append(jax.ShapeDtypeStruct(a.shape, BF16))
    return specs, shapes


def _with_casts(body, n_in, n_out, n_cast):
    def kernel(*refs):
        ins, refs = refs[:n_in], refs[n_in:]
        srcs, refs = refs[:n_cast], refs[n_cast:]
        outs, refs = refs[:n_out], refs[n_out:]
        dsts, scratch = refs[:n_cast], refs[n_cast:]
        for src, dst in zip(srcs, dsts):
            dst[...] = src[...].astype(BF16)
        body(*ins, *outs, *scratch)
    return kernel


def _proj_kernel(x_ref, g_ref, cos_ref, sin_ref, w_ref, o_ref, hn_ref):
    j = pl.program_id(1)

    @pl.when(j == 0)
    def _():
        hn_ref[...] = _rms(x_ref[...], g_ref[...]).astype(BF16)

    tn = o_ref.shape[1]
    half = RET_QK_DIM // 2
    k_start, v_start, g_start = RET_QK_WIDTH, 2 * RET_QK_WIDTH, 2 * RET_QK_WIDTH + RET_V_WIDTH

    def column_tile(jj):
        hn = hn_ref[...]
        for h in range(tn // RET_QK_DIM):
            lo = h * RET_QK_DIM
            col = jj * tn + lo
            y = _dot(hn, w_ref[:, col:col + RET_QK_DIM])
            if col < v_start:
                scale = RET_QK_DIM ** -0.5 if col >= k_start else 1.0
                cos = cos_ref[...] * scale
                sin = sin_ref[...] * scale
                t1 = y[:, :half]
                t2 = y[:, half:]
                o_ref[:, lo:lo + half] = (t1 * cos - t2 * sin).astype(BF16)
                o_ref[:, lo + half:lo + RET_QK_DIM] = (t1 * sin + t2 * cos).astype(BF16)
            elif col < g_start:
                o_ref[:, lo:lo + RET_QK_DIM] = y.astype(BF16)
            else:
                o_ref[:, lo:lo + RET_QK_DIM] = _silu(y).astype(BF16)

    for jj in range(w_ref.shape[1] // tn):
        pl.when(j == jj)(functools.partial(column_tile, jj))


def _proj(x2d, g, cos, sin, w_in, seq, casts):
    t = x2d.shape[0]
    tm, tn = PROJ_TM, PROJ_TN
    pos_tiles = seq // tm
    cast_specs, cast_shapes = _cast_specs(casts, t // tm, lambda i, j: i)
    outs = pl.pallas_call(
        _with_casts(_proj_kernel, 5, 1, len(casts)),
        grid=(t // tm, RET_IN_WIDTH // tn),
        in_specs=[
            pl.BlockSpec((tm, D_MODEL), lambda i, j: (i, 0)),
            pl.BlockSpec((1, D_MODEL), lambda i, j: (0, 0)),
            pl.BlockSpec((tm, RET_QK_DIM // 2), lambda i, j: (i % pos_tiles, 0)),
            pl.BlockSpec((tm, RET_QK_DIM // 2), lambda i, j: (i % pos_tiles, 0)),
            pl.BlockSpec((D_MODEL, RET_IN_WIDTH), lambda i, j: (0, 0), pipeline_mode=pl.Buffered(1)),
        ] + cast_specs,
        out_specs=[pl.BlockSpec((tm, tn), lambda i, j: (i, j))] + cast_specs,
        out_shape=[jax.ShapeDtypeStruct((t, RET_IN_WIDTH), BF16)] + cast_shapes,
        scratch_shapes=[pltpu.VMEM((tm, D_MODEL), BF16)],
        compiler_params=_params("arbitrary", "arbitrary"),
        name="proj",
    )(x2d, g, cos, sin, w_in, *casts)
    return outs[0], outs[1:]


def _ret_kernel(lg_ref, q_ref, k_ref, v_ref, g_ref, gain_ref, o_ref, sfa_ref, sba_ref, sf_ref, sb_ref):
    c = RET_CHUNK
    n_chunks = q_ref.shape[0] // c
    h = pl.program_id(1)
    lgf = lg_ref[0, h]
    lgb = lg_ref[1, h]

    row = lax.broadcasted_iota(jnp.int32, (c, RET_QK_DIM), 0).astype(F32)
    xi_f = jnp.exp((row + 1.0) * lgf).astype(BF16)
    zeta_f = jnp.exp((c - 1.0 - row) * lgf).astype(BF16)
    xi_b = jnp.exp((c - row) * lgb).astype(BF16)
    zeta_b = jnp.exp(row * lgb).astype(BF16)
    g_f = jnp.exp(jnp.full((1, RET_V_DIM), c, F32) * lgf)
    g_b = jnp.exp(jnp.full((1, RET_V_DIM), c, F32) * lgb)
    di = lax.broadcasted_iota(jnp.int32, (c, c), 0)
    dj = lax.broadcasted_iota(jnp.int32, (c, c), 1)
    diff = (di - dj).astype(F32)
    decay = jnp.exp(jnp.where(diff >= 0, diff * lgf, -diff * lgb))

    def chunk(n):
        r0 = pl.multiple_of(n * c, c)
        return pl.ds(r0, c)

    def scaled(t, s):
        return t * s

    def outer(kz, vn):
        return lax.dot_general(kz, vn, (((0,), (0,)), ((), ())), preferred_element_type=F32)

    sf_ref[...] = jnp.zeros_like(sf_ref)
    sb_ref[...] = jnp.zeros_like(sb_ref)

    def states(step, carry):
        nf = step
        nb = n_chunks - 1 - step
        sf = sf_ref[...]
        sb = sb_ref[...]
        sfa_ref[nf] = sf.astype(BF16)
        sba_ref[nb] = sb.astype(BF16)
        rf, rb = chunk(nf), chunk(nb)
        sf_ref[...] = sf * g_f + outer(scaled(k_ref[rf, :], zeta_f), v_ref[rf, :])
        sb_ref[...] = sb * g_b + outer(scaled(k_ref[rb, :], zeta_b), v_ref[rb, :])
        return carry

    lax.fori_loop(0, n_chunks, states, 0, unroll=2)

    gain = gain_ref[...]

    def fwd(n, carry):
        rows = chunk(n)
        qn, kn, vn = q_ref[rows, :], k_ref[rows, :], v_ref[rows, :]
        scores = lax.dot_general(qn, kn, (((1,), (1,)), ((), ())), preferred_element_type=F32) * decay
        y = (_dot(scores.astype(BF16), vn) + _dot(scaled(qn, xi_f), sfa_ref[n])
             + _dot(scaled(qn, xi_b), sba_ref[n]))
        mu = _lane_mean(y)
        yc = y - mu
        var = _lane_mean(yc * yc)
        yn = yc * lax.rsqrt(var + EPS) * gain
        o_ref[rows, :] = g_ref[rows, :] * yn.astype(BF16)
        return carry

    lax.fori_loop(0, n_chunks, fwd, 0, unroll=4)


def _retention(proj, log_gamma, gain, batch, seq, casts):
    t = proj.shape[0]
    qk_blocks = RET_QK_WIDTH // RET_QK_DIM
    k_off = qk_blocks
    v_off = 2 * RET_QK_WIDTH // RET_V_DIM
    g_off = v_off + RET_V_WIDTH // RET_V_DIM
    cast_specs, cast_shapes = _cast_specs(casts, batch * RET_HEADS, lambda b, h: b * RET_HEADS + h)
    outs = pl.pallas_call(
        _with_casts(_ret_kernel, 6, 1, len(casts)),
        grid=(batch, RET_HEADS),
        in_specs=[
            pl.BlockSpec(memory_space=pltpu.SMEM),
            pl.BlockSpec((seq, RET_QK_DIM), lambda b, h: (b, h)),
            pl.BlockSpec((seq, RET_QK_DIM), lambda b, h: (b, k_off + h)),
            pl.BlockSpec((seq, RET_V_DIM), lambda b, h: (b, v_off + h)),
            pl.BlockSpec((seq, RET_V_DIM), lambda b, h: (b, g_off + h)),
            pl.BlockSpec((1, RET_V_DIM), lambda b, h: (0, h)),
        ] + cast_specs,
        out_specs=[pl.BlockSpec((seq, RET_V_DIM), lambda b, h: (b, h))] + cast_specs,
        out_shape=[jax.ShapeDtypeStruct((t, RET_V_WIDTH), BF16)] + cast_shapes,
        scratch_shapes=[
            pltpu.VMEM((seq // RET_CHUNK, RET_QK_DIM, RET_V_DIM), BF16),
            pltpu.VMEM((seq // RET_CHUNK, RET_QK_DIM, RET_V_DIM), BF16),
            pltpu.VMEM((RET_QK_DIM, RET_V_DIM), F32),
            pltpu.VMEM((RET_QK_DIM, RET_V_DIM), F32),
        ],
        compiler_params=_params("arbitrary", "arbitrary"),
        name="retention",
    )(log_gamma, proj, proj, proj, proj, gain, *casts)
    return outs[0], outs[1:]


def _ffn0_kernel(x_ref, z_ref, wo_ref, g_ref, wg_ref, wu_ref, wd_ref, g2_ref, cc_ref, sc_ref,
                 o_ref, zr_ref, zi_ref):
    h1 = x_ref[...] + _dot(z_ref[...], wo_ref[...])
    hn = _rms(h1, g_ref[...]).astype(BF16)
    d_ff = wg_ref.shape[1]
    part = None
    for lo in range(0, d_ff, FFN0_SUB):
        cols = slice(lo, min(lo + FFN0_SUB, d_ff))
        hidden = (_silu(_dot(hn, wg_ref[:, cols])) * _dot(hn, wu_ref[:, cols])).astype(BF16)
        d = _dot(hidden, wd_ref[cols, :])
        part = d if part is None else part + d
    h2 = h1 + part
    o_ref[...] = h2
    hn2 = _rms(h2, g2_ref[...]).astype(BF16)
    for grp in range(FNET_GROUPS):
        cols = slice(grp * FNET_GROUP_DIM, (grp + 1) * FNET_GROUP_DIM)
        zr_ref[:, cols] = _dot(hn2[:, cols], cc_ref[...]).astype(BF16)
        zi_ref[:, cols] = _dot(hn2[:, cols], sc_ref[...]).astype(BF16)


def _ffn0(x2d, z, w_out, g, w_gate, w_up, w_down, g2, cc, sc, casts):
    t = x2d.shape[0]
    d_ff = w_gate.shape[1]
    tm = FFN0_TM
    once = pl.Buffered(1)
    row = pl.BlockSpec((tm, D_MODEL), lambda i: (i, 0))
    cast_specs, cast_shapes = _cast_specs(casts, t // tm, lambda i: i)
    outs = pl.pallas_call(
        _with_casts(_ffn0_kernel, 10, 3, len(casts)),
        grid=(t // tm,),
        in_specs=[
            row,
            pl.BlockSpec((tm, RET_V_WIDTH), lambda i: (i, 0)),
            pl.BlockSpec((RET_V_WIDTH, D_MODEL), lambda i: (0, 0), pipeline_mode=once),
            pl.BlockSpec((1, D_MODEL), lambda i: (0, 0), pipeline_mode=once),
            pl.BlockSpec((D_MODEL, d_ff), lambda i: (0, 0), pipeline_mode=once),
            pl.BlockSpec((D_MODEL, d_ff), lambda i: (0, 0), pipeline_mode=once),
            pl.BlockSpec((d_ff, D_MODEL), lambda i: (0, 0), pipeline_mode=once),
            pl.BlockSpec((1, D_MODEL), lambda i: (0, 0), pipeline_mode=once),
            pl.BlockSpec((FNET_GROUP_DIM, FNET_GROUP_DIM), lambda i: (0, 0), pipeline_mode=once),
            pl.BlockSpec((FNET_GROUP_DIM, FNET_GROUP_DIM), lambda i: (0, 0), pipeline_mode=once),
        ] + cast_specs,
        out_specs=[row, row, row] + cast_specs,
        out_shape=[jax.ShapeDtypeStruct((t, D_MODEL), F32), jax.ShapeDtypeStruct((t, D_MODEL), BF16),
                   jax.ShapeDtypeStruct((t, D_MODEL), BF16)] + cast_shapes,
        compiler_params=_params("arbitrary"),
        name="ffn0",
    )(x2d, z, w_out, g, w_gate, w_up, w_down, g2, cc, sc, *casts)
    return outs[:3], outs[3:]


def _dft_across_blocks(xs):
    n = len(xs)
    if n == 1:
        return xs
    even = _dft_across_blocks(xs[0::2])
    odd = _dft_across_blocks(xs[1::2])
    out = [None] * n
    for k in range(n // 2):
        er, ei = even[k]
        o_r, o_i = odd[k]
        if k == 0:
            tr, ti = o_r, o_i
            out[k], out[k + n // 2] = (er + tr, ei + ti), (er - tr, ei - ti)
        elif 4 * k == n:
            out[k], out[k + n // 2] = (er + o_i, ei - o_r), (er - o_i, ei + o_r)
        else:
            c, s = math.cos(2.0 * math.pi * k / n), math.sin(2.0 * math.pi * k / n)
            tr = o_r * c + o_i * s
            ti = o_i * c - o_r * s
            out[k], out[k + n // 2] = (er + tr, ei + ti), (er - tr, ei - ti)
    return out


def _fnet_seq_kernel(zr_ref, zi_ref, gc_ref, gs_ref, o_ref, ar_ref, ai_ref, ys_ref):
    blk = FNET_SEQ_BLOCK
    n1 = zr_ref.shape[0] // blk
    rc = FNET_ROW_CHUNK
    lanes = 128

    def body(r, carry):
        r0 = pl.multiple_of(r * rc, rc)
        for l in range(zr_ref.shape[1] // lanes):
            cols = slice(l * lanes, (l + 1) * lanes)
            zs = [(zr_ref[pl.ds(s1 * blk + r0, rc), cols].astype(F32),
                   zi_ref[pl.ds(s1 * blk + r0, rc), cols].astype(F32)) for s1 in range(n1)]
            for k1, (a_r, a_i) in enumerate(_dft_across_blocks(zs)):
                ar_ref[pl.ds(k1 * blk + r0, rc), cols] = a_r.astype(BF16)
                ai_ref[pl.ds(k1 * blk + r0, rc), cols] = a_i.astype(BF16)
        return carry

    lax.fori_loop(0, blk // rc, body, 0, unroll=2)

    for k1 in range(n1):
        rows = slice(k1 * blk, (k1 + 1) * blk)
        y = _dot(gc_ref[rows, :], ar_ref[rows, :]) + _dot(gs_ref[rows, :], ai_ref[rows, :])
        for l in range(y.shape[1] // lanes):
            ys_ref[l, pl.ds(k1, blk, stride=n1), :] = y[:, l * lanes:(l + 1) * lanes]
    for l in range(o_ref.shape[1] // lanes):
        o_ref[:, l * lanes:(l + 1) * lanes] = ys_ref[l]


def _fnet_sequence(zr, zi, gc, gs, batch, seq, casts):
    t = zr.shape[0]
    dt = FNET_DT
    blkspec = pl.BlockSpec((seq, dt), lambda b, c: (b, c))
    tab = pl.BlockSpec((seq, FNET_SEQ_BLOCK), lambda b, c: (0, 0))
    col_tiles = D_MODEL // dt
    cast_specs, cast_shapes = _cast_specs(casts, batch * col_tiles, lambda b, c: b * col_tiles + c)
    outs = pl.pallas_call(
        _with_casts(_fnet_seq_kernel, 4, 1, len(casts)),
        grid=(batch, col_tiles),
        in_specs=[blkspec, blkspec, tab, tab] + cast_specs,
        out_specs=[blkspec] + cast_specs,
        out_shape=[jax.ShapeDtypeStruct((t, D_MODEL), F32)] + cast_shapes,
        scratch_shapes=[pltpu.VMEM((seq, dt), BF16), pltpu.VMEM((seq, dt), BF16),
                        pltpu.VMEM((dt // 128, seq, 128), F32)],
        compiler_params=_params("arbitrary", "arbitrary"),
        name="fnet_sequence",
    )(zr, zi, gc, gs, *casts)
    return outs[0], outs[1:]


def _router_kernel(h_ref, y_ref, w_ref, g_ref, rhi_ref, rlo_ref, h3_ref, hn_ref, idx_ref, wt_ref):
    h3 = h_ref[...] + _dot(y_ref[...].astype(BF16), w_ref[...])
    h3_ref[...] = h3
    hn = _rms(h3, g_ref[...])
    hn_ref[...] = hn
    hi = hn.astype(BF16)
    lo = (hn - hi.astype(F32)).astype(BF16)

    def nt(a, b):
        return lax.dot_general(a, b, (((1,), (1,)), ((), ())), preferred_element_type=F32)

    logits = nt(rhi_ref[...], hi) + (nt(rhi_ref[...], lo) + nt(rlo_ref[...], hi))
    row = lax.broadcasted_iota(jnp.int32, logits.shape, 0)
    m1 = jnp.max(logits, axis=0, keepdims=True)
    i1 = jnp.min(jnp.where(logits == m1, row, N_EXPERTS), axis=0, keepdims=True)
    rest = jnp.where(row == i1, -jnp.inf, logits)
    m2 = jnp.max(rest, axis=0, keepdims=True)
    i2 = jnp.min(jnp.where(rest == m2, row, N_EXPERTS), axis=0, keepdims=True)
    e2 = jnp.exp(m2 - m1)
    inv = 1.0 / (1.0 + e2)
    idx_ref[0:1, :] = i1
    idx_ref[1:2, :] = i2
    wt_ref[0:1, :] = inv
    wt_ref[1:2, :] = e2 * inv


def _router(h2, y, w_fnet, g, r_hi, r_lo):
    t = h2.shape[0]
    tm = ROW_TM
    row = pl.BlockSpec((tm, D_MODEL), lambda i: (i, 0))
    pair = pl.BlockSpec((TOP_K, tm), lambda i: (0, i))
    rtr = pl.BlockSpec((N_EXPERTS, D_MODEL), lambda i: (0, 0))
    return pl.pallas_call(
        _router_kernel,
        grid=(t // tm,),
        in_specs=[row, row, pl.BlockSpec((D_MODEL, D_MODEL), lambda i: (0, 0)),
                  pl.BlockSpec((1, D_MODEL), lambda i: (0, 0)), rtr, rtr],
        out_specs=[row, row, pair, pair],
        out_shape=[jax.ShapeDtypeStruct((t, D_MODEL), F32), jax.ShapeDtypeStruct((t, D_MODEL), F32),
                   jax.ShapeDtypeStruct((TOP_K, t), jnp.int32), jax.ShapeDtypeStruct((TOP_K, t), F32)],
        compiler_params=_params("arbitrary"),
        name="router",
    )(h2, y, w_fnet, g, r_hi, r_lo)


def _routing_tables(top_idx, tm):
    t = top_idx.shape[1]
    n_pairs = TOP_K * t
    n_tiles = n_pairs // tm + N_EXPERTS
    e_flat = top_idx.reshape(n_pairs)
    order = jnp.argsort(e_flat, stable=True).astype(jnp.int32)
    experts = jnp.arange(N_EXPERTS, dtype=jnp.int32)
    counts = jnp.sum((e_flat[:, None] == experts[None, :]).astype(jnp.int32), axis=0)
    off = jnp.cumsum(counts) - counts
    tiles_e = (counts + tm - 1) // tm
    tile_end = jnp.cumsum(tiles_e)
    tile_off = tile_end - tiles_e
    n_used = tile_end[-1]
    last_e = jnp.max(jnp.where(tiles_e > 0, experts, 0))
    tile_ids = jnp.arange(n_tiles, dtype=jnp.int32)
    tile_e = jnp.minimum(jnp.sum((tile_ids[:, None] >= tile_end[None, :]).astype(jnp.int32), axis=1), last_e)
    u = jnp.arange(tm, dtype=jnp.int32)
    rank = (tile_ids - tile_off[tile_e])[:, None] * tm + u[None, :]
    valid = (tile_ids < n_used)[:, None] & (rank < counts[tile_e][:, None])
    src = order[jnp.clip(off[tile_e][:, None] + rank, 0, n_pairs - 1)]
    tok = jnp.where(valid, src % t, 0).astype(jnp.int32)
    spare = n_pairs + jnp.broadcast_to(u[None, :], (1, tm))
    dst = jnp.where(valid, src, spare).astype(jnp.int32)
    tok_next = jnp.concatenate([tok[1:], jnp.zeros((2, tm), jnp.int32)], axis=0)
    dst_prev = jnp.concatenate([spare.astype(jnp.int32), dst], axis=0)
    tile_e = jnp.concatenate([tile_e, tile_e[-1:]]).astype(jnp.int32)
    return (tile_e, n_used.reshape(1).astype(jnp.int32), tok[0].reshape(1, 1, tm),
            tok_next.reshape(n_tiles + 1, 1, tm), dst_prev.reshape(n_tiles + 1, 1, tm))


def _moe_kernel(te_ref, nu_ref, tok0_ref, tokn_ref, dstp_ref, hn_hbm, wg_ref, wu_ref, wd_ref,
                y2_hbm, xbuf, xb, acc, ybuf, gsem, ssem, *, n_ff):
    del te_ref
    i = pl.program_id(0)
    j = pl.program_id(1)
    n_used = nu_ref[0]
    tm = xbuf.shape[0]
    fc = wg_ref.shape[2]
    rows_per_step = tm // n_ff
    spare0 = y2_hbm.shape[0] - tm
    unroll = 8

    def gather_copy(tok, u):
        return pltpu.make_async_copy(hn_hbm.at[pl.ds(tok, 1), :], xbuf.at[pl.ds(u, 1), :], gsem)

    def scatter_copy(u, row):
        return pltpu.make_async_copy(ybuf.at[pl.ds(u, 1), :], y2_hbm.at[pl.ds(row, 1), :], ssem)

    def start_gather(ref):
        def body(u, carry):
            gather_copy(ref[0, 0, u], u).start()
            return carry
        lax.fori_loop(0, tm, body, 0, unroll=unroll)

    def wait_gather():
        pltpu.make_async_copy(hn_hbm.at[pl.ds(0, tm), :], xbuf, gsem).wait()

    def start_scatter_all():
        def body(u, carry):
            scatter_copy(u, dstp_ref[0, 0, u]).start()
            return carry
        lax.fori_loop(0, tm, body, 0, unroll=unroll)

    def wait_scatter():
        pltpu.make_async_copy(ybuf, y2_hbm.at[pl.ds(0, tm), :], ssem).wait()

    @pl.when((i == 0) & (j == 0))
    def _():
        ybuf[...] = jnp.zeros_like(ybuf)
        fill = pltpu.make_async_copy(ybuf, y2_hbm.at[pl.ds(spare0, tm), :], ssem)
        fill.start()
        fill.wait()
        start_gather(tok0_ref)

    def issue_gathers(jj):
        for u in range(jj * rows_per_step, (jj + 1) * rows_per_step):
            gather_copy(tokn_ref[0, 0, u], u).start()

    def ff_step(jj):
        if jj == 0:
            wait_gather()
            xb[...] = xbuf[...].astype(BF16)
            for u in range(tm):
                scatter_copy(u, dstp_ref[0, 0, u]).start(priority=1)
            issue_gathers(0)
        if jj == n_ff - 1:
            wait_scatter()

        x = xb[...]
        part = None
        for lo in range(0, fc, MOE_SUB):
            cols = slice(lo, min(lo + MOE_SUB, fc))
            hidden = (_silu(_dot(x, wg_ref[0, :, cols])) * _dot(x, wu_ref[0, :, cols])).astype(BF16)
            d = _dot(hidden, wd_ref[0, cols, :])
            part = d if part is None else part + d

        if jj == 0:
            acc[...] = part
        elif jj < n_ff - 1:
            acc[...] += part
        else:
            ybuf[...] = acc[...] + part

    for jj in range(n_ff):
        if jj > 0:
            pl.when((i < n_used) & (j == jj))(functools.partial(issue_gathers, jj))
        pl.when((i < n_used) & (j == jj))(functools.partial(ff_step, jj))

    @pl.when((i == n_used) & (j == 0))
    def _():
        start_scatter_all()
        wait_scatter()
        wait_gather()


def _moe(hn, tile_e, n_used, tok0, tok_next, dst_prev, w_gate, w_up, w_down):
    t = hn.shape[0]
    d_ff = w_gate.shape[2]
    tm, fc = MOE_TM, MOE_FC
    n_steps = tok_next.shape[0]
    n_ff = d_ff // fc
    assert n_ff >= 2 and n_ff * fc == d_ff and tm % n_ff == 0

    def ff_step(i, j, nu):
        return jnp.where(i < nu[0], j, n_ff - 1)

    grid_spec = pltpu.PrefetchScalarGridSpec(
        num_scalar_prefetch=2,
        grid=(n_steps, n_ff),
        in_specs=[
            pl.BlockSpec((1, 1, tm), lambda i, j, te, nu: (0, 0, 0), memory_space=pltpu.SMEM),
            pl.BlockSpec((1, 1, tm), lambda i, j, te, nu: (i, 0, 0), memory_space=pltpu.SMEM),
            pl.BlockSpec((1, 1, tm), lambda i, j, te, nu: (i, 0, 0), memory_space=pltpu.SMEM),
            pl.BlockSpec(memory_space=pl.ANY),
            pl.BlockSpec((1, D_MODEL, fc), lambda i, j, te, nu: (te[i], 0, ff_step(i, j, nu))),
            pl.BlockSpec((1, D_MODEL, fc), lambda i, j, te, nu: (te[i], 0, ff_step(i, j, nu))),
            pl.BlockSpec((1, fc, D_MODEL), lambda i, j, te, nu: (te[i], ff_step(i, j, nu), 0)),
        ],
        out_specs=pl.BlockSpec(memory_space=pl.ANY),
        scratch_shapes=[
            pltpu.VMEM((tm, D_MODEL), F32),
            pltpu.VMEM((tm, D_MODEL), BF16),
            pltpu.VMEM((tm, D_MODEL), F32),
            pltpu.VMEM((tm, D_MODEL), F32),
            pltpu.SemaphoreType.DMA(()),
            pltpu.SemaphoreType.DMA(()),
        ],
    )
    return pl.pallas_call(
        functools.partial(_moe_kernel, n_ff=n_ff),
        grid_spec=grid_spec,
        out_shape=jax.ShapeDtypeStruct((TOP_K * t + tm, D_MODEL), F32),
        compiler_params=_params("arbitrary", "arbitrary"),
        name="moe",
    )(tile_e, n_used, tok0, tok_next, dst_prev, hn, w_gate, w_up, w_down)


def _final_kernel(h_ref, ya_ref, yb_ref, wt_ref, g_ref, o_ref):
    w = wt_ref[...]
    moe = ya_ref[...] * w[:, 0:1] + yb_ref[...] * w[:, 1:2]
    o_ref[...] = _rms(h_ref[...] + moe, g_ref[...])


def _final(h3, y2, top_w, g):
    t = h3.shape[0]
    tm = ROW_TM
    second = t // tm
    row = pl.BlockSpec((tm, D_MODEL), lambda i: (i, 0))
    return pl.pallas_call(
        _final_kernel,
        grid=(t // tm,),
        in_specs=[row, row, pl.BlockSpec((tm, D_MODEL), lambda i: (i + second, 0)),
                  pl.BlockSpec((tm, TOP_K), lambda i: (i, 0)), pl.BlockSpec((1, D_MODEL), lambda i: (0, 0))],
        out_specs=row,
        out_shape=jax.ShapeDtypeStruct((t, D_MODEL), F32),
        compiler_params=_params("arbitrary"),
        name="final_norm",
    )(h3, y2, y2, top_w, g)


def _rotary_tables(seq):
    half = RET_QK_DIM // 2
    inv = ROPE_BASE ** (-np.arange(half, dtype=np.float64) / half)
    ang = np.arange(seq, dtype=np.float64)[:, None] * inv[None, :]
    return jnp.asarray(np.cos(ang), F32), jnp.asarray(np.sin(ang), F32)


def _channel_dft_tables(n, scale):
    k = np.arange(n, dtype=np.int64)
    ang = ((k[:, None] * k[None, :]) % n) * (2.0 * math.pi / n)
    return (jnp.asarray(np.cos(ang) * scale, F32).astype(BF16),
            jnp.asarray(-np.sin(ang) * scale, F32).astype(BF16))


def _sequence_dft_tables(seq, scale):
    blk = FNET_SEQ_BLOCK
    n1 = seq // blk
    k1 = np.arange(n1, dtype=np.int64)[:, None, None]
    k2 = np.arange(blk, dtype=np.int64)[None, :, None]
    s2 = np.arange(blk, dtype=np.int64)[None, None, :]
    ang = (((k1 + n1 * k2) * s2) % seq) * (2.0 * math.pi / seq)
    gc = jnp.asarray((np.cos(ang) * scale).reshape(n1 * blk, blk), F32).astype(BF16)
    gs = jnp.asarray((np.sin(ang) * scale).reshape(n1 * blk, blk), F32).astype(BF16)
    return gc, gs


def kernel(x, mix_norm, ffn_norm, ret_w_in, ret_decay_logit, ret_gn_gain, ret_w_out,
           dense_w_gate, dense_w_up, dense_w_down, fnet_w_out, moe_router,
           moe_w_gate, moe_w_up, moe_w_down, final_norm):
    batch, seq, d = x.shape
    t = batch * seq
    n1 = seq // FNET_SEQ_BLOCK
    assert d == D_MODEL and seq % RET_CHUNK == 0 and seq % PROJ_TM == 0
    assert n1 * FNET_SEQ_BLOCK == seq and n1 & (n1 - 1) == 0
    x2d = x.reshape(t, d)

    cos, sin = _rotary_tables(seq)
    n_exp, _, d_ffe = moe_w_gate[0].shape
    proj, (moe_wd,) = _proj(x2d, mix_norm[0:1], cos, sin, ret_w_in[0].astype(BF16), seq,
                            [moe_w_down[0].reshape(n_exp * d_ffe, d)])
    log_gamma = jax.nn.log_sigmoid(ret_decay_logit[0].astype(F32))
    z, (w_out, d_wg, d_wu, d_wd) = _retention(
        proj, log_gamma, ret_gn_gain[0:1], batch, seq,
        [ret_w_out[0], dense_w_gate[0], dense_w_up[0], dense_w_down[0]])
    cc, sc = _channel_dft_tables(FNET_GROUP_DIM, FNET_GROUP_DIM ** -0.5)
    gc, gs = _sequence_dft_tables(seq, seq ** -0.5)
    (h2, zr, zi), (moe_wu, w_fnet) = _ffn0(
        x2d, z, w_out, ffn_norm[0:1], d_wg, d_wu, d_wd, mix_norm[1:2], cc, sc,
        [moe_w_up[0].reshape(n_exp * d, d_ffe), fnet_w_out[0]])

    y, (moe_wg,) = _fnet_sequence(zr, zi, gc, gs, batch, seq, [moe_w_gate[0].reshape(n_exp * d, d_ffe)])
    router_t = moe_router[0].T
    r_hi = router_t.astype(BF16)
    r_lo = (router_t - r_hi.astype(F32)).astype(BF16)
    h3, hn3, top_idx, top_w = _router(h2, y, w_fnet, ffn_norm[1:2], r_hi, r_lo)

    tile_e, n_used, tok0, tok_next, dst_prev = _routing_tables(top_idx, MOE_TM)
    y2 = _moe(hn3, tile_e, n_used, tok0, tok_next, dst_prev,
              moe_wg.reshape(n_exp, d, d_ffe), moe_wu.reshape(n_exp, d, d_ffe),
              moe_wd.reshape(n_exp, d_ffe, d))
    out = _final(h3, y2, top_w.T, final_norm.reshape(1, d))
    return out.reshape(batch, seq, d)
```

```python
import functools
import math

import jax
import jax.numpy as jnp
import numpy as np
from jax import lax
from jax.experimental import pallas as pl
from jax.experimental.pallas import tpu as pltpu

F32 = jnp.float32
BF16 = jnp.bfloat16

D_MODEL = 1024
RET_HEADS = 4
RET_QK_DIM = D_MODEL // RET_HEADS
RET_V_DIM = 2 * RET_QK_DIM
RET_QK_WIDTH = RET_HEADS * RET_QK_DIM
RET_V_WIDTH = RET_HEADS * RET_V_DIM
RET_IN_WIDTH = 2 * RET_QK_WIDTH + 2 * RET_V_WIDTH
ROPE_BASE = 10000.0
FNET_GROUPS = 4
FNET_GROUP_DIM = D_MODEL // FNET_GROUPS
N_EXPERTS = 8
TOP_K = 2
EPS = 1e-6

RET_CHUNK = 256
VMEM_LIMIT_BYTES = 56 * 1024 * 1024

PROJ_TM, PROJ_TN = 1024, 2048
FFN0_TM, FFN0_SUB = 512, 512
ROW_TM = 1024
FNET_SEQ_BLOCK = 256
FNET_DT = 256
FNET_ROW_CHUNK = 16
MOE_TM, MOE_FC, MOE_SUB = 1024, 1792, 512


def _params(*semantics):
    return pltpu.CompilerParams(dimension_semantics=semantics, vmem_limit_bytes=VMEM_LIMIT_BYTES)


def _dot(a, b):
    return jnp.dot(a, b, preferred_element_type=F32)


def _lane_mean(x):
    lanes = 128
    n = x.shape[-1]
    acc = x[:, 0:lanes]
    for lo in range(lanes, n, lanes):
        acc = acc + x[:, lo:lo + lanes]
    return jnp.sum(acc, axis=-1, keepdims=True) * (1.0 / n)


def _rms(x, g):
    return x * lax.rsqrt(_lane_mean(x * x) + EPS) * g


def _silu(a):
    return a * (1.0 / (1.0 + jnp.exp(-a)))


BF16_ROW_TILE = 16


def _cast_specs(arrays, n_steps, step_of):
    specs, shapes = [], []
    for a in arrays:
        rows, cols = a.shape
        n_blocks = n_steps
        while rows % (n_blocks * BF16_ROW_TILE):
            n_blocks //= 2
        assert n_blocks >= 1

        def index(*ids, n_blocks=n_blocks):
            return (jnp.minimum(step_of(*ids), n_blocks - 1), 0)

        specs.append(pl.BlockSpec((rows // n_blocks, cols), index))
        shapes.append(jax.ShapeDtypeStruct(a.shape, BF16))
    return specs, shapes


def _with_casts(body, n_in, n_out, n_cast):
    def kernel(*refs):
        ins, refs = refs[:n_in], refs[n_in:]
        srcs, refs = refs[:n_cast], refs[n_cast:]
        outs, refs = refs[:n_out], refs[n_out:]
        dsts, scratch = refs[:n_cast], refs[n_cast:]
        for src, dst in zip(srcs, dsts):
            dst[...] = src[...].astype(BF16)
        body(*ins, *outs, *scratch)
    return kernel


def _proj_kernel(x_ref, g_ref, cos_ref, sin_ref, w_ref, o_ref, hn_ref):
    j = pl.program_id(1)

    @pl.when(j == 0)
    def _():
        hn_ref[...] = _rms(x_ref[...], g_ref[...]).astype(BF16)

    tn = o_ref.shape[1]
    half = RET_QK_DIM // 2
    k_start, v_start, g_start = RET_QK_WIDTH, 2 * RET_QK_WIDTH, 2 * RET_QK_WIDTH + RET_V_WIDTH

    def column_tile(jj):
        hn = hn_ref[...]
        for h in range(tn // RET_QK_DIM):
            lo = h * RET_QK_DIM
            col = jj * tn + lo
            y = _dot(hn, w_ref[:, col:col + RET_QK_DIM])
            if col < v_start:
                scale = RET_QK_DIM ** -0.5 if col >= k_start else 1.0
                cos = cos_ref[...] * scale
                sin = sin_ref[...] * scale
                t1 = y[:, :half]
                t2 = y[:, half:]
                o_ref[:, lo:lo + half] = (t1 * cos - t2 * sin).astype(BF16)
                o_ref[:, lo + half:lo + RET_QK_DIM] = (t1 * sin + t2 * cos).astype(BF16)
            elif col < g_start:
                o_ref[:, lo:lo + RET_QK_DIM] = y.astype(BF16)
            else:
                o_ref[:, lo:lo + RET_QK_DIM] = _silu(y).astype(BF16)

    for jj in range(w_ref.shape[1] // tn):
        pl.when(j == jj)(functools.partial(column_tile, jj))


def _proj(x2d, g, cos, sin, w_in, seq, casts):
    t = x2d.shape[0]
    tm, tn = PROJ_TM, PROJ_TN
    pos_tiles = seq // tm
    cast_specs, cast_shapes = _cast_specs(casts, t // tm, lambda i, j: i)
    outs = pl.pallas_call(
        _with_casts(_proj_kernel, 5, 1, len(casts)),
        grid=(t // tm, RET_IN_WIDTH // tn),
        in_specs=[
            pl.BlockSpec((tm, D_MODEL), lambda i, j: (i, 0)),
            pl.BlockSpec((1, D_MODEL), lambda i, j: (0, 0)),
            pl.BlockSpec((tm, RET_QK_DIM // 2), lambda i, j: (i % pos_tiles, 0)),
            pl.BlockSpec((tm, RET_QK_DIM // 2), lambda i, j: (i % pos_tiles, 0)),
            pl.BlockSpec((D_MODEL, RET_IN_WIDTH), lambda i, j: (0, 0), pipeline_mode=pl.Buffered(1)),
        ] + cast_specs,
        out_specs=[pl.BlockSpec((tm, tn), lambda i, j: (i, j))] + cast_specs,
        out_shape=[jax.ShapeDtypeStruct((t, RET_IN_WIDTH), BF16)] + cast_shapes,
        scratch_shapes=[pltpu.VMEM((tm, D_MODEL), BF16)],
        compiler_params=_params("arbitrary", "arbitrary"),
        name="proj",
    )(x2d, g, cos, sin, w_in, *casts)
    return outs[0], outs[1:]


def _ret_kernel(lg_ref, q_ref, k_ref, v_ref, g_ref, gain_ref, o_ref, sfa_ref, sba_ref, sf_ref, sb_ref):
    c = RET_CHUNK
    n_chunks = q_ref.shape[0] // c
    h = pl.program_id(1)
    lgf = lg_ref[0, h]
    lgb = lg_ref[1, h]

    row = lax.broadcasted_iota(jnp.int32, (c, RET_QK_DIM), 0).astype(F32)
    xi_f = jnp.exp((row + 1.0) * lgf).astype(BF16)
    zeta_f = jnp.exp((c - 1.0 - row) * lgf).astype(BF16)
    xi_b = jnp.exp((c - row) * lgb).astype(BF16)
    zeta_b = jnp.exp(row * lgb).astype(BF16)
    g_f = jnp.exp(jnp.full((1, RET_V_DIM), c, F32) * lgf)
    g_b = jnp.exp(jnp.full((1, RET_V_DIM), c, F32) * lgb)
    di = lax.broadcasted_iota(jnp.int32, (c, c), 0)
    dj = lax.broadcasted_iota(jnp.int32, (c, c), 1)
    diff = (di - dj).astype(F32)
    decay = jnp.exp(jnp.where(diff >= 0, diff * lgf, -diff * lgb))

    def chunk(n):
        r0 = pl.multiple_of(n * c, c)
        return pl.ds(r0, c)

    def scaled(t, s):
        return t * s

    def outer(kz, vn):
        return lax.dot_general(kz, vn, (((0,), (0,)), ((), ())), preferred_element_type=F32)

    sf_ref[...] = jnp.zeros_like(sf_ref)
    sb_ref[...] = jnp.zeros_like(sb_ref)

    def states(step, carry):
        nf = step
        nb = n_chunks - 1 - step
        sf = sf_ref[...]
        sb = sb_ref[...]
        sfa_ref[nf] = sf.astype(BF16)
        sba_ref[nb] = sb.astype(BF16)
        rf, rb = chunk(nf), chunk(nb)
        sf_ref[...] = sf * g_f + outer(scaled(k_ref[rf, :], zeta_f), v_ref[rf, :])
        sb_ref[...] = sb * g_b + outer(scaled(k_ref[rb, :], zeta_b), v_ref[rb, :])
        return carry

    lax.fori_loop(0, n_chunks, states, 0, unroll=4)

    gain = gain_ref[...]

    def fwd(n, carry):
        rows = chunk(n)
        qn, kn, vn = q_ref[rows, :], k_ref[rows, :], v_ref[rows, :]
        scores = lax.dot_general(qn, kn, (((1,), (1,)), ((), ())), preferred_element_type=F32) * decay
        y = (_dot(scores.astype(BF16), vn) + _dot(scaled(qn, xi_f), sfa_ref[n])
             + _dot(scaled(qn, xi_b), sba_ref[n]))
        mu = _lane_mean(y)
        yc = y - mu
        var = _lane_mean(yc * yc)
        yn = yc * lax.rsqrt(var + EPS) * gain
        o_ref[rows, :] = g_ref[rows, :] * yn.astype(BF16)
        return carry

    lax.fori_loop(0, n_chunks, fwd, 0, unroll=8)


def _retention(proj, log_gamma, gain, batch, seq, casts):
    t = proj.shape[0]
    qk_blocks = RET_QK_WIDTH // RET_QK_DIM
    k_off = qk_blocks
    v_off = 2 * RET_QK_WIDTH // RET_V_DIM
    g_off = v_off + RET_V_WIDTH // RET_V_DIM
    cast_specs, cast_shapes = _cast_specs(casts, batch * RET_HEADS, lambda b, h: b * RET_HEADS + h)
    outs = pl.pallas_call(
        _with_casts(_ret_kernel, 6, 1, len(casts)),
        grid=(batch, RET_HEADS),
        in_specs=[
            pl.BlockSpec(memory_space=pltpu.SMEM),
            pl.BlockSpec((seq, RET_QK_DIM), lambda b, h: (b, h)),
            pl.BlockSpec((seq, RET_QK_DIM), lambda b, h: (b, k_off + h)),
            pl.BlockSpec((seq, RET_V_DIM), lambda b, h: (b, v_off + h)),
            pl.BlockSpec((seq, RET_V_DIM), lambda b, h: (b, g_off + h)),
            pl.BlockSpec((1, RET_V_DIM), lambda b, h: (0, h)),
        ] + cast_specs,
        out_specs=[pl.BlockSpec((seq, RET_V_DIM), lambda b, h: (b, h))] + cast_specs,
        out_shape=[jax.ShapeDtypeStruct((t, RET_V_WIDTH), BF16)] + cast_shapes,
        scratch_shapes=[
            pltpu.VMEM((seq // RET_CHUNK, RET_QK_DIM, RET_V_DIM), BF16),
            pltpu.VMEM((seq // RET_CHUNK, RET_QK_DIM, RET_V_DIM), BF16),
            pltpu.VMEM((RET_QK_DIM, RET_V_DIM), F32),
            pltpu.VMEM((RET_QK_DIM, RET_V_DIM), F32),
        ],
        compiler_params=_params("arbitrary", "arbitrary"),
        name="retention",
    )(log_gamma, proj, proj, proj, proj, gain, *casts)
    return outs[0], outs[1:]


def _ffn0_kernel(x_ref, z_ref, wo_ref, g_ref, wg_ref, wu_ref, wd_ref, g2_ref, cc_ref, sc_ref,
                 o_ref, zr_ref, zi_ref):
    h1 = x_ref[...] + _dot(z_ref[...], wo_ref[...])
    hn = _rms(h1, g_ref[...]).astype(BF16)
    d_ff = wg_ref.shape[1]
    part = None
    for lo in range(0, d_ff, FFN0_SUB):
        cols = slice(lo, min(lo + FFN0_SUB, d_ff))
        hidden = (_silu(_dot(hn, wg_ref[:, cols])) * _dot(hn, wu_ref[:, cols])).astype(BF16)
        d = _dot(hidden, wd_ref[cols, :])
        part = d if part is None else part + d
    h2 = h1 + part
    o_ref[...] = h2
    hn2 = _rms(h2, g2_ref[...]).astype(BF16)
    for grp in range(FNET_GROUPS):
        cols = slice(grp * FNET_GROUP_DIM, (grp + 1) * FNET_GROUP_DIM)
        zr_ref[:, cols] = _dot(hn2[:, cols], cc_ref[...]).astype(BF16)
        zi_ref[:, cols] = _dot(hn2[:, cols], sc_ref[...]).astype(BF16)


def _ffn0(x2d, z, w_out, g, w_gate, w_up, w_down, g2, cc, sc, casts):
    t = x2d.shape[0]
    d_ff = w_gate.shape[1]
    tm = FFN0_TM
    once = pl.Buffered(1)
    row = pl.BlockSpec((tm, D_MODEL), lambda i: (i, 0))
    cast_specs, cast_shapes = _cast_specs(casts, t // tm, lambda i: i)
    outs = pl.pallas_call(
        _with_casts(_ffn0_kernel, 10, 3, len(casts)),
        grid=(t // tm,),
        in_specs=[
            row,
            pl.BlockSpec((tm, RET_V_WIDTH), lambda i: (i, 0)),
            pl.BlockSpec((RET_V_WIDTH, D_MODEL), lambda i: (0, 0), pipeline_mode=once),
            pl.BlockSpec((1, D_MODEL), lambda i: (0, 0), pipeline_mode=once),
            pl.BlockSpec((D_MODEL, d_ff), lambda i: (0, 0), pipeline_mode=once),
            pl.BlockSpec((D_MODEL, d_ff), lambda i: (0, 0), pipeline_mode=once),
            pl.BlockSpec((d_ff, D_MODEL), lambda i: (0, 0), pipeline_mode=once),
            pl.BlockSpec((1, D_MODEL), lambda i: (0, 0), pipeline_mode=once),
            pl.BlockSpec((FNET_GROUP_DIM, FNET_GROUP_DIM), lambda i: (0, 0), pipeline_mode=once),
            pl.BlockSpec((FNET_GROUP_DIM, FNET_GROUP_DIM), lambda i: (0, 0), pipeline_mode=once),
        ] + cast_specs,
        out_specs=[row, row, row] + cast_specs,
        out_shape=[jax.ShapeDtypeStruct((t, D_MODEL), F32), jax.ShapeDtypeStruct((t, D_MODEL), BF16),
                   jax.ShapeDtypeStruct((t, D_MODEL), BF16)] + cast_shapes,
        compiler_params=_params("arbitrary"),
        name="ffn0",
    )(x2d, z, w_out, g, w_gate, w_up, w_down, g2, cc, sc, *casts)
    return outs[:3], outs[3:]


def _dft_across_blocks(xs):
    n = len(xs)
    if n == 1:
        return xs
    even = _dft_across_blocks(xs[0::2])
    odd = _dft_across_blocks(xs[1::2])
    out = [None] * n
    for k in range(n // 2):
        er, ei = even[k]
        o_r, o_i = odd[k]
        if k == 0:
            tr, ti = o_r, o_i
            out[k], out[k + n // 2] = (er + tr, ei + ti), (er - tr, ei - ti)
        elif 4 * k == n:
            out[k], out[k + n // 2] = (er + o_i, ei - o_r), (er - o_i, ei + o_r)
        else:
            c, s = math.cos(2.0 * math.pi * k / n), math.sin(2.0 * math.pi * k / n)
            tr = o_r * c + o_i * s
            ti = o_i * c - o_r * s
            out[k], out[k + n // 2] = (er + tr, ei + ti), (er - tr, ei - ti)
    return out


def _fnet_seq_kernel(zr_ref, zi_ref, gc_ref, gs_ref, o_ref, ar_ref, ai_ref, ys_ref):
    blk = FNET_SEQ_BLOCK
    n1 = zr_ref.shape[0] // blk
    rc = FNET_ROW_CHUNK
    lanes = 128

    def body(r, carry):
        r0 = pl.multiple_of(r * rc, rc)
        for l in range(zr_ref.shape[1] // lanes):
            cols = slice(l * lanes, (l + 1) * lanes)
            zs = [(zr_ref[pl.ds(s1 * blk + r0, rc), cols].astype(F32),
                   zi_ref[pl.ds(s1 * blk + r0, rc), cols].astype(F32)) for s1 in range(n1)]
            for k1, (a_r, a_i) in enumerate(_dft_across_blocks(zs)):
                ar_ref[pl.ds(k1 * blk + r0, rc), cols] = a_r.astype(BF16)
                ai_ref[pl.ds(k1 * blk + r0, rc), cols] = a_i.astype(BF16)
        return carry

    lax.fori_loop(0, blk // rc, body, 0, unroll=2)

    for k1 in range(n1):
        rows = slice(k1 * blk, (k1 + 1) * blk)
        y = _dot(gc_ref[rows, :], ar_ref[rows, :]) + _dot(gs_ref[rows, :], ai_ref[rows, :])
        for l in range(y.shape[1] // lanes):
            ys_ref[l, pl.ds(k1, blk, stride=n1), :] = y[:, l * lanes:(l + 1) * lanes]
    for l in range(o_ref.shape[1] // lanes):
        o_ref[:, l * lanes:(l + 1) * lanes] = ys_ref[l]


def _fnet_sequence(zr, zi, gc, gs, batch, seq, casts):
    t = zr.shape[0]
    dt = FNET_DT
    blkspec = pl.BlockSpec((seq, dt), lambda b, c: (b, c))
    tab = pl.BlockSpec((seq, FNET_SEQ_BLOCK), lambda b, c: (0, 0))
    col_tiles = D_MODEL // dt
    cast_specs, cast_shapes = _cast_specs(casts, batch * col_tiles, lambda b, c: b * col_tiles + c)
    outs = pl.pallas_call(
        _with_casts(_fnet_seq_kernel, 4, 1, len(casts)),
        grid=(batch, col_tiles),
        in_specs=[blkspec, blkspec, tab, tab] + cast_specs,
        out_specs=[blkspec] + cast_specs,
        out_shape=[jax.ShapeDtypeStruct((t, D_MODEL), F32)] + cast_shapes,
        scratch_shapes=[pltpu.VMEM((seq, dt), BF16), pltpu.VMEM((seq, dt), BF16),
                        pltpu.VMEM((dt // 128, seq, 128), F32)],
        compiler_params=_params("arbitrary", "arbitrary"),
        name="fnet_sequence",
    )(zr, zi, gc, gs, *casts)
    return outs[0], outs[1:]


def _router_kernel(h_ref, y_ref, w_ref, g_ref, rhi_ref, rlo_ref, h3_ref, hn_ref, idx_ref, wt_ref):
    h3 = h_ref[...] + _dot(y_ref[...].astype(BF16), w_ref[...])
    h3_ref[...] = h3
    hn = _rms(h3, g_ref[...])
    hn_ref[...] = hn
    hi = hn.astype(BF16)
    lo = (hn - hi.astype(F32)).astype(BF16)

    def nt(a, b):
        return lax.dot_general(a, b, (((1,), (1,)), ((), ())), preferred_element_type=F32)

    logits = nt(rhi_ref[...], hi) + (nt(rhi_ref[...], lo) + nt(rlo_ref[...], hi))
    row = lax.broadcasted_iota(jnp.int32, logits.shape, 0)
    m1 = jnp.max(logits, axis=0, keepdims=True)
    i1 = jnp.min(jnp.where(logits == m1, row, N_EXPERTS), axis=0, keepdims=True)
    rest = jnp.where(row == i1, -jnp.inf, logits)
    m2 = jnp.max(rest, axis=0, keepdims=True)
    i2 = jnp.min(jnp.where(rest == m2, row, N_EXPERTS), axis=0, keepdims=True)
    e2 = jnp.exp(m2 - m1)
    inv = 1.0 / (1.0 + e2)
    idx_ref[0:1, :] = i1
    idx_ref[1:2, :] = i2
    wt_ref[0:1, :] = inv
    wt_ref[1:2, :] = e2 * inv


def _router(h2, y, w_fnet, g, r_hi, r_lo):
    t = h2.shape[0]
    tm = ROW_TM
    row = pl.BlockSpec((tm, D_MODEL), lambda i: (i, 0))
    pair = pl.BlockSpec((TOP_K, tm), lambda i: (0, i))
    rtr = pl.BlockSpec((N_EXPERTS, D_MODEL), lambda i: (0, 0))
    return pl.pallas_call(
        _router_kernel,
        grid=(t // tm,),
        in_specs=[row, row, pl.BlockSpec((D_MODEL, D_MODEL), lambda i: (0, 0)),
                  pl.BlockSpec((1, D_MODEL), lambda i: (0, 0)), rtr, rtr],
        out_specs=[row, row, pair, pair],
        out_shape=[jax.ShapeDtypeStruct((t, D_MODEL), F32), jax.ShapeDtypeStruct((t, D_MODEL), F32),
                   jax.ShapeDtypeStruct((TOP_K, t), jnp.int32), jax.ShapeDtypeStruct((TOP_K, t), F32)],
        compiler_params=_params("arbitrary"),
        name="router",
    )(h2, y, w_fnet, g, r_hi, r_lo)


def _routing_tables(top_idx, tm):
    t = top_idx.shape[1]
    n_pairs = TOP_K * t
    n_tiles = n_pairs // tm + N_EXPERTS
    e_flat = top_idx.reshape(n_pairs)
    order = jnp.argsort(e_flat, stable=True).astype(jnp.int32)
    experts = jnp.arange(N_EXPERTS, dtype=jnp.int32)
    counts = jnp.sum((e_flat[:, None] == experts[None, :]).astype(jnp.int32), axis=0)
    off = jnp.cumsum(counts) - counts
    tiles_e = (counts + tm - 1) // tm
    tile_end = jnp.cumsum(tiles_e)
    tile_off = tile_end - tiles_e
    n_used = tile_end[-1]
    last_e = jnp.max(jnp.where(tiles_e > 0, experts, 0))
    tile_ids = jnp.arange(n_tiles, dtype=jnp.int32)
    tile_e = jnp.minimum(jnp.sum((tile_ids[:, None] >= tile_end[None, :]).astype(jnp.int32), axis=1), last_e)
    u = jnp.arange(tm, dtype=jnp.int32)
    rank = (tile_ids - tile_off[tile_e])[:, None] * tm + u[None, :]
    valid = (tile_ids < n_used)[:, None] & (rank < counts[tile_e][:, None])
    src = order[jnp.clip(off[tile_e][:, None] + rank, 0, n_pairs - 1)]
    tok = jnp.where(valid, src % t, 0).astype(jnp.int32)
    spare = n_pairs + jnp.broadcast_to(u[None, :], (1, tm))
    dst = jnp.where(valid, src, spare).astype(jnp.int32)
    tok_next = jnp.concatenate([tok[1:], jnp.zeros((2, tm), jnp.int32)], axis=0)
    dst_prev = jnp.concatenate([spare.astype(jnp.int32), dst], axis=0)
    tile_e = jnp.concatenate([tile_e, tile_e[-1:]]).astype(jnp.int32)
    return (tile_e, n_used.reshape(1).astype(jnp.int32), tok[0].reshape(1, 1, tm),
            tok_next.reshape(n_tiles + 1, 1, tm), dst_prev.reshape(n_tiles + 1, 1, tm))


def _moe_kernel(te_ref, nu_ref, tok0_ref, tokn_ref, dstp_ref, hn_hbm, wg_ref, wu_ref, wd_ref,
                y2_hbm, xbuf, xb, acc, ybuf, gsem, ssem, *, n_ff):
    del te_ref
    i = pl.program_id(0)
    j = pl.program_id(1)
    n_used = nu_ref[0]
    tm = xbuf.shape[0]
    fc = wg_ref.shape[2]
    rows_per_step = tm // n_ff
    spare0 = y2_hbm.shape[0] - tm
    unroll = 8

    def gather_copy(tok, u):
        return pltpu.make_async_copy(hn_hbm.at[pl.ds(tok, 1), :], xbuf.at[pl.ds(u, 1), :], gsem)

    def scatter_copy(u, row):
        return pltpu.make_async_copy(ybuf.at[pl.ds(u, 1), :], y2_hbm.at[pl.ds(row, 1), :], ssem)

    def start_gather(ref):
        def body(u, carry):
            gather_copy(ref[0, 0, u], u).start()
            return carry
        lax.fori_loop(0, tm, body, 0, unroll=unroll)

    def wait_gather():
        pltpu.make_async_copy(hn_hbm.at[pl.ds(0, tm), :], xbuf, gsem).wait()

    def start_scatter_all():
        def body(u, carry):
            scatter_copy(u, dstp_ref[0, 0, u]).start()
            return carry
        lax.fori_loop(0, tm, body, 0, unroll=unroll)

    def wait_scatter():
        pltpu.make_async_copy(ybuf, y2_hbm.at[pl.ds(0, tm), :], ssem).wait()

    @pl.when((i == 0) & (j == 0))
    def _():
        ybuf[...] = jnp.zeros_like(ybuf)
        fill = pltpu.make_async_copy(ybuf, y2_hbm.at[pl.ds(spare0, tm), :], ssem)
        fill.start()
        fill.wait()
        start_gather(tok0_ref)

    def issue_gathers(jj):
        for u in range(jj * rows_per_step, (jj + 1) * rows_per_step):
            gather_copy(tokn_ref[0, 0, u], u).start()

    def ff_step(jj):
        if jj == 0:
            wait_gather()
            xb[...] = xbuf[...].astype(BF16)
            for u in range(tm):
                scatter_copy(u, dstp_ref[0, 0, u]).start(priority=1)
            issue_gathers(0)
        if jj == n_ff - 1:
            wait_scatter()

        x = xb[...]
        part = None
        for lo in range(0, fc, MOE_SUB):
            cols = slice(lo, min(lo + MOE_SUB, fc))
            hidden = (_silu(_dot(x, wg_ref[0, :, cols])) * _dot(x, wu_ref[0, :, cols])).astype(BF16)
            d = _dot(hidden, wd_ref[0, cols, :])
            part = d if part is None else part + d

        if jj == 0:
            acc[...] = part
        elif jj < n_ff - 1:
            acc[...] += part
        else:
            ybuf[...] = acc[...] + part

    for jj in range(n_ff):
        if jj > 0:
            pl.when((i < n_used) & (j == jj))(functools.partial(issue_gathers, jj))
        pl.when((i < n_used) & (j == jj))(functools.partial(ff_step, jj))

    @pl.when((i == n_used) & (j == 0))
    def _():
        start_scatter_all()
        wait_scatter()
        wait_gather()


def _moe(hn, tile_e, n_used, tok0, tok_next, dst_prev, w_gate, w_up, w_down):
    t = hn.shape[0]
    d_ff = w_gate.shape[2]
    tm, fc = MOE_TM, MOE_FC
    n_steps = tok_next.shape[0]
    n_ff = d_ff // fc
    assert n_ff >= 2 and n_ff * fc == d_ff and tm % n_ff == 0

    def ff_step(i, j, nu):
        return jnp.where(i < nu[0], j, n_ff - 1)

    grid_spec = pltpu.PrefetchScalarGridSpec(
        num_scalar_prefetch=2,
        grid=(n_steps, n_ff),
        in_specs=[
            pl.BlockSpec((1, 1, tm), lambda i, j, te, nu: (0, 0, 0), memory_space=pltpu.SMEM),
            pl.BlockSpec((1, 1, tm), lambda i, j, te, nu: (i, 0, 0), memory_space=pltpu.SMEM),
            pl.BlockSpec((1, 1, tm), lambda i, j, te, nu: (i, 0, 0), memory_space=pltpu.SMEM),
            pl.BlockSpec(memory_space=pl.ANY),
            pl.BlockSpec((1, D_MODEL, fc), lambda i, j, te, nu: (te[i], 0, ff_step(i, j, nu))),
            pl.BlockSpec((1, D_MODEL, fc), lambda i, j, te, nu: (te[i], 0, ff_step(i, j, nu))),
            pl.BlockSpec((1, fc, D_MODEL), lambda i, j, te, nu: (te[i], ff_step(i, j, nu), 0)),
        ],
        out_specs=pl.BlockSpec(memory_space=pl.ANY),
        scratch_shapes=[
            pltpu.VMEM((tm, D_MODEL), F32),
            pltpu.VMEM((tm, D_MODEL), BF16),
            pltpu.VMEM((tm, D_MODEL), F32),
            pltpu.VMEM((tm, D_MODEL), F32),
            pltpu.SemaphoreType.DMA(()),
            pltpu.SemaphoreType.DMA(()),
        ],
    )
    return pl.pallas_call(
        functools.partial(_moe_kernel, n_ff=n_ff),
        grid_spec=grid_spec,
        out_shape=jax.ShapeDtypeStruct((TOP_K * t + tm, D_MODEL), F32),
        compiler_params=_params("arbitrary", "arbitrary"),
        name="moe",
    )(tile_e, n_used, tok0, tok_next, dst_prev, hn, w_gate, w_up, w_down)


def _final_kernel(h_ref, ya_ref, yb_ref, wt_ref, g_ref, o_ref):
    w = wt_ref[...]
    moe = ya_ref[...] * w[:, 0:1] + yb_ref[...] * w[:, 1:2]
    o_ref[...] = _rms(h_ref[...] + moe, g_ref[...])


def _final(h3, y2, top_w, g):
    t = h3.shape[0]
    tm = ROW_TM
    second = t // tm
    row = pl.BlockSpec((tm, D_MODEL), lambda i: (i, 0))
    return pl.pallas_call(
        _final_kernel,
        grid=(t // tm,),
        in_specs=[row, row, pl.BlockSpec((tm, D_MODEL), lambda i: (i + second, 0)),
                  pl.BlockSpec((tm, TOP_K), lambda i: (i, 0)), pl.BlockSpec((1, D_MODEL), lambda i: (0, 0))],
        out_specs=row,
        out_shape=jax.ShapeDtypeStruct((t, D_MODEL), F32),
        compiler_params=_params("arbitrary"),
        name="final_norm",
    )(h3, y2, y2, top_w, g)


def _rotary_tables(seq):
    half = RET_QK_DIM // 2
    inv = ROPE_BASE ** (-np.arange(half, dtype=np.float64) / half)
    ang = np.arange(seq, dtype=np.float64)[:, None] * inv[None, :]
    return jnp.asarray(np.cos(ang), F32), jnp.asarray(np.sin(ang), F32)


def _channel_dft_tables(n, scale):
    k = np.arange(n, dtype=np.int64)
    ang = ((k[:, None] * k[None, :]) % n) * (2.0 * math.pi / n)
    return (jnp.asarray(np.cos(ang) * scale, F32).astype(BF16),
            jnp.asarray(-np.sin(ang) * scale, F32).astype(BF16))


def _sequence_dft_tables(seq, scale):
    blk = FNET_SEQ_BLOCK
    n1 = seq // blk
    k1 = np.arange(n1, dtype=np.int64)[:, None, None]
    k2 = np.arange(blk, dtype=np.int64)[None, :, None]
    s2 = np.arange(blk, dtype=np.int64)[None, None, :]
    ang = (((k1 + n1 * k2) * s2) % seq) * (2.0 * math.pi / seq)
    gc = jnp.asarray((np.cos(ang) * scale).reshape(n1 * blk, blk), F32).astype(BF16)
    gs = jnp.asarray((np.sin(ang) * scale).reshape(n1 * blk, blk), F32).astype(BF16)
    return gc, gs


def kernel(x, mix_norm, ffn_norm, ret_w_in, ret_decay_logit, ret_gn_gain, ret_w_out,
           dense_w_gate, dense_w_up, dense_w_down, fnet_w_out, moe_router,
           moe_w_gate, moe_w_up, moe_w_down, final_norm):
    batch, seq, d = x.shape
    t = batch * seq
    n1 = seq // FNET_SEQ_BLOCK
    assert d == D_MODEL and seq % RET_CHUNK == 0 and seq % PROJ_TM == 0
    assert n1 * FNET_SEQ_BLOCK == seq and n1 & (n1 - 1) == 0
    x2d = x.reshape(t, d)

    cos, sin = _rotary_tables(seq)
    n_exp, _, d_ffe = moe_w_gate[0].shape
    proj, (moe_wd,) = _proj(x2d, mix_norm[0:1], cos, sin, ret_w_in[0].astype(BF16), seq,
                            [moe_w_down[0].reshape(n_exp * d_ffe, d)])
    log_gamma = jax.nn.log_sigmoid(ret_decay_logit[0].astype(F32))
    z, (w_out, d_wg, d_wu, d_wd) = _retention(
        proj, log_gamma, ret_gn_gain[0:1], batch, seq,
        [ret_w_out[0], dense_w_gate[0], dense_w_up[0], dense_w_down[0]])
    cc, sc = _channel_dft_tables(FNET_GROUP_DIM, FNET_GROUP_DIM ** -0.5)
    gc, gs = _sequence_dft_tables(seq, seq ** -0.5)
    (h2, zr, zi), (moe_wu, w_fnet) = _ffn0(
        x2d, z, w_out, ffn_norm[0:1], d_wg, d_wu, d_wd, mix_norm[1:2], cc, sc,
        [moe_w_up[0].reshape(n_exp * d, d_ffe), fnet_w_out[0]])

    y, (moe_wg,) = _fnet_sequence(zr, zi, gc, gs, batch, seq, [moe_w_gate[0].reshape(n_exp * d, d_ffe)])
    router_t = moe_router[0].T
    r_hi = router_t.astype(BF16)
    r_lo = (router_t - r_hi.astype(F32)).astype(BF16)
    h3, hn3, top_idx, top_w = _router(h2, y, w_fnet, ffn_norm[1:2], r_hi, r_lo)

    tile_e, n_used, tok0, tok_next, dst_prev = _routing_tables(top_idx, MOE_TM)
    y2 = _moe(hn3, tile_e, n_used, tok0, tok_next, dst_prev,
              moe_wg.reshape(n_exp, d, d_ffe), moe_wu.reshape(n_exp, d, d_ffe),
              moe_wd.reshape(n_exp, d_ffe, d))
    out = _final(h3, y2, top_w.T, final_norm.reshape(1, d))
    return out.reshape(batch, seq, d)
```

```python
import functools
import math

import jax
import jax.numpy as jnp
import numpy as np
from jax import lax
from jax.experimental import pallas as pl
from jax.experimental.pallas import tpu as pltpu

F32 = jnp.float32
BF16 = jnp.bfloat16

D_MODEL = 1024
RET_HEADS = 4
RET_QK_DIM = D_MODEL // RET_HEADS
RET_V_DIM = 2 * RET_QK_DIM
RET_QK_WIDTH = RET_HEADS * RET_QK_DIM
RET_V_WIDTH = RET_HEADS * RET_V_DIM
RET_IN_WIDTH = 2 * RET_QK_WIDTH + 2 * RET_V_WIDTH
ROPE_BASE = 10000.0
FNET_GROUPS = 4
FNET_GROUP_DIM = D_MODEL // FNET_GROUPS
N_EXPERTS = 8
TOP_K = 2
EPS = 1e-6

RET_CHUNK = 256
VMEM_LIMIT_BYTES = 56 * 1024 * 1024

PROJ_TM, PROJ_TN = 1024, 2048
FFN0_TM, FFN0_SUB = 512, 512
ROW_TM = 1024
FNET_SEQ_BLOCK = 256
FNET_DT = 256
FNET_ROW_CHUNK = 16
MOE_TM, MOE_FC, MOE_SUB = 1024, 1792, 512


def _params(*semantics):
    return pltpu.CompilerParams(dimension_semantics=semantics, vmem_limit_bytes=VMEM_LIMIT_BYTES)


def _dot(a, b):
    return jnp.dot(a, b, preferred_element_type=F32)


def _lane_mean(x):
    lanes = 128
    n = x.shape[-1]
    acc = x[:, 0:lanes]
    for lo in range(lanes, n, lanes):
        acc = acc + x[:, lo:lo + lanes]
    return jnp.sum(acc, axis=-1, keepdims=True) * (1.0 / n)


def _rms(x, g):
    return x * lax.rsqrt(_lane_mean(x * x) + EPS) * g


def _silu(a):
    return a * (1.0 / (1.0 + jnp.exp(-a)))


BF16_ROW_TILE = 16


def _cast_specs(arrays, n_steps, step_of):
    specs, shapes = [], []
    for a in arrays:
        rows, cols = a.shape
        n_blocks = n_steps
        while rows % (n_blocks * BF16_ROW_TILE):
            n_blocks //= 2
        assert n_blocks >= 1

        def index(*ids, n_blocks=n_blocks):
            return (jnp.minimum(step_of(*ids), n_blocks - 1), 0)

        specs.append(pl.BlockSpec((rows // n_blocks, cols), index))
        shapes.append(jax.ShapeDtypeStruct(a.shape, BF16))
    return specs, shapes


def _with_casts(body, n_in, n_out, n_cast):
    def kernel(*refs):
        ins, refs = refs[:n_in], refs[n_in:]
        srcs, refs = refs[:n_cast], refs[n_cast:]
        outs, refs = refs[:n_out], refs[n_out:]
        dsts, scratch = refs[:n_cast], refs[n_cast:]
        for src, dst in zip(srcs, dsts):
            dst[...] = src[...].astype(BF16)
        body(*ins, *outs, *scratch)
    return kernel


def _proj_kernel(x_ref, g_ref, cos_ref, sin_ref, w_ref, o_ref, hn_ref):
    j = pl.program_id(1)

    @pl.when(j == 0)
    def _():
        hn_ref[...] = _rms(x_ref[...], g_ref[...]).astype(BF16)

    tn = o_ref.shape[1]
    half = RET_QK_DIM // 2
    k_start, v_start, g_start = RET_QK_WIDTH, 2 * RET_QK_WIDTH, 2 * RET_QK_WIDTH + RET_V_WIDTH

    def column_tile(jj):
        hn = hn_ref[...]
        for h in range(tn // RET_QK_DIM):
            lo = h * RET_QK_DIM
            col = jj * tn + lo
            y = _dot(hn, w_ref[:, col:col + RET_QK_DIM])
            if col < v_start:
                scale = RET_QK_DIM ** -0.5 if col >= k_start else 1.0
                cos = cos_ref[...] * scale
                sin = sin_ref[...] * scale
                t1 = y[:, :half]
                t2 = y[:, half:]
                o_ref[:, lo:lo + half] = (t1 * cos - t2 * sin).astype(BF16)
                o_ref[:, lo + half:lo + RET_QK_DIM] = (t1 * sin + t2 * cos).astype(BF16)
            elif col < g_start:
                o_ref[:, lo:lo + RET_QK_DIM] = y.astype(BF16)
            else:
                o_ref[:, lo:lo + RET_QK_DIM] = _silu(y).astype(BF16)

    for jj in range(w_ref.shape[1] // tn):
        pl.when(j == jj)(functools.partial(column_tile, jj))


def _proj(x2d, g, cos, sin, w_in, seq, casts):
    t = x2d.shape[0]
    tm, tn = PROJ_TM, PROJ_TN
    pos_tiles = seq // tm
    cast_specs, cast_shapes = _cast_specs(casts, t // tm, lambda i, j: i)
    outs = pl.pallas_call(
        _with_casts(_proj_kernel, 5, 1, len(casts)),
        grid=(t // tm, RET_IN_WIDTH // tn),
        in_specs=[
            pl.BlockSpec((tm, D_MODEL), lambda i, j: (i, 0)),
            pl.BlockSpec((1, D_MODEL), lambda i, j: (0, 0)),
            pl.BlockSpec((tm, RET_QK_DIM // 2), lambda i, j: (i % pos_tiles, 0)),
            pl.BlockSpec((tm, RET_QK_DIM // 2), lambda i, j: (i % pos_tiles, 0)),
            pl.BlockSpec((D_MODEL, RET_IN_WIDTH), lambda i, j: (0, 0), pipeline_mode=pl.Buffered(1)),
        ] + cast_specs,
        out_specs=[pl.BlockSpec((tm, tn), lambda i, j: (i, j))] + cast_specs,
        out_shape=[jax.ShapeDtypeStruct((t, RET_IN_WIDTH), BF16)] + cast_shapes,
        scratch_shapes=[pltpu.VMEM((tm, D_MODEL), BF16)],
        compiler_params=_params("arbitrary", "arbitrary"),
        name="proj",
    )(x2d, g, cos, sin, w_in, *casts)
    return outs[0], outs[1:]


def _ret_kernel(lg_ref, q_ref, k_ref, v_ref, g_ref, gain_ref, o_ref, sfa_ref, sba_ref, sf_ref, sb_ref):
    c = RET_CHUNK
    n_chunks = q_ref.shape[0] // c
    h = pl.program_id(1)
    lgf = lg_ref[0, h]
    lgb = lg_ref[1, h]

    row = lax.broadcasted_iota(jnp.int32, (c, RET_QK_DIM), 0).astype(F32)
    xi_f = jnp.exp((row + 1.0) * lgf).astype(BF16)
    zeta_f = jnp.exp((c - 1.0 - row) * lgf).astype(BF16)
    xi_b = jnp.exp((c - row) * lgb).astype(BF16)
    zeta_b = jnp.exp(row * lgb).astype(BF16)
    g_f = jnp.exp(jnp.full((1, RET_V_DIM), c, F32) * lgf)
    g_b = jnp.exp(jnp.full((1, RET_V_DIM), c, F32) * lgb)
    di = lax.broadcasted_iota(jnp.int32, (c, c), 0)
    dj = lax.broadcasted_iota(jnp.int32, (c, c), 1)
    diff = (di - dj).astype(F32)
    decay = jnp.exp(jnp.where(diff >= 0, diff * lgf, -diff * lgb))

    def chunk(n):
        r0 = pl.multiple_of(n * c, c)
        return pl.ds(r0, c)

    def scaled(t, s):
        return t * s

    def outer(kz, vn):
        return lax.dot_general(kz, vn, (((0,), (0,)), ((), ())), preferred_element_type=F32)

    sf_ref[...] = jnp.zeros_like(sf_ref)
    sb_ref[...] = jnp.zeros_like(sb_ref)

    def states(step, carry):
        nf = step
        nb = n_chunks - 1 - step
        sf = sf_ref[...]
        sb = sb_ref[...]
        sfa_ref[nf] = sf.astype(BF16)
        sba_ref[nb] = sb.astype(BF16)
        rf, rb = chunk(nf), chunk(nb)
        sf_ref[...] = sf * g_f + outer(scaled(k_ref[rf, :], zeta_f), v_ref[rf, :])
        sb_ref[...] = sb * g_b + outer(scaled(k_ref[rb, :], zeta_b), v_ref[rb, :])
        return carry

    lax.fori_loop(0, n_chunks, states, 0, unroll=4)

    gain = gain_ref[...]

    def fwd(n, carry):
        rows = chunk(n)
        qn, kn, vn = q_ref[rows, :], k_ref[rows, :], v_ref[rows, :]
        scores = lax.dot_general(qn, kn, (((1,), (1,)), ((), ())), preferred_element_type=F32) * decay
        y = (_dot(scores.astype(BF16), vn) + _dot(scaled(qn, xi_f), sfa_ref[n])
             + _dot(scaled(qn, xi_b), sba_ref[n]))
        mu = _lane_mean(y)
        yc = y - mu
        var = _lane_mean(yc * yc)
        yn = yc * lax.rsqrt(var + EPS) * gain
        o_ref[rows, :] = g_ref[rows, :] * yn.astype(BF16)
        return carry

    lax.fori_loop(0, n_chunks, fwd, 0, unroll=8)


def _retention(proj, log_gamma, gain, batch, seq, casts):
    t = proj.shape[0]
    qk_blocks = RET_QK_WIDTH // RET_QK_DIM
    k_off = qk_blocks
    v_off = 2 * RET_QK_WIDTH // RET_V_DIM
    g_off = v_off + RET_V_WIDTH // RET_V_DIM
    cast_specs, cast_shapes = _cast_specs(casts, batch * RET_HEADS, lambda b, h: b * RET_HEADS + h)
    outs = pl.pallas_call(
        _with_casts(_ret_kernel, 6, 1, len(casts)),
        grid=(batch, RET_HEADS),
        in_specs=[
            pl.BlockSpec(memory_space=pltpu.SMEM),
            pl.BlockSpec((seq, RET_QK_DIM), lambda b, h: (b, h)),
            pl.BlockSpec((seq, RET_QK_DIM), lambda b, h: (b, k_off + h)),
            pl.BlockSpec((seq, RET_V_DIM), lambda b, h: (b, v_off + h)),
            pl.BlockSpec((seq, RET_V_DIM), lambda b, h: (b, g_off + h)),
            pl.BlockSpec((1, RET_V_DIM), lambda b, h: (0, h)),
        ] + cast_specs,
        out_specs=[pl.BlockSpec((seq, RET_V_DIM), lambda b, h: (b, h))] + cast_specs,
        out_shape=[jax.ShapeDtypeStruct((t, RET_V_WIDTH), BF16)] + cast_shapes,
        scratch_shapes=[
            pltpu.VMEM((seq // RET_CHUNK, RET_QK_DIM, RET_V_DIM), BF16),
            pltpu.VMEM((seq // RET_CHUNK, RET_QK_DIM, RET_V_DIM), BF16),
            pltpu.VMEM((RET_QK_DIM, RET_V_DIM), F32),
            pltpu.VMEM((RET_QK_DIM, RET_V_DIM), F32),
        ],
        compiler_params=_params("arbitrary", "arbitrary"),
        name="retention",
    )(log_gamma, proj, proj, proj, proj, gain, *casts)
    return outs[0], outs[1:]


def _ffn0_kernel(x_ref, z_ref, wo_ref, g_ref, wg_ref, wu_ref, wd_ref, g2_ref, cc_ref, sc_ref,
                 o_ref, zr_ref, zi_ref):
    h1 = x_ref[...] + _dot(z_ref[...], wo_ref[...])
    hn = _rms(h1, g_ref[...]).astype(BF16)
    d_ff = wg_ref.shape[1]
    part = None
    for lo in range(0, d_ff, FFN0_SUB):
        cols = slice(lo, min(lo + FFN0_SUB, d_ff))
        hidden = (_silu(_dot(hn, wg_ref[:, cols])) * _dot(hn, wu_ref[:, cols])).astype(BF16)
        d = _dot(hidden, wd_ref[cols, :])
        part = d if part is None else part + d
    h2 = h1 + part
    o_ref[...] = h2
    hn2 = _rms(h2, g2_ref[...]).astype(BF16)
    for grp in range(FNET_GROUPS):
        cols = slice(grp * FNET_GROUP_DIM, (grp + 1) * FNET_GROUP_DIM)
        zr_ref[:, cols] = _dot(hn2[:, cols], cc_ref[...]).astype(BF16)
        zi_ref[:, cols] = _dot(hn2[:, cols], sc_ref[...]).astype(BF16)


def _ffn0(x2d, z, w_out, g, w_gate, w_up, w_down, g2, cc, sc, casts):
    t = x2d.shape[0]
    d_ff = w_gate.shape[1]
    tm = FFN0_TM
    once = pl.Buffered(1)
    row = pl.BlockSpec((tm, D_MODEL), lambda i: (i, 0))
    cast_specs, cast_shapes = _cast_specs(casts, t // tm, lambda i: i)
    outs = pl.pallas_call(
        _with_casts(_ffn0_kernel, 10, 3, len(casts)),
        grid=(t // tm,),
        in_specs=[
            row,
            pl.BlockSpec((tm, RET_V_WIDTH), lambda i: (i, 0)),
            pl.BlockSpec((RET_V_WIDTH, D_MODEL), lambda i: (0, 0), pipeline_mode=once),
            pl.BlockSpec((1, D_MODEL), lambda i: (0, 0), pipeline_mode=once),
            pl.BlockSpec((D_MODEL, d_ff), lambda i: (0, 0), pipeline_mode=once),
            pl.BlockSpec((D_MODEL, d_ff), lambda i: (0, 0), pipeline_mode=once),
            pl.BlockSpec((d_ff, D_MODEL), lambda i: (0, 0), pipeline_mode=once),
            pl.BlockSpec((1, D_MODEL), lambda i: (0, 0), pipeline_mode=once),
            pl.BlockSpec((FNET_GROUP_DIM, FNET_GROUP_DIM), lambda i: (0, 0), pipeline_mode=once),
            pl.BlockSpec((FNET_GROUP_DIM, FNET_GROUP_DIM), lambda i: (0, 0), pipeline_mode=once),
        ] + cast_specs,
        out_specs=[row, row, row] + cast_specs,
        out_shape=[jax.ShapeDtypeStruct((t, D_MODEL), F32), jax.ShapeDtypeStruct((t, D_MODEL), BF16),
                   jax.ShapeDtypeStruct((t, D_MODEL), BF16)] + cast_shapes,
        compiler_params=_params("arbitrary"),
        name="ffn0",
    )(x2d, z, w_out, g, w_gate, w_up, w_down, g2, cc, sc, *casts)
    return outs[:3], outs[3:]


def _dft_across_blocks(xs):
    n = len(xs)
    if n == 1:
        return xs
    even = _dft_across_blocks(xs[0::2])
    odd = _dft_across_blocks(xs[1::2])
    out = [None] * n
    for k in range(n // 2):
        er, ei = even[k]
        o_r, o_i = odd[k]
        if k == 0:
            tr, ti = o_r, o_i
            out[k], out[k + n // 2] = (er + tr, ei + ti), (er - tr, ei - ti)
        elif 4 * k == n:
            out[k], out[k + n // 2] = (er + o_i, ei - o_r), (er - o_i, ei + o_r)
        else:
            c, s = math.cos(2.0 * math.pi * k / n), math.sin(2.0 * math.pi * k / n)
            tr = o_r * c + o_i * s
            ti = o_i * c - o_r * s
            out[k], out[k + n // 2] = (er + tr, ei + ti), (er - tr, ei - ti)
    return out


def _fnet_seq_kernel(zr_ref, zi_ref, gc_ref, gs_ref, o_ref, ar_ref, ai_ref, ys_ref):
    blk = FNET_SEQ_BLOCK
    n1 = zr_ref.shape[0] // blk
    rc = FNET_ROW_CHUNK
    lanes = 128

    def body(r, carry):
        r0 = pl.multiple_of(r * rc, rc)
        for l in range(zr_ref.shape[1] // lanes):
            cols = slice(l * lanes, (l + 1) * lanes)
            zs = [(zr_ref[pl.ds(s1 * blk + r0, rc), cols].astype(F32),
                   zi_ref[pl.ds(s1 * blk + r0, rc), cols].astype(F32)) for s1 in range(n1)]
            for k1, (a_r, a_i) in enumerate(_dft_across_blocks(zs)):
                ar_ref[pl.ds(k1 * blk + r0, rc), cols] = a_r.astype(BF16)
                ai_ref[pl.ds(k1 * blk + r0, rc), cols] = a_i.astype(BF16)
        return carry

    lax.fori_loop(0, blk // rc, body, 0, unroll=2)

    for k1 in range(n1):
        rows = slice(k1 * blk, (k1 + 1) * blk)
        y = _dot(gc_ref[rows, :], ar_ref[rows, :]) + _dot(gs_ref[rows, :], ai_ref[rows, :])
        for l in range(y.shape[1] // lanes):
            ys_ref[l, pl.ds(k1, blk, stride=n1), :] = y[:, l * lanes:(l + 1) * lanes]
    for l in range(o_ref.shape[1] // lanes):
        o_ref[:, l * lanes:(l + 1) * lanes] = ys_ref[l]


def _fnet_sequence(zr, zi, gc, gs, batch, seq, casts):
    t = zr.shape[0]
    dt = FNET_DT
    blkspec = pl.BlockSpec((seq, dt), lambda b, c: (b, c))
    tab = pl.BlockSpec((seq, FNET_SEQ_BLOCK), lambda b, c: (0, 0))
    col_tiles = D_MODEL // dt
    cast_specs, cast_shapes = _cast_specs(casts, batch * col_tiles, lambda b, c: b * col_tiles + c)
    outs = pl.pallas_call(
        _with_casts(_fnet_seq_kernel, 4, 1, len(casts)),
        grid=(batch, col_tiles),
        in_specs=[blkspec, blkspec, tab, tab] + cast_specs,
        out_specs=[blkspec] + cast_specs,
        out_shape=[jax.ShapeDtypeStruct((t, D_MODEL), F32)] + cast_shapes,
        scratch_shapes=[pltpu.VMEM((seq, dt), BF16), pltpu.VMEM((seq, dt), BF16),
                        pltpu.VMEM((dt // 128, seq, 128), F32)],
        compiler_params=_params("arbitrary", "arbitrary"),
        name="fnet_sequence",
    )(zr, zi, gc, gs, *casts)
    return outs[0], outs[1:]


def _router_kernel(h_ref, y_ref, w_ref, g_ref, rhi_ref, rlo_ref, h3_ref, hn_ref, idx_ref, wt_ref):
    h3 = h_ref[...] + _dot(y_ref[...].astype(BF16), w_ref[...])
    h3_ref[...] = h3
    hn = _rms(h3, g_ref[...])
    for c in range(D_MODEL // 128):
        hn_ref[pl.ds(c, hn.shape[0], stride=D_MODEL // 128), :] = hn[:, c * 128:(c + 1) * 128]
    hi = hn.astype(BF16)
    lo = (hn - hi.astype(F32)).astype(BF16)

    def nt(a, b):
        return lax.dot_general(a, b, (((1,), (1,)), ((), ())), preferred_element_type=F32)

    logits = nt(rhi_ref[...], hi) + (nt(rhi_ref[...], lo) + nt(rlo_ref[...], hi))
    row = lax.broadcasted_iota(jnp.int32, logits.shape, 0)
    m1 = jnp.max(logits, axis=0, keepdims=True)
    i1 = jnp.min(jnp.where(logits == m1, row, N_EXPERTS), axis=0, keepdims=True)
    rest = jnp.where(row == i1, -jnp.inf, logits)
    m2 = jnp.max(rest, axis=0, keepdims=True)
    i2 = jnp.min(jnp.where(rest == m2, row, N_EXPERTS), axis=0, keepdims=True)
    e2 = jnp.exp(m2 - m1)
    inv = 1.0 / (1.0 + e2)
    idx_ref[0:1, :] = i1
    idx_ref[1:2, :] = i2
    wt_ref[0:1, :] = inv
    wt_ref[1:2, :] = e2 * inv


def _router(h2, y, w_fnet, g, r_hi, r_lo):
    t = h2.shape[0]
    tm = ROW_TM
    row = pl.BlockSpec((tm, D_MODEL), lambda i: (i, 0))
    pair = pl.BlockSpec((TOP_K, tm), lambda i: (0, i))
    rtr = pl.BlockSpec((N_EXPERTS, D_MODEL), lambda i: (0, 0))
    return pl.pallas_call(
        _router_kernel,
        grid=(t // tm,),
        in_specs=[row, row, pl.BlockSpec((D_MODEL, D_MODEL), lambda i: (0, 0)),
                  pl.BlockSpec((1, D_MODEL), lambda i: (0, 0)), rtr, rtr],
        out_specs=[row, pl.BlockSpec((tm * (D_MODEL // 128), 128), lambda i: (i, 0)), pair, pair],
        out_shape=[jax.ShapeDtypeStruct((t, D_MODEL), F32), jax.ShapeDtypeStruct((t * (D_MODEL // 128), 128), F32),
                   jax.ShapeDtypeStruct((TOP_K, t), jnp.int32), jax.ShapeDtypeStruct((TOP_K, t), F32)],
        compiler_params=_params("arbitrary"),
        name="router",
    )(h2, y, w_fnet, g, r_hi, r_lo)


def _routing_tables(top_idx, tm):
    t = top_idx.shape[1]
    n_pairs = TOP_K * t
    n_tiles = n_pairs // tm + N_EXPERTS
    e_flat = top_idx.reshape(n_pairs)
    order = jnp.argsort(e_flat, stable=True).astype(jnp.int32)
    experts = jnp.arange(N_EXPERTS, dtype=jnp.int32)
    counts = jnp.sum((e_flat[:, None] == experts[None, :]).astype(jnp.int32), axis=0)
    off = jnp.cumsum(counts) - counts
    tiles_e = (counts + tm - 1) // tm
    tile_end = jnp.cumsum(tiles_e)
    tile_off = tile_end - tiles_e
    n_used = tile_end[-1]
    last_e = jnp.max(jnp.where(tiles_e > 0, experts, 0))
    tile_ids = jnp.arange(n_tiles, dtype=jnp.int32)
    tile_e = jnp.minimum(jnp.sum((tile_ids[:, None] >= tile_end[None, :]).astype(jnp.int32), axis=1), last_e)
    u = jnp.arange(tm, dtype=jnp.int32)
    rank = (tile_ids - tile_off[tile_e])[:, None] * tm + u[None, :]
    valid = (tile_ids < n_used)[:, None] & (rank < counts[tile_e][:, None])
    src = order[jnp.clip(off[tile_e][:, None] + rank, 0, n_pairs - 1)]
    tok = jnp.where(valid, src % t, 0).astype(jnp.int32)
    spare = n_pairs + jnp.broadcast_to(u[None, :], (1, tm))
    dst = jnp.where(valid, src, spare).astype(jnp.int32)
    tok_next = jnp.concatenate([tok[1:], jnp.zeros((2, tm), jnp.int32)], axis=0)
    dst_prev = jnp.concatenate([spare.astype(jnp.int32), dst], axis=0)
    tile_e = jnp.concatenate([tile_e, tile_e[-1:]]).astype(jnp.int32)
    return (tile_e, n_used.reshape(1).astype(jnp.int32), tok[0].reshape(1, 1, tm),
            tok_next.reshape(n_tiles + 1, 1, tm), dst_prev.reshape(n_tiles + 1, 1, tm))


def _moe_kernel(te_ref, nu_ref, tok0_ref, tokn_ref, dstp_ref, hn_hbm, wg_ref, wu_ref, wd_ref,
                y2_hbm, xbuf, xb, acc, ybuf, gsem, ssem, *, n_ff):
    del te_ref
    i = pl.program_id(0)
    j = pl.program_id(1)
    n_used = nu_ref[0]
    tm = xb.shape[0]
    slabs = xb.shape[1] // 128
    fc = wg_ref.shape[2]
    rows_per_step = tm // n_ff
    spare0 = y2_hbm.shape[0] - tm
    unroll = 8

    def tile_rows(r):
        start = r * slabs
        return pl.ds(start if isinstance(start, int) else pl.multiple_of(start, slabs), slabs)

    def gather_copy(tok, u):
        return pltpu.make_async_copy(hn_hbm.at[tile_rows(tok), :], xbuf.at[tile_rows(u), :], gsem)

    def scatter_copy(u, row):
        return pltpu.make_async_copy(ybuf.at[pl.ds(u, 1), :], y2_hbm.at[pl.ds(row, 1), :], ssem)

    def start_gather(ref):
        def body(u, carry):
            gather_copy(ref[0, 0, u], u).start()
            return carry
        lax.fori_loop(0, tm, body, 0, unroll=unroll)

    def wait_gather():
        pltpu.make_async_copy(hn_hbm.at[pl.ds(0, tm * slabs), :], xbuf, gsem).wait()

    def start_scatter_all():
        def body(u, carry):
            scatter_copy(u, dstp_ref[0, 0, u]).start()
            return carry
        lax.fori_loop(0, tm, body, 0, unroll=unroll)

    def wait_scatter():
        pltpu.make_async_copy(ybuf, y2_hbm.at[pl.ds(0, tm), :], ssem).wait()

    @pl.when((i == 0) & (j == 0))
    def _():
        ybuf[...] = jnp.zeros_like(ybuf)
        fill = pltpu.make_async_copy(ybuf, y2_hbm.at[pl.ds(spare0, tm), :], ssem)
        fill.start()
        fill.wait()
        start_gather(tok0_ref)

    def issue_gathers(jj):
        for u in range(jj * rows_per_step, (jj + 1) * rows_per_step):
            gather_copy(tokn_ref[0, 0, u], u).start()

    def ff_step(jj):
        if jj == 0:
            wait_gather()
            for c in range(slabs):
                xb[:, c * 128:(c + 1) * 128] = xbuf[pl.ds(c, tm, stride=slabs), :].astype(BF16)
            for u in range(tm):
                scatter_copy(u, dstp_ref[0, 0, u]).start(priority=1)
            issue_gathers(0)
        if jj == n_ff - 1:
            wait_scatter()

        x = xb[...]
        part = None
        for lo in range(0, fc, MOE_SUB):
            cols = slice(lo, min(lo + MOE_SUB, fc))
            hidden = (_silu(_dot(x, wg_ref[0, :, cols])) * _dot(x, wu_ref[0, :, cols])).astype(BF16)
            d = _dot(hidden, wd_ref[0, cols, :])
            part = d if part is None else part + d

        if jj == 0:
            acc[...] = part
        elif jj < n_ff - 1:
            acc[...] += part
        else:
            ybuf[...] = acc[...] + part

    for jj in range(n_ff):
        if jj > 0:
            pl.when((i < n_used) & (j == jj))(functools.partial(issue_gathers, jj))
        pl.when((i < n_used) & (j == jj))(functools.partial(ff_step, jj))

    @pl.when((i == n_used) & (j == 0))
    def _():
        start_scatter_all()
        wait_scatter()
        wait_gather()


def _moe(hn, tile_e, n_used, tok0, tok_next, dst_prev, w_gate, w_up, w_down):
    t = hn.shape[0] // (D_MODEL // 128)
    d_ff = w_gate.shape[2]
    tm, fc = MOE_TM, MOE_FC
    n_steps = tok_next.shape[0]
    n_ff = d_ff // fc
    assert n_ff >= 2 and n_ff * fc == d_ff and tm % n_ff == 0

    def ff_step(i, j, nu):
        return jnp.where(i < nu[0], j, n_ff - 1)

    grid_spec = pltpu.PrefetchScalarGridSpec(
        num_scalar_prefetch=2,
        grid=(n_steps, n_ff),
        in_specs=[
            pl.BlockSpec((1, 1, tm), lambda i, j, te, nu: (0, 0, 0), memory_space=pltpu.SMEM),
            pl.BlockSpec((1, 1, tm), lambda i, j, te, nu: (i, 0, 0), memory_space=pltpu.SMEM),
            pl.BlockSpec((1, 1, tm), lambda i, j, te, nu: (i, 0, 0), memory_space=pltpu.SMEM),
            pl.BlockSpec(memory_space=pl.ANY),
            pl.BlockSpec((1, D_MODEL, fc), lambda i, j, te, nu: (te[i], 0, ff_step(i, j, nu))),
            pl.BlockSpec((1, D_MODEL, fc), lambda i, j, te, nu: (te[i], 0, ff_step(i, j, nu))),
            pl.BlockSpec((1, fc, D_MODEL), lambda i, j, te, nu: (te[i], ff_step(i, j, nu), 0)),
        ],
        out_specs=pl.BlockSpec(memory_space=pl.ANY),
        scratch_shapes=[
            pltpu.VMEM((tm * (D_MODEL // 128), 128), F32),
            pltpu.VMEM((tm, D_MODEL), BF16),
            pltpu.VMEM((tm, D_MODEL), F32),
            pltpu.VMEM((tm, D_MODEL), F32),
            pltpu.SemaphoreType.DMA(()),
            pltpu.SemaphoreType.DMA(()),
        ],
    )
    return pl.pallas_call(
        functools.partial(_moe_kernel, n_ff=n_ff),
        grid_spec=grid_spec,
        out_shape=jax.ShapeDtypeStruct((TOP_K * t + tm, D_MODEL), F32),
        compiler_params=_params("arbitrary", "arbitrary"),
        name="moe",
    )(tile_e, n_used, tok0, tok_next, dst_prev, hn, w_gate, w_up, w_down)


def _final_kernel(h_ref, ya_ref, yb_ref, wt_ref, g_ref, o_ref):
    w = wt_ref[...]
    moe = ya_ref[...] * w[:, 0:1] + yb_ref[...] * w[:, 1:2]
    o_ref[...] = _rms(h_ref[...] + moe, g_ref[...])


def _final(h3, y2, top_w, g):
    t = h3.shape[0]
    tm = ROW_TM
    second = t // tm
    row = pl.BlockSpec((tm, D_MODEL), lambda i: (i, 0))
    return pl.pallas_call(
        _final_kernel,
        grid=(t // tm,),
        in_specs=[row, row, pl.BlockSpec((tm, D_MODEL), lambda i: (i + second, 0)),
                  pl.BlockSpec((tm, TOP_K), lambda i: (i, 0)), pl.BlockSpec((1, D_MODEL), lambda i: (0, 0))],
        out_specs=row,
        out_shape=jax.ShapeDtypeStruct((t, D_MODEL), F32),
        compiler_params=_params("arbitrary"),
        name="final_norm",
    )(h3, y2, y2, top_w, g)


def _rotary_tables(seq):
    half = RET_QK_DIM // 2
    inv = ROPE_BASE ** (-np.arange(half, dtype=np.float64) / half)
    ang = np.arange(seq, dtype=np.float64)[:, None] * inv[None, :]
    return jnp.asarray(np.cos(ang), F32), jnp.asarray(np.sin(ang), F32)


def _channel_dft_tables(n, scale):
    k = np.arange(n, dtype=np.int64)
    ang = ((k[:, None] * k[None, :]) % n) * (2.0 * math.pi / n)
    return (jnp.asarray(np.cos(ang) * scale, F32).astype(BF16),
            jnp.asarray(-np.sin(ang) * scale, F32).astype(BF16))


def _sequence_dft_tables(seq, scale):
    blk = FNET_SEQ_BLOCK
    n1 = seq // blk
    k1 = np.arange(n1, dtype=np.int64)[:, None, None]
    k2 = np.arange(blk, dtype=np.int64)[None, :, None]
    s2 = np.arange(blk, dtype=np.int64)[None, None, :]
    ang = (((k1 + n1 * k2) * s2) % seq) * (2.0 * math.pi / seq)
    gc = jnp.asarray((np.cos(ang) * scale).reshape(n1 * blk, blk), F32).astype(BF16)
    gs = jnp.asarray((np.sin(ang) * scale).reshape(n1 * blk, blk), F32).astype(BF16)
    return gc, gs


def kernel(x, mix_norm, ffn_norm, ret_w_in, ret_decay_logit, ret_gn_gain, ret_w_out,
           dense_w_gate, dense_w_up, dense_w_down, fnet_w_out, moe_router,
           moe_w_gate, moe_w_up, moe_w_down, final_norm):
    batch, seq, d = x.shape
    t = batch * seq
    n1 = seq // FNET_SEQ_BLOCK
    assert d == D_MODEL and seq % RET_CHUNK == 0 and seq % PROJ_TM == 0
    assert n1 * FNET_SEQ_BLOCK == seq and n1 & (n1 - 1) == 0
    x2d = x.reshape(t, d)

    cos, sin = _rotary_tables(seq)
    n_exp, _, d_ffe = moe_w_gate[0].shape
    proj, (moe_wd,) = _proj(x2d, mix_norm[0:1], cos, sin, ret_w_in[0].astype(BF16), seq,
                            [moe_w_down[0].reshape(n_exp * d_ffe, d)])
    log_gamma = jax.nn.log_sigmoid(ret_decay_logit[0].astype(F32))
    z, (w_out, d_wg, d_wu, d_wd) = _retention(
        proj, log_gamma, ret_gn_gain[0:1], batch, seq,
        [ret_w_out[0], dense_w_gate[0], dense_w_up[0], dense_w_down[0]])
    cc, sc = _channel_dft_tables(FNET_GROUP_DIM, FNET_GROUP_DIM ** -0.5)
    gc, gs = _sequence_dft_tables(seq, seq ** -0.5)
    (h2, zr, zi), (moe_wu, w_fnet) = _ffn0(
        x2d, z, w_out, ffn_norm[0:1], d_wg, d_wu, d_wd, mix_norm[1:2], cc, sc,
        [moe_w_up[0].reshape(n_exp * d, d_ffe), fnet_w_out[0]])

    y, (moe_wg,) = _fnet_sequence(zr, zi, gc, gs, batch, seq, [moe_w_gate[0].reshape(n_exp * d, d_ffe)])
    router_t = moe_router[0].T
    r_hi = router_t.astype(BF16)
    r_lo = (router_t - r_hi.astype(F32)).astype(BF16)
    h3, hn3, top_idx, top_w = _router(h2, y, w_fnet, ffn_norm[1:2], r_hi, r_lo)

    tile_e, n_used, tok0, tok_next, dst_prev = _routing_tables(top_idx, MOE_TM)
    y2 = _moe(hn3, tile_e, n_used, tok0, tok_next, dst_prev,
              moe_wg.reshape(n_exp, d, d_ffe), moe_wu.reshape(n_exp, d, d_ffe),
              moe_wd.reshape(n_exp, d_ffe, d))
    out = _final(h3, y2, top_w.T, final_norm.reshape(1, d))
    return out.reshape(batch, seq, d)
```

```python
import functools
import math

import jax
import jax.numpy as jnp
import numpy as np
from jax import lax
from jax.experimental import pallas as pl
from jax.experimental.pallas import tpu as pltpu

F32 = jnp.float32
BF16 = jnp.bfloat16

D_MODEL = 1024
RET_HEADS = 4
RET_QK_DIM = D_MODEL // RET_HEADS
RET_V_DIM = 2 * RET_QK_DIM
RET_QK_WIDTH = RET_HEADS * RET_QK_DIM
RET_V_WIDTH = RET_HEADS * RET_V_DIM
RET_IN_WIDTH = 2 * RET_QK_WIDTH + 2 * RET_V_WIDTH
ROPE_BASE = 10000.0
FNET_GROUPS = 4
FNET_GROUP_DIM = D_MODEL // FNET_GROUPS
N_EXPERTS = 8
TOP_K = 2
EPS = 1e-6

RET_CHUNK = 256
VMEM_LIMIT_BYTES = 56 * 1024 * 1024

PROJ_TM, PROJ_TN = 1024, 2048
FFN0_TM, FFN0_SUB = 512, 512
ROW_TM = 1024
FNET_SEQ_BLOCK = 256
FNET_DT = 256
FNET_ROW_CHUNK = 16
MOE_TM, MOE_FC, MOE_SUB = 1024, 1792, 512


def _params(*semantics):
    return pltpu.CompilerParams(dimension_semantics=semantics, vmem_limit_bytes=VMEM_LIMIT_BYTES)


def _dot(a, b):
    return jnp.dot(a, b, preferred_element_type=F32)


def _lane_mean(x):
    lanes = 128
    n = x.shape[-1]
    acc = x[:, 0:lanes]
    for lo in range(lanes, n, lanes):
        acc = acc + x[:, lo:lo + lanes]
    return jnp.sum(acc, axis=-1, keepdims=True) * (1.0 / n)


def _rms(x, g):
    return x * lax.rsqrt(_lane_mean(x * x) + EPS) * g


def _silu(a):
    return a * (1.0 / (1.0 + jnp.exp(-a)))


BF16_ROW_TILE = 16


def _cast_specs(arrays, n_steps, step_of):
    specs, shapes = [], []
    for a in arrays:
        rows, cols = a.shape
        n_blocks = n_steps
        while rows % (n_blocks * BF16_ROW_TILE):
            n_blocks //= 2
        assert n_blocks >= 1

        def index(*ids, n_blocks=n_blocks):
            return (jnp.minimum(step_of(*ids), n_blocks - 1), 0)

        specs.append(pl.BlockSpec((rows // n_blocks, cols), index))
        shapes.append(jax.ShapeDtypeStruct(a.shape, BF16))
    return specs, shapes


def _with_casts(body, n_in, n_out, n_cast):
    def kernel(*refs):
        ins, refs = refs[:n_in], refs[n_in:]
        srcs, refs = refs[:n_cast], refs[n_cast:]
        outs, refs = refs[:n_out], refs[n_out:]
        dsts, scratch = refs[:n_cast], refs[n_cast:]
        for src, dst in zip(srcs, dsts):
            dst[...] = src[...].astype(BF16)
        body(*ins, *outs, *scratch)
    return kernel


def _proj_kernel(x_ref, g_ref, cos_ref, sin_ref, w_ref, o_ref, hn_ref):
    j = pl.program_id(1)

    @pl.when(j == 0)
    def _():
        hn_ref[...] = _rms(x_ref[...], g_ref[...]).astype(BF16)

    tn = o_ref.shape[1]
    half = RET_QK_DIM // 2
    k_start, v_start, g_start = RET_QK_WIDTH, 2 * RET_QK_WIDTH, 2 * RET_QK_WIDTH + RET_V_WIDTH

    def column_tile(jj):
        hn = hn_ref[...]
        for h in range(tn // RET_QK_DIM):
            lo = h * RET_QK_DIM
            col = jj * tn + lo
            y = _dot(hn, w_ref[:, col:col + RET_QK_DIM])
            if col < v_start:
                scale = RET_QK_DIM ** -0.5 if col >= k_start else 1.0
                cos = cos_ref[...] * scale
                sin = sin_ref[...] * scale
                t1 = y[:, :half]
                t2 = y[:, half:]
                o_ref[:, lo:lo + half] = (t1 * cos - t2 * sin).astype(BF16)
                o_ref[:, lo + half:lo + RET_QK_DIM] = (t1 * sin + t2 * cos).astype(BF16)
            elif col < g_start:
                o_ref[:, lo:lo + RET_QK_DIM] = y.astype(BF16)
            else:
                o_ref[:, lo:lo + RET_QK_DIM] = _silu(y).astype(BF16)

    for jj in range(w_ref.shape[1] // tn):
        pl.when(j == jj)(functools.partial(column_tile, jj))


def _proj(x2d, g, cos, sin, w_in, seq, casts):
    t = x2d.shape[0]
    tm, tn = PROJ_TM, PROJ_TN
    pos_tiles = seq // tm
    cast_specs, cast_shapes = _cast_specs(casts, t // tm, lambda i, j: i)
    outs = pl.pallas_call(
        _with_casts(_proj_kernel, 5, 1, len(casts)),
        grid=(t // tm, RET_IN_WIDTH // tn),
        in_specs=[
            pl.BlockSpec((tm, D_MODEL), lambda i, j: (i, 0)),
            pl.BlockSpec((1, D_MODEL), lambda i, j: (0, 0)),
            pl.BlockSpec((tm, RET_QK_DIM // 2), lambda i, j: (i % pos_tiles, 0)),
            pl.BlockSpec((tm, RET_QK_DIM // 2), lambda i, j: (i % pos_tiles, 0)),
            pl.BlockSpec((D_MODEL, RET_IN_WIDTH), lambda i, j: (0, 0), pipeline_mode=pl.Buffered(1)),
        ] + cast_specs,
        out_specs=[pl.BlockSpec((tm, tn), lambda i, j: (i, j))] + cast_specs,
        out_shape=[jax.ShapeDtypeStruct((t, RET_IN_WIDTH), BF16)] + cast_shapes,
        scratch_shapes=[pltpu.VMEM((tm, D_MODEL), BF16)],
        compiler_params=_params("arbitrary", "arbitrary"),
        name="proj",
    )(x2d, g, cos, sin, w_in, *casts)
    return outs[0], outs[1:]


def _ret_kernel(lg_ref, q_ref, k_ref, v_ref, g_ref, gain_ref, o_ref, sfa_ref, sba_ref, sf_ref, sb_ref):
    c = RET_CHUNK
    n_chunks = q_ref.shape[0] // c
    h = pl.program_id(1)
    lgf = lg_ref[0, h]
    lgb = lg_ref[1, h]

    row = lax.broadcasted_iota(jnp.int32, (c, RET_QK_DIM), 0).astype(F32)
    xi_f = jnp.exp((row + 1.0) * lgf).astype(BF16)
    zeta_f = jnp.exp((c - 1.0 - row) * lgf).astype(BF16)
    xi_b = jnp.exp((c - row) * lgb).astype(BF16)
    zeta_b = jnp.exp(row * lgb).astype(BF16)
    g_f = jnp.exp(jnp.full((1, RET_V_DIM), c, F32) * lgf)
    g_b = jnp.exp(jnp.full((1, RET_V_DIM), c, F32) * lgb)
    di = lax.broadcasted_iota(jnp.int32, (c, c), 0)
    dj = lax.broadcasted_iota(jnp.int32, (c, c), 1)
    diff = (di - dj).astype(F32)
    decay = jnp.exp(jnp.where(diff >= 0, diff * lgf, -diff * lgb))

    def chunk(n):
        r0 = pl.multiple_of(n * c, c)
        return pl.ds(r0, c)

    def scaled(t, s):
        return t * s

    def outer(kz, vn):
        return lax.dot_general(kz, vn, (((0,), (0,)), ((), ())), preferred_element_type=F32)

    sf_ref[...] = jnp.zeros_like(sf_ref)
    sb_ref[...] = jnp.zeros_like(sb_ref)

    def states(step, carry):
        nf = step
        nb = n_chunks - 1 - step
        sf = sf_ref[...]
        sb = sb_ref[...]
        sfa_ref[nf] = sf.astype(BF16)
        sba_ref[nb] = sb.astype(BF16)
        rf, rb = chunk(nf), chunk(nb)
        sf_ref[...] = sf * g_f + outer(scaled(k_ref[rf, :], zeta_f), v_ref[rf, :])
        sb_ref[...] = sb * g_b + outer(scaled(k_ref[rb, :], zeta_b), v_ref[rb, :])
        return carry

    lax.fori_loop(0, n_chunks, states, 0, unroll=8)

    gain = gain_ref[...]

    def fwd(n, carry):
        rows = chunk(n)
        qn, kn, vn = q_ref[rows, :], k_ref[rows, :], v_ref[rows, :]
        scores = lax.dot_general(qn, kn, (((1,), (1,)), ((), ())), preferred_element_type=F32) * decay
        y = (_dot(scores.astype(BF16), vn) + _dot(scaled(qn, xi_f), sfa_ref[n])
             + _dot(scaled(qn, xi_b), sba_ref[n]))
        mu = _lane_mean(y)
        yc = y - mu
        var = _lane_mean(yc * yc)
        yn = yc * lax.rsqrt(var + EPS) * gain
        o_ref[rows, :] = g_ref[rows, :] * yn.astype(BF16)
        return carry

    lax.fori_loop(0, n_chunks, fwd, 0, unroll=16)


def _retention(proj, log_gamma, gain, batch, seq, casts):
    t = proj.shape[0]
    qk_blocks = RET_QK_WIDTH // RET_QK_DIM
    k_off = qk_blocks
    v_off = 2 * RET_QK_WIDTH // RET_V_DIM
    g_off = v_off + RET_V_WIDTH // RET_V_DIM
    cast_specs, cast_shapes = _cast_specs(casts, batch * RET_HEADS, lambda b, h: b * RET_HEADS + h)
    outs = pl.pallas_call(
        _with_casts(_ret_kernel, 6, 1, len(casts)),
        grid=(batch, RET_HEADS),
        in_specs=[
            pl.BlockSpec(memory_space=pltpu.SMEM),
            pl.BlockSpec((seq, RET_QK_DIM), lambda b, h: (b, h)),
            pl.BlockSpec((seq, RET_QK_DIM), lambda b, h: (b, k_off + h)),
            pl.BlockSpec((seq, RET_V_DIM), lambda b, h: (b, v_off + h)),
            pl.BlockSpec((seq, RET_V_DIM), lambda b, h: (b, g_off + h)),
            pl.BlockSpec((1, RET_V_DIM), lambda b, h: (0, h)),
        ] + cast_specs,
        out_specs=[pl.BlockSpec((seq, RET_V_DIM), lambda b, h: (b, h))] + cast_specs,
        out_shape=[jax.ShapeDtypeStruct((t, RET_V_WIDTH), BF16)] + cast_shapes,
        scratch_shapes=[
            pltpu.VMEM((seq // RET_CHUNK, RET_QK_DIM, RET_V_DIM), BF16),
            pltpu.VMEM((seq // RET_CHUNK, RET_QK_DIM, RET_V_DIM), BF16),
            pltpu.VMEM((RET_QK_DIM, RET_V_DIM), F32),
            pltpu.VMEM((RET_QK_DIM, RET_V_DIM), F32),
        ],
        compiler_params=_params("arbitrary", "arbitrary"),
        name="retention",
    )(log_gamma, proj, proj, proj, proj, gain, *casts)
    return outs[0], outs[1:]


def _ffn0_kernel(x_ref, z_ref, wo_ref, g_ref, wg_ref, wu_ref, wd_ref, g2_ref, cc_ref, sc_ref,
                 o_ref, zr_ref, zi_ref):
    h1 = x_ref[...] + _dot(z_ref[...], wo_ref[...])
    hn = _rms(h1, g_ref[...]).astype(BF16)
    d_ff = wg_ref.shape[1]
    part = None
    for lo in range(0, d_ff, FFN0_SUB):
        cols = slice(lo, min(lo + FFN0_SUB, d_ff))
        hidden = (_silu(_dot(hn, wg_ref[:, cols])) * _dot(hn, wu_ref[:, cols])).astype(BF16)
        d = _dot(hidden, wd_ref[cols, :])
        part = d if part is None else part + d
    h2 = h1 + part
    o_ref[...] = h2
    hn2 = _rms(h2, g2_ref[...]).astype(BF16)
    for grp in range(FNET_GROUPS):
        cols = slice(grp * FNET_GROUP_DIM, (grp + 1) * FNET_GROUP_DIM)
        zr_ref[:, cols] = _dot(hn2[:, cols], cc_ref[...]).astype(BF16)
        zi_ref[:, cols] = _dot(hn2[:, cols], sc_ref[...]).astype(BF16)


def _ffn0(x2d, z, w_out, g, w_gate, w_up, w_down, g2, cc, sc, casts):
    t = x2d.shape[0]
    d_ff = w_gate.shape[1]
    tm = FFN0_TM
    once = pl.Buffered(1)
    row = pl.BlockSpec((tm, D_MODEL), lambda i: (i, 0))
    cast_specs, cast_shapes = _cast_specs(casts, t // tm, lambda i: i)
    outs = pl.pallas_call(
        _with_casts(_ffn0_kernel, 10, 3, len(casts)),
        grid=(t // tm,),
        in_specs=[
            row,
            pl.BlockSpec((tm, RET_V_WIDTH), lambda i: (i, 0)),
            pl.BlockSpec((RET_V_WIDTH, D_MODEL), lambda i: (0, 0), pipeline_mode=once),
            pl.BlockSpec((1, D_MODEL), lambda i: (0, 0), pipeline_mode=once),
            pl.BlockSpec((D_MODEL, d_ff), lambda i: (0, 0), pipeline_mode=once),
            pl.BlockSpec((D_MODEL, d_ff), lambda i: (0, 0), pipeline_mode=once),
            pl.BlockSpec((d_ff, D_MODEL), lambda i: (0, 0), pipeline_mode=once),
            pl.BlockSpec((1, D_MODEL), lambda i: (0, 0), pipeline_mode=once),
            pl.BlockSpec((FNET_GROUP_DIM, FNET_GROUP_DIM), lambda i: (0, 0), pipeline_mode=once),
            pl.BlockSpec((FNET_GROUP_DIM, FNET_GROUP_DIM), lambda i: (0, 0), pipeline_mode=once),
        ] + cast_specs,
        out_specs=[row, row, row] + cast_specs,
        out_shape=[jax.ShapeDtypeStruct((t, D_MODEL), F32), jax.ShapeDtypeStruct((t, D_MODEL), BF16),
                   jax.ShapeDtypeStruct((t, D_MODEL), BF16)] + cast_shapes,
        compiler_params=_params("arbitrary"),
        name="ffn0",
    )(x2d, z, w_out, g, w_gate, w_up, w_down, g2, cc, sc, *casts)
    return outs[:3], outs[3:]


def _dft_across_blocks(xs):
    n = len(xs)
    if n == 1:
        return xs
    even = _dft_across_blocks(xs[0::2])
    odd = _dft_across_blocks(xs[1::2])
    out = [None] * n
    for k in range(n // 2):
        er, ei = even[k]
        o_r, o_i = odd[k]
        if k == 0:
            tr, ti = o_r, o_i
            out[k], out[k + n // 2] = (er + tr, ei + ti), (er - tr, ei - ti)
        elif 4 * k == n:
            out[k], out[k + n // 2] = (er + o_i, ei - o_r), (er - o_i, ei + o_r)
        else:
            c, s = math.cos(2.0 * math.pi * k / n), math.sin(2.0 * math.pi * k / n)
            tr = o_r * c + o_i * s
            ti = o_i * c - o_r * s
            out[k], out[k + n // 2] = (er + tr, ei + ti), (er - tr, ei - ti)
    return out


def _fnet_seq_kernel(zr_ref, zi_ref, gc_ref, gs_ref, o_ref, ar_ref, ai_ref, ys_ref):
    blk = FNET_SEQ_BLOCK
    n1 = zr_ref.shape[0] // blk
    rc = FNET_ROW_CHUNK
    lanes = 128

    def body(r, carry):
        r0 = pl.multiple_of(r * rc, rc)
        for l in range(zr_ref.shape[1] // lanes):
            cols = slice(l * lanes, (l + 1) * lanes)
            zs = [(zr_ref[pl.ds(s1 * blk + r0, rc), cols].astype(F32),
                   zi_ref[pl.ds(s1 * blk + r0, rc), cols].astype(F32)) for s1 in range(n1)]
            for k1, (a_r, a_i) in enumerate(_dft_across_blocks(zs)):
                ar_ref[pl.ds(k1 * blk + r0, rc), cols] = a_r.astype(BF16)
                ai_ref[pl.ds(k1 * blk + r0, rc), cols] = a_i.astype(BF16)
        return carry

    lax.fori_loop(0, blk // rc, body, 0, unroll=2)

    for k1 in range(n1):
        rows = slice(k1 * blk, (k1 + 1) * blk)
        y = _dot(gc_ref[rows, :], ar_ref[rows, :]) + _dot(gs_ref[rows, :], ai_ref[rows, :])
        for l in range(y.shape[1] // lanes):
            ys_ref[l, pl.ds(k1, blk, stride=n1), :] = y[:, l * lanes:(l + 1) * lanes]
    for l in range(o_ref.shape[1] // lanes):
        o_ref[:, l * lanes:(l + 1) * lanes] = ys_ref[l]


def _fnet_sequence(zr, zi, gc, gs, batch, seq, casts):
    t = zr.shape[0]
    dt = FNET_DT
    blkspec = pl.BlockSpec((seq, dt), lambda b, c: (b, c))
    tab = pl.BlockSpec((seq, FNET_SEQ_BLOCK), lambda b, c: (0, 0))
    col_tiles = D_MODEL // dt
    cast_specs, cast_shapes = _cast_specs(casts, batch * col_tiles, lambda b, c: b * col_tiles + c)
    outs = pl.pallas_call(
        _with_casts(_fnet_seq_kernel, 4, 1, len(casts)),
        grid=(batch, col_tiles),
        in_specs=[blkspec, blkspec, tab, tab] + cast_specs,
        out_specs=[blkspec] + cast_specs,
        out_shape=[jax.ShapeDtypeStruct((t, D_MODEL), F32)] + cast_shapes,
        scratch_shapes=[pltpu.VMEM((seq, dt), BF16), pltpu.VMEM((seq, dt), BF16),
                        pltpu.VMEM((dt // 128, seq, 128), F32)],
        compiler_params=_params("arbitrary", "arbitrary"),
        name="fnet_sequence",
    )(zr, zi, gc, gs, *casts)
    return outs[0], outs[1:]


def _router_kernel(h_ref, y_ref, w_ref, g_ref, rhi_ref, rlo_ref, h3_ref, hn_ref, idx_ref, wt_ref):
    h3 = h_ref[...] + _dot(y_ref[...].astype(BF16), w_ref[...])
    h3_ref[...] = h3
    hn = _rms(h3, g_ref[...])
    hn_ref[...] = hn
    hi = hn.astype(BF16)
    lo = (hn - hi.astype(F32)).astype(BF16)

    def nt(a, b):
        return lax.dot_general(a, b, (((1,), (1,)), ((), ())), preferred_element_type=F32)

    logits = nt(rhi_ref[...], hi) + (nt(rhi_ref[...], lo) + nt(rlo_ref[...], hi))
    row = lax.broadcasted_iota(jnp.int32, logits.shape, 0)
    m1 = jnp.max(logits, axis=0, keepdims=True)
    i1 = jnp.min(jnp.where(logits == m1, row, N_EXPERTS), axis=0, keepdims=True)
    rest = jnp.where(row == i1, -jnp.inf, logits)
    m2 = jnp.max(rest, axis=0, keepdims=True)
    i2 = jnp.min(jnp.where(rest == m2, row, N_EXPERTS), axis=0, keepdims=True)
    e2 = jnp.exp(m2 - m1)
    inv = 1.0 / (1.0 + e2)
    idx_ref[0:1, :] = i1
    idx_ref[1:2, :] = i2
    wt_ref[0:1, :] = inv
    wt_ref[1:2, :] = e2 * inv


def _router(h2, y, w_fnet, g, r_hi, r_lo):
    t = h2.shape[0]
    tm = ROW_TM
    row = pl.BlockSpec((tm, D_MODEL), lambda i: (i, 0))
    pair = pl.BlockSpec((TOP_K, tm), lambda i: (0, i))
    rtr = pl.BlockSpec((N_EXPERTS, D_MODEL), lambda i: (0, 0))
    return pl.pallas_call(
        _router_kernel,
        grid=(t // tm,),
        in_specs=[row, row, pl.BlockSpec((D_MODEL, D_MODEL), lambda i: (0, 0)),
                  pl.BlockSpec((1, D_MODEL), lambda i: (0, 0)), rtr, rtr],
        out_specs=[row, row, pair, pair],
        out_shape=[jax.ShapeDtypeStruct((t, D_MODEL), F32), jax.ShapeDtypeStruct((t, D_MODEL), F32),
                   jax.ShapeDtypeStruct((TOP_K, t), jnp.int32), jax.ShapeDtypeStruct((TOP_K, t), F32)],
        compiler_params=_params("arbitrary"),
        name="router",
    )(h2, y, w_fnet, g, r_hi, r_lo)


def _routing_tables(top_idx, tm):
    t = top_idx.shape[1]
    n_pairs = TOP_K * t
    n_tiles = n_pairs // tm + N_EXPERTS
    e_flat = top_idx.reshape(n_pairs)
    order = jnp.argsort(e_flat, stable=True).astype(jnp.int32)
    experts = jnp.arange(N_EXPERTS, dtype=jnp.int32)
    counts = jnp.sum((e_flat[:, None] == experts[None, :]).astype(jnp.int32), axis=0)
    off = jnp.cumsum(counts) - counts
    tiles_e = (counts + tm - 1) // tm
    tile_end = jnp.cumsum(tiles_e)
    tile_off = tile_end - tiles_e
    n_used = tile_end[-1]
    last_e = jnp.max(jnp.where(tiles_e > 0, experts, 0))
    tile_ids = jnp.arange(n_tiles, dtype=jnp.int32)
    tile_e = jnp.minimum(jnp.sum((tile_ids[:, None] >= tile_end[None, :]).astype(jnp.int32), axis=1), last_e)
    u = jnp.arange(tm, dtype=jnp.int32)
    rank = (tile_ids - tile_off[tile_e])[:, None] * tm + u[None, :]
    valid = (tile_ids < n_used)[:, None] & (rank < counts[tile_e][:, None])
    src = order[jnp.clip(off[tile_e][:, None] + rank, 0, n_pairs - 1)]
    tok = jnp.where(valid, src % t, 0).astype(jnp.int32)
    spare = n_pairs + jnp.broadcast_to(u[None, :], (1, tm))
    dst = jnp.where(valid, src, spare).astype(jnp.int32)
    tok_next = jnp.concatenate([tok[1:], jnp.zeros((2, tm), jnp.int32)], axis=0)
    dst_prev = jnp.concatenate([spare.astype(jnp.int32), dst], axis=0)
    tile_e = jnp.concatenate([tile_e, tile_e[-1:]]).astype(jnp.int32)
    return (tile_e, n_used.reshape(1).astype(jnp.int32), tok[0].reshape(1, 1, tm),
            tok_next.reshape(n_tiles + 1, 1, tm), dst_prev.reshape(n_tiles + 1, 1, tm))


def _moe_kernel(te_ref, nu_ref, tok0_ref, tokn_ref, dstp_ref, hn_hbm, wg_ref, wu_ref, wd_ref,
                y2_hbm, xbuf, xb, acc, ybuf, gsem, ssem, *, n_ff):
    del te_ref
    i = pl.program_id(0)
    j = pl.program_id(1)
    n_used = nu_ref[0]
    tm = xbuf.shape[0]
    fc = wg_ref.shape[2]
    rows_per_step = tm // n_ff
    spare0 = y2_hbm.shape[0] - tm
    unroll = 8

    def gather_copy(tok, u):
        return pltpu.make_async_copy(hn_hbm.at[pl.ds(tok, 1), :], xbuf.at[pl.ds(u, 1), :], gsem)

    def scatter_copy(u, row):
        return pltpu.make_async_copy(ybuf.at[pl.ds(u, 1), :], y2_hbm.at[pl.ds(row, 1), :], ssem)

    def start_gather(ref):
        def body(u, carry):
            gather_copy(ref[0, 0, u], u).start()
            return carry
        lax.fori_loop(0, tm, body, 0, unroll=unroll)

    def wait_gather():
        pltpu.make_async_copy(hn_hbm.at[pl.ds(0, tm), :], xbuf, gsem).wait()

    def start_scatter_all():
        def body(u, carry):
            scatter_copy(u, dstp_ref[0, 0, u]).start()
            return carry
        lax.fori_loop(0, tm, body, 0, unroll=unroll)

    def wait_scatter():
        pltpu.make_async_copy(ybuf, y2_hbm.at[pl.ds(0, tm), :], ssem).wait()

    @pl.when((i == 0) & (j == 0))
    def _():
        ybuf[...] = jnp.zeros_like(ybuf)
        fill = pltpu.make_async_copy(ybuf, y2_hbm.at[pl.ds(spare0, tm), :], ssem)
        fill.start()
        fill.wait()
        start_gather(tok0_ref)

    def issue_gathers(jj):
        for u in range(jj * rows_per_step, (jj + 1) * rows_per_step):
            gather_copy(tokn_ref[0, 0, u], u).start()

    def ff_step(jj):
        if jj == 0:
            wait_gather()
            xb[...] = xbuf[...].astype(BF16)
            for u in range(tm):
                scatter_copy(u, dstp_ref[0, 0, u]).start(priority=1)
            issue_gathers(0)
        if jj == n_ff - 1:
            wait_scatter()

        x = xb[...]
        part = None
        for lo in range(0, fc, MOE_SUB):
            cols = slice(lo, min(lo + MOE_SUB, fc))
            hidden = (_silu(_dot(x, wg_ref[0, :, cols])) * _dot(x, wu_ref[0, :, cols])).astype(BF16)
            d = _dot(hidden, wd_ref[0, cols, :])
            part = d if part is None else part + d

        if jj == 0:
            acc[...] = part
        elif jj < n_ff - 1:
            acc[...] += part
        else:
            ybuf[...] = acc[...] + part

    for jj in range(n_ff):
        if jj > 0:
            pl.when((i < n_used) & (j == jj))(functools.partial(issue_gathers, jj))
        pl.when((i < n_used) & (j == jj))(functools.partial(ff_step, jj))

    @pl.when((i == n_used) & (j == 0))
    def _():
        start_scatter_all()
        wait_scatter()
        wait_gather()


def _moe(hn, tile_e, n_used, tok0, tok_next, dst_prev, w_gate, w_up, w_down):
    t = hn.shape[0]
    d_ff = w_gate.shape[2]
    tm, fc = MOE_TM, MOE_FC
    n_steps = tok_next.shape[0]
    n_ff = d_ff // fc
    assert n_ff >= 2 and n_ff * fc == d_ff and tm % n_ff == 0

    def ff_step(i, j, nu):
        return jnp.where(i < nu[0], j, n_ff - 1)

    grid_spec = pltpu.PrefetchScalarGridSpec(
        num_scalar_prefetch=2,
        grid=(n_steps, n_ff),
        in_specs=[
            pl.BlockSpec((1, 1, tm), lambda i, j, te, nu: (0, 0, 0), memory_space=pltpu.SMEM),
            pl.BlockSpec((1, 1, tm), lambda i, j, te, nu: (i, 0, 0), memory_space=pltpu.SMEM),
            pl.BlockSpec((1, 1, tm), lambda i, j, te, nu: (i, 0, 0), memory_space=pltpu.SMEM),
            pl.BlockSpec(memory_space=pl.ANY),
            pl.BlockSpec((1, D_MODEL, fc), lambda i, j, te, nu: (te[i], 0, ff_step(i, j, nu))),
            pl.BlockSpec((1, D_MODEL, fc), lambda i, j, te, nu: (te[i], 0, ff_step(i, j, nu))),
            pl.BlockSpec((1, fc, D_MODEL), lambda i, j, te, nu: (te[i], ff_step(i, j, nu), 0)),
        ],
        out_specs=pl.BlockSpec(memory_space=pl.ANY),
        scratch_shapes=[
            pltpu.VMEM((tm, D_MODEL), F32),
            pltpu.VMEM((tm, D_MODEL), BF16),
            pltpu.VMEM((tm, D_MODEL), F32),
            pltpu.VMEM((tm, D_MODEL), F32),
            pltpu.SemaphoreType.DMA(()),
            pltpu.SemaphoreType.DMA(()),
        ],
    )
    return pl.pallas_call(
        functools.partial(_moe_kernel, n_ff=n_ff),
        grid_spec=grid_spec,
        out_shape=jax.ShapeDtypeStruct((TOP_K * t + tm, D_MODEL), F32),
        compiler_params=_params("arbitrary", "arbitrary"),
        name="moe",
    )(tile_e, n_used, tok0, tok_next, dst_prev, hn, w_gate, w_up, w_down)


def _final_kernel(h_ref, ya_ref, yb_ref, wt_ref, g_ref, o_ref):
    w = wt_ref[...]
    moe = ya_ref[...] * w[:, 0:1] + yb_ref[...] * w[:, 1:2]
    o_ref[...] = _rms(h_ref[...] + moe, g_ref[...])


def _final(h3, y2, top_w, g):
    t = h3.shape[0]
    tm = ROW_TM
    second = t // tm
    row = pl.BlockSpec((tm, D_MODEL), lambda i: (i, 0))
    return pl.pallas_call(
        _final_kernel,
        grid=(t // tm,),
        in_specs=[row, row, pl.BlockSpec((tm, D_MODEL), lambda i: (i + second, 0)),
                  pl.BlockSpec((tm, TOP_K), lambda i: (i, 0)), pl.BlockSpec((1, D_MODEL), lambda i: (0, 0))],
        out_specs=row,
        out_shape=jax.ShapeDtypeStruct((t, D_MODEL), F32),
        compiler_params=_params("arbitrary"),
        name="final_norm",
    )(h3, y2, y2, top_w, g)


def _rotary_tables(seq):
    half = RET_QK_DIM // 2
    inv = ROPE_BASE ** (-np.arange(half, dtype=np.float64) / half)
    ang = np.arange(seq, dtype=np.float64)[:, None] * inv[None, :]
    return jnp.asarray(np.cos(ang), F32), jnp.asarray(np.sin(ang), F32)


def _channel_dft_tables(n, scale):
    k = np.arange(n, dtype=np.int64)
    ang = ((k[:, None] * k[None, :]) % n) * (2.0 * math.pi / n)
    return (jnp.asarray(np.cos(ang) * scale, F32).astype(BF16),
            jnp.asarray(-np.sin(ang) * scale, F32).astype(BF16))


def _sequence_dft_tables(seq, scale):
    blk = FNET_SEQ_BLOCK
    n1 = seq // blk
    k1 = np.arange(n1, dtype=np.int64)[:, None, None]
    k2 = np.arange(blk, dtype=np.int64)[None, :, None]
    s2 = np.arange(blk, dtype=np.int64)[None, None, :]
    ang = (((k1 + n1 * k2) * s2) % seq) * (2.0 * math.pi / seq)
    gc = jnp.asarray((np.cos(ang) * scale).reshape(n1 * blk, blk), F32).astype(BF16)
    gs = jnp.asarray((np.sin(ang) * scale).reshape(n1 * blk, blk), F32).astype(BF16)
    return gc, gs


def kernel(x, mix_norm, ffn_norm, ret_w_in, ret_decay_logit, ret_gn_gain, ret_w_out,
           dense_w_gate, dense_w_up, dense_w_down, fnet_w_out, moe_router,
           moe_w_gate, moe_w_up, moe_w_down, final_norm):
    batch, seq, d = x.shape
    t = batch * seq
    n1 = seq // FNET_SEQ_BLOCK
    assert d == D_MODEL and seq % RET_CHUNK == 0 and seq % PROJ_TM == 0
    assert n1 * FNET_SEQ_BLOCK == seq and n1 & (n1 - 1) == 0
    x2d = x.reshape(t, d)

    cos, sin = _rotary_tables(seq)
    n_exp, _, d_ffe = moe_w_gate[0].shape
    proj, (moe_wd,) = _proj(x2d, mix_norm[0:1], cos, sin, ret_w_in[0].astype(BF16), seq,
                            [moe_w_down[0].reshape(n_exp * d_ffe, d)])
    log_gamma = jax.nn.log_sigmoid(ret_decay_logit[0].astype(F32))
    z, (w_out, d_wg, d_wu, d_wd) = _retention(
        proj, log_gamma, ret_gn_gain[0:1], batch, seq,
        [ret_w_out[0], dense_w_gate[0], dense_w_up[0], dense_w_down[0]])
    cc, sc = _channel_dft_tables(FNET_GROUP_DIM, FNET_GROUP_DIM ** -0.5)
    gc, gs = _sequence_dft_tables(seq, seq ** -0.5)
    (h2, zr, zi), (moe_wu, w_fnet) = _ffn0(
        x2d, z, w_out, ffn_norm[0:1], d_wg, d_wu, d_wd, mix_norm[1:2], cc, sc,
        [moe_w_up[0].reshape(n_exp * d, d_ffe), fnet_w_out[0]])

    y, (moe_wg,) = _fnet_sequence(zr, zi, gc, gs, batch, seq, [moe_w_gate[0].reshape(n_exp * d, d_ffe)])
    router_t = moe_router[0].T
    r_hi = router_t.astype(BF16)
    r_lo = (router_t - r_hi.astype(F32)).astype(BF16)
    h3, hn3, top_idx, top_w = _router(h2, y, w_fnet, ffn_norm[1:2], r_hi, r_lo)

    tile_e, n_used, tok0, tok_next, dst_prev = _routing_tables(top_idx, MOE_TM)
    y2 = _moe(hn3, tile_e, n_used, tok0, tok_next, dst_prev,
              moe_wg.reshape(n_exp, d, d_ffe), moe_wu.reshape(n_exp, d, d_ffe),
              moe_wd.reshape(n_exp, d_ffe, d))
    out = _final(h3, y2, top_w.T, final_norm.reshape(1, d))
    return out.reshape(batch, seq, d)
```

```python
import functools
import math

import jax
import jax.numpy as jnp
import numpy as np
from jax import lax
from jax.experimental import pallas as pl
from jax.experimental.pallas import tpu as pltpu

F32 = jnp.float32
BF16 = jnp.bfloat16

D_MODEL = 1024
RET_HEADS = 4
RET_QK_DIM = D_MODEL // RET_HEADS
RET_V_DIM = 2 * RET_QK_DIM
RET_QK_WIDTH = RET_HEADS * RET_QK_DIM
RET_V_WIDTH = RET_HEADS * RET_V_DIM
RET_IN_WIDTH = 2 * RET_QK_WIDTH + 2 * RET_V_WIDTH
ROPE_BASE = 10000.0
FNET_GROUPS = 4
FNET_GROUP_DIM = D_MODEL // FNET_GROUPS
N_EXPERTS = 8
TOP_K = 2
EPS = 1e-6

RET_CHUNK = 256
VMEM_LIMIT_BYTES = 56 * 1024 * 1024

PROJ_TM, PROJ_TN = 1024, 2048
FFN0_TM, FFN0_SUB = 512, 512
ROW_TM = 1024
FNET_SEQ_BLOCK = 256
FNET_DT = 256
FNET_ROW_CHUNK = 16
RET_STATE_UNROLL, RET_CHUNK_UNROLL, FNET_DFT_UNROLL = 8, 16, 2
MOE_TM, MOE_FC, MOE_SUB = 1024, 1792, 512


def _params(*semantics):
    return pltpu.CompilerParams(dimension_semantics=semantics, vmem_limit_bytes=VMEM_LIMIT_BYTES)


def _dot(a, b):
    return jnp.dot(a, b, preferred_element_type=F32)


def _lane_mean(x):
    lanes = 128
    n = x.shape[-1]
    acc = x[:, 0:lanes]
    for lo in range(lanes, n, lanes):
        acc = acc + x[:, lo:lo + lanes]
    return jnp.sum(acc, axis=-1, keepdims=True) * (1.0 / n)


def _rms(x, g):
    return x * lax.rsqrt(_lane_mean(x * x) + EPS) * g


def _silu(a):
    return a * (1.0 / (1.0 + jnp.exp(-a)))


BF16_ROW_TILE = 16


def _cast_specs(arrays, n_steps, step_of):
    specs, shapes = [], []
    for a in arrays:
        rows, cols = a.shape
        n_blocks = n_steps
        while rows % (n_blocks * BF16_ROW_TILE):
            n_blocks //= 2
        assert n_blocks >= 1

        def index(*ids, n_blocks=n_blocks):
            return (jnp.minimum(step_of(*ids), n_blocks - 1), 0)

        specs.append(pl.BlockSpec((rows // n_blocks, cols), index))
        shapes.append(jax.ShapeDtypeStruct(a.shape, BF16))
    return specs, shapes


def _with_casts(body, n_in, n_out, n_cast):
    def kernel(*refs):
        ins, refs = refs[:n_in], refs[n_in:]
        srcs, refs = refs[:n_cast], refs[n_cast:]
        outs, refs = refs[:n_out], refs[n_out:]
        dsts, scratch = refs[:n_cast], refs[n_cast:]
        for src, dst in zip(srcs, dsts):
            dst[...] = src[...].astype(BF16)
        body(*ins, *outs, *scratch)
    return kernel


def _proj_kernel(x_ref, g_ref, cos_ref, sin_ref, w_ref, o_ref, hn_ref):
    j = pl.program_id(1)

    @pl.when(j == 0)
    def _():
        hn_ref[...] = _rms(x_ref[...], g_ref[...]).astype(BF16)

    tn = o_ref.shape[1]
    half = RET_QK_DIM // 2
    k_start, v_start, g_start = RET_QK_WIDTH, 2 * RET_QK_WIDTH, 2 * RET_QK_WIDTH + RET_V_WIDTH

    def column_tile(jj):
        hn = hn_ref[...]
        for h in range(tn // RET_QK_DIM):
            lo = h * RET_QK_DIM
            col = jj * tn + lo
            y = _dot(hn, w_ref[:, col:col + RET_QK_DIM])
            if col < v_start:
                scale = RET_QK_DIM ** -0.5 if col >= k_start else 1.0
                cos = cos_ref[...] * scale
                sin = sin_ref[...] * scale
                t1 = y[:, :half]
                t2 = y[:, half:]
                o_ref[:, lo:lo + half] = (t1 * cos - t2 * sin).astype(BF16)
                o_ref[:, lo + half:lo + RET_QK_DIM] = (t1 * sin + t2 * cos).astype(BF16)
            elif col < g_start:
                o_ref[:, lo:lo + RET_QK_DIM] = y.astype(BF16)
            else:
                o_ref[:, lo:lo + RET_QK_DIM] = _silu(y).astype(BF16)

    for jj in range(w_ref.shape[1] // tn):
        pl.when(j == jj)(functools.partial(column_tile, jj))


def _proj(x2d, g, cos, sin, w_in, seq, casts):
    t = x2d.shape[0]
    tm, tn = PROJ_TM, PROJ_TN
    pos_tiles = seq // tm
    cast_specs, cast_shapes = _cast_specs(casts, t // tm, lambda i, j: i)
    outs = pl.pallas_call(
        _with_casts(_proj_kernel, 5, 1, len(casts)),
        grid=(t // tm, RET_IN_WIDTH // tn),
        in_specs=[
            pl.BlockSpec((tm, D_MODEL), lambda i, j: (i, 0)),
            pl.BlockSpec((1, D_MODEL), lambda i, j: (0, 0)),
            pl.BlockSpec((tm, RET_QK_DIM // 2), lambda i, j: (i % pos_tiles, 0)),
            pl.BlockSpec((tm, RET_QK_DIM // 2), lambda i, j: (i % pos_tiles, 0)),
            pl.BlockSpec((D_MODEL, RET_IN_WIDTH), lambda i, j: (0, 0), pipeline_mode=pl.Buffered(1)),
        ] + cast_specs,
        out_specs=[pl.BlockSpec((tm, tn), lambda i, j: (i, j))] + cast_specs,
        out_shape=[jax.ShapeDtypeStruct((t, RET_IN_WIDTH), BF16)] + cast_shapes,
        scratch_shapes=[pltpu.VMEM((tm, D_MODEL), BF16)],
        compiler_params=_params("arbitrary", "arbitrary"),
        name="proj",
    )(x2d, g, cos, sin, w_in, *casts)
    return outs[0], outs[1:]


def _ret_kernel(lg_ref, q_ref, k_ref, v_ref, g_ref, gain_ref, o_ref, sfa_ref, sba_ref, sf_ref, sb_ref):
    c = RET_CHUNK
    n_chunks = q_ref.shape[0] // c
    h = pl.program_id(1)
    lgf = lg_ref[0, h]
    lgb = lg_ref[1, h]

    row = lax.broadcasted_iota(jnp.int32, (c, RET_QK_DIM), 0).astype(F32)
    xi_f = jnp.exp((row + 1.0) * lgf).astype(BF16)
    zeta_f = jnp.exp((c - 1.0 - row) * lgf).astype(BF16)
    xi_b = jnp.exp((c - row) * lgb).astype(BF16)
    zeta_b = jnp.exp(row * lgb).astype(BF16)
    g_f = jnp.exp(jnp.full((1, RET_V_DIM), c, F32) * lgf)
    g_b = jnp.exp(jnp.full((1, RET_V_DIM), c, F32) * lgb)
    di = lax.broadcasted_iota(jnp.int32, (c, c), 0)
    dj = lax.broadcasted_iota(jnp.int32, (c, c), 1)
    diff = (di - dj).astype(F32)
    decay = jnp.exp(jnp.where(diff >= 0, diff * lgf, -diff * lgb))

    def chunk(n):
        r0 = pl.multiple_of(n * c, c)
        return pl.ds(r0, c)

    def scaled(t, s):
        return t * s

    def outer(kz, vn):
        return lax.dot_general(kz, vn, (((0,), (0,)), ((), ())), preferred_element_type=F32)

    sf_ref[...] = jnp.zeros_like(sf_ref)
    sb_ref[...] = jnp.zeros_like(sb_ref)

    def states(step, carry):
        nf = step
        nb = n_chunks - 1 - step
        sf = sf_ref[...]
        sb = sb_ref[...]
        sfa_ref[nf] = sf.astype(BF16)
        sba_ref[nb] = sb.astype(BF16)
        rf, rb = chunk(nf), chunk(nb)
        sf_ref[...] = sf * g_f + outer(scaled(k_ref[rf, :], zeta_f), v_ref[rf, :])
        sb_ref[...] = sb * g_b + outer(scaled(k_ref[rb, :], zeta_b), v_ref[rb, :])
        return carry

    lax.fori_loop(0, n_chunks, states, 0, unroll=RET_STATE_UNROLL)

    gain = gain_ref[...]

    def fwd(n, carry):
        rows = chunk(n)
        qn, kn, vn = q_ref[rows, :], k_ref[rows, :], v_ref[rows, :]
        scores = lax.dot_general(qn, kn, (((1,), (1,)), ((), ())), preferred_element_type=F32) * decay
        y = (_dot(scores.astype(BF16), vn) + _dot(scaled(qn, xi_f), sfa_ref[n])
             + _dot(scaled(qn, xi_b), sba_ref[n]))
        mu = _lane_mean(y)
        yc = y - mu
        var = _lane_mean(yc * yc)
        yn = yc * lax.rsqrt(var + EPS) * gain
        o_ref[rows, :] = g_ref[rows, :] * yn.astype(BF16)
        return carry

    lax.fori_loop(0, n_chunks, fwd, 0, unroll=RET_CHUNK_UNROLL)


def _retention(proj, log_gamma, gain, batch, seq, casts):
    t = proj.shape[0]
    qk_blocks = RET_QK_WIDTH // RET_QK_DIM
    k_off = qk_blocks
    v_off = 2 * RET_QK_WIDTH // RET_V_DIM
    g_off = v_off + RET_V_WIDTH // RET_V_DIM
    cast_specs, cast_shapes = _cast_specs(casts, batch * RET_HEADS, lambda b, h: b * RET_HEADS + h)
    outs = pl.pallas_call(
        _with_casts(_ret_kernel, 6, 1, len(casts)),
        grid=(batch, RET_HEADS),
        in_specs=[
            pl.BlockSpec(memory_space=pltpu.SMEM),
            pl.BlockSpec((seq, RET_QK_DIM), lambda b, h: (b, h)),
            pl.BlockSpec((seq, RET_QK_DIM), lambda b, h: (b, k_off + h)),
            pl.BlockSpec((seq, RET_V_DIM), lambda b, h: (b, v_off + h)),
            pl.BlockSpec((seq, RET_V_DIM), lambda b, h: (b, g_off + h)),
            pl.BlockSpec((1, RET_V_DIM), lambda b, h: (0, h)),
        ] + cast_specs,
        out_specs=[pl.BlockSpec((seq, RET_V_DIM), lambda b, h: (b, h))] + cast_specs,
        out_shape=[jax.ShapeDtypeStruct((t, RET_V_WIDTH), BF16)] + cast_shapes,
        scratch_shapes=[
            pltpu.VMEM((seq // RET_CHUNK, RET_QK_DIM, RET_V_DIM), BF16),
            pltpu.VMEM((seq // RET_CHUNK, RET_QK_DIM, RET_V_DIM), BF16),
            pltpu.VMEM((RET_QK_DIM, RET_V_DIM), F32),
            pltpu.VMEM((RET_QK_DIM, RET_V_DIM), F32),
        ],
        compiler_params=_params("arbitrary", "arbitrary"),
        name="retention",
    )(log_gamma, proj, proj, proj, proj, gain, *casts)
    return outs[0], outs[1:]


def _ffn0_kernel(x_ref, z_ref, wo_ref, g_ref, wg_ref, wu_ref, wd_ref, g2_ref, cc_ref, sc_ref,
                 o_ref, zr_ref, zi_ref):
    h1 = x_ref[...] + _dot(z_ref[...], wo_ref[...])
    hn = _rms(h1, g_ref[...]).astype(BF16)
    d_ff = wg_ref.shape[1]
    part = None
    for lo in range(0, d_ff, FFN0_SUB):
        cols = slice(lo, min(lo + FFN0_SUB, d_ff))
        hidden = (_silu(_dot(hn, wg_ref[:, cols])) * _dot(hn, wu_ref[:, cols])).astype(BF16)
        d = _dot(hidden, wd_ref[cols, :])
        part = d if part is None else part + d
    h2 = h1 + part
    o_ref[...] = h2
    hn2 = _rms(h2, g2_ref[...]).astype(BF16)
    for grp in range(FNET_GROUPS):
        cols = slice(grp * FNET_GROUP_DIM, (grp + 1) * FNET_GROUP_DIM)
        zr_ref[:, cols] = _dot(hn2[:, cols], cc_ref[...]).astype(BF16)
        zi_ref[:, cols] = _dot(hn2[:, cols], sc_ref[...]).astype(BF16)


def _ffn0(x2d, z, w_out, g, w_gate, w_up, w_down, g2, cc, sc, casts):
    t = x2d.shape[0]
    d_ff = w_gate.shape[1]
    tm = FFN0_TM
    once = pl.Buffered(1)
    row = pl.BlockSpec((tm, D_MODEL), lambda i: (i, 0))
    cast_specs, cast_shapes = _cast_specs(casts, t // tm, lambda i: i)
    outs = pl.pallas_call(
        _with_casts(_ffn0_kernel, 10, 3, len(casts)),
        grid=(t // tm,),
        in_specs=[
            row,
            pl.BlockSpec((tm, RET_V_WIDTH), lambda i: (i, 0)),
            pl.BlockSpec((RET_V_WIDTH, D_MODEL), lambda i: (0, 0), pipeline_mode=once),
            pl.BlockSpec((1, D_MODEL), lambda i: (0, 0), pipeline_mode=once),
            pl.BlockSpec((D_MODEL, d_ff), lambda i: (0, 0), pipeline_mode=once),
            pl.BlockSpec((D_MODEL, d_ff), lambda i: (0, 0), pipeline_mode=once),
            pl.BlockSpec((d_ff, D_MODEL), lambda i: (0, 0), pipeline_mode=once),
            pl.BlockSpec((1, D_MODEL), lambda i: (0, 0), pipeline_mode=once),
            pl.BlockSpec((FNET_GROUP_DIM, FNET_GROUP_DIM), lambda i: (0, 0), pipeline_mode=once),
            pl.BlockSpec((FNET_GROUP_DIM, FNET_GROUP_DIM), lambda i: (0, 0), pipeline_mode=once),
        ] + cast_specs,
        out_specs=[row, row, row] + cast_specs,
        out_shape=[jax.ShapeDtypeStruct((t, D_MODEL), F32), jax.ShapeDtypeStruct((t, D_MODEL), BF16),
                   jax.ShapeDtypeStruct((t, D_MODEL), BF16)] + cast_shapes,
        compiler_params=_params("arbitrary"),
        name="ffn0",
    )(x2d, z, w_out, g, w_gate, w_up, w_down, g2, cc, sc, *casts)
    return outs[:3], outs[3:]


def _dft_across_blocks(xs):
    n = len(xs)
    if n == 1:
        return xs
    even = _dft_across_blocks(xs[0::2])
    odd = _dft_across_blocks(xs[1::2])
    out = [None] * n
    for k in range(n // 2):
        er, ei = even[k]
        o_r, o_i = odd[k]
        if k == 0:
            tr, ti = o_r, o_i
            out[k], out[k + n // 2] = (er + tr, ei + ti), (er - tr, ei - ti)
        elif 4 * k == n:
            out[k], out[k + n // 2] = (er + o_i, ei - o_r), (er - o_i, ei + o_r)
        else:
            c, s = math.cos(2.0 * math.pi * k / n), math.sin(2.0 * math.pi * k / n)
            tr = o_r * c + o_i * s
            ti = o_i * c - o_r * s
            out[k], out[k + n // 2] = (er + tr, ei + ti), (er - tr, ei - ti)
    return out


def _fnet_seq_kernel(zr_ref, zi_ref, gc_ref, gs_ref, o_ref, ar_ref, ai_ref, ys_ref):
    blk = FNET_SEQ_BLOCK
    n1 = zr_ref.shape[0] // blk
    rc = FNET_ROW_CHUNK
    lanes = 128

    def body(r, carry):
        r0 = pl.multiple_of(r * rc, rc)
        for l in range(zr_ref.shape[1] // lanes):
            cols = slice(l * lanes, (l + 1) * lanes)
            zs = [(zr_ref[pl.ds(s1 * blk + r0, rc), cols].astype(F32),
                   zi_ref[pl.ds(s1 * blk + r0, rc), cols].astype(F32)) for s1 in range(n1)]
            for k1, (a_r, a_i) in enumerate(_dft_across_blocks(zs)):
                ar_ref[pl.ds(k1 * blk + r0, rc), cols] = a_r.astype(BF16)
                ai_ref[pl.ds(k1 * blk + r0, rc), cols] = a_i.astype(BF16)
        return carry

    lax.fori_loop(0, blk // rc, body, 0, unroll=FNET_DFT_UNROLL)

    for k1 in range(n1):
        rows = slice(k1 * blk, (k1 + 1) * blk)
        y = _dot(gc_ref[rows, :], ar_ref[rows, :]) + _dot(gs_ref[rows, :], ai_ref[rows, :])
        for l in range(y.shape[1] // lanes):
            ys_ref[l, pl.ds(k1, blk, stride=n1), :] = y[:, l * lanes:(l + 1) * lanes]
    for l in range(o_ref.shape[1] // lanes):
        o_ref[:, l * lanes:(l + 1) * lanes] = ys_ref[l]


def _fnet_sequence(zr, zi, gc, gs, batch, seq, casts):
    t = zr.shape[0]
    dt = FNET_DT
    blkspec = pl.BlockSpec((seq, dt), lambda b, c: (b, c))
    tab = pl.BlockSpec((seq, FNET_SEQ_BLOCK), lambda b, c: (0, 0))
    col_tiles = D_MODEL // dt
    cast_specs, cast_shapes = _cast_specs(casts, batch * col_tiles, lambda b, c: b * col_tiles + c)
    outs = pl.pallas_call(
        _with_casts(_fnet_seq_kernel, 4, 1, len(casts)),
        grid=(batch, col_tiles),
        in_specs=[blkspec, blkspec, tab, tab] + cast_specs,
        out_specs=[blkspec] + cast_specs,
        out_shape=[jax.ShapeDtypeStruct((t, D_MODEL), F32)] + cast_shapes,
        scratch_shapes=[pltpu.VMEM((seq, dt), BF16), pltpu.VMEM((seq, dt), BF16),
                        pltpu.VMEM((dt // 128, seq, 128), F32)],
        compiler_params=_params("arbitrary", "arbitrary"),
        name="fnet_sequence",
    )(zr, zi, gc, gs, *casts)
    return outs[0], outs[1:]


def _router_kernel(h_ref, y_ref, w_ref, g_ref, rhi_ref, rlo_ref, h3_ref, hn_ref, idx_ref, wt_ref):
    h3 = h_ref[...] + _dot(y_ref[...].astype(BF16), w_ref[...])
    h3_ref[...] = h3
    hn = _rms(h3, g_ref[...])
    hn_ref[...] = hn
    hi = hn.astype(BF16)
    lo = (hn - hi.astype(F32)).astype(BF16)

    def nt(a, b):
        return lax.dot_general(a, b, (((1,), (1,)), ((), ())), preferred_element_type=F32)

    logits = nt(rhi_ref[...], hi) + (nt(rhi_ref[...], lo) + nt(rlo_ref[...], hi))
    row = lax.broadcasted_iota(jnp.int32, logits.shape, 0)
    m1 = jnp.max(logits, axis=0, keepdims=True)
    i1 = jnp.min(jnp.where(logits == m1, row, N_EXPERTS), axis=0, keepdims=True)
    rest = jnp.where(row == i1, -jnp.inf, logits)
    m2 = jnp.max(rest, axis=0, keepdims=True)
    i2 = jnp.min(jnp.where(rest == m2, row, N_EXPERTS), axis=0, keepdims=True)
    e2 = jnp.exp(m2 - m1)
    inv = 1.0 / (1.0 + e2)
    idx_ref[0:1, :] = i1
    idx_ref[1:2, :] = i2
    wt_ref[0:1, :] = inv
    wt_ref[1:2, :] = e2 * inv


def _router(h2, y, w_fnet, g, r_hi, r_lo):
    t = h2.shape[0]
    tm = ROW_TM
    row = pl.BlockSpec((tm, D_MODEL), lambda i: (i, 0))
    pair = pl.BlockSpec((TOP_K, tm), lambda i: (0, i))
    rtr = pl.BlockSpec((N_EXPERTS, D_MODEL), lambda i: (0, 0))
    return pl.pallas_call(
        _router_kernel,
        grid=(t // tm,),
        in_specs=[row, row, pl.BlockSpec((D_MODEL, D_MODEL), lambda i: (0, 0)),
                  pl.BlockSpec((1, D_MODEL), lambda i: (0, 0)), rtr, rtr],
        out_specs=[row, row, pair, pair],
        out_shape=[jax.ShapeDtypeStruct((t, D_MODEL), F32), jax.ShapeDtypeStruct((t, D_MODEL), F32),
                   jax.ShapeDtypeStruct((TOP_K, t), jnp.int32), jax.ShapeDtypeStruct((TOP_K, t), F32)],
        compiler_params=_params("arbitrary"),
        name="router",
    )(h2, y, w_fnet, g, r_hi, r_lo)


def _routing_tables(top_idx, tm):
    t = top_idx.shape[1]
    n_pairs = TOP_K * t
    n_tiles = n_pairs // tm + N_EXPERTS
    e_flat = top_idx.reshape(n_pairs)
    order = jnp.argsort(e_flat, stable=True).astype(jnp.int32)
    experts = jnp.arange(N_EXPERTS, dtype=jnp.int32)
    counts = jnp.sum((e_flat[:, None] == experts[None, :]).astype(jnp.int32), axis=0)
    off = jnp.cumsum(counts) - counts
    tiles_e = (counts + tm - 1) // tm
    tile_end = jnp.cumsum(tiles_e)
    tile_off = tile_end - tiles_e
    n_used = tile_end[-1]
    last_e = jnp.max(jnp.where(tiles_e > 0, experts, 0))
    tile_ids = jnp.arange(n_tiles, dtype=jnp.int32)
    tile_e = jnp.minimum(jnp.sum((tile_ids[:, None] >= tile_end[None, :]).astype(jnp.int32), axis=1), last_e)
    u = jnp.arange(tm, dtype=jnp.int32)
    rank = (tile_ids - tile_off[tile_e])[:, None] * tm + u[None, :]
    valid = (tile_ids < n_used)[:, None] & (rank < counts[tile_e][:, None])
    src = order[jnp.clip(off[tile_e][:, None] + rank, 0, n_pairs - 1)]
    tok = jnp.where(valid, src % t, 0).astype(jnp.int32)
    spare = n_pairs + jnp.broadcast_to(u[None, :], (1, tm))
    dst = jnp.where(valid, src, spare).astype(jnp.int32)
    tok_next = jnp.concatenate([tok[1:], jnp.zeros((2, tm), jnp.int32)], axis=0)
    dst_prev = jnp.concatenate([spare.astype(jnp.int32), dst], axis=0)
    tile_e = jnp.concatenate([tile_e, tile_e[-1:]]).astype(jnp.int32)
    return (tile_e, n_used.reshape(1).astype(jnp.int32), tok[0].reshape(1, 1, tm),
            tok_next.reshape(n_tiles + 1, 1, tm), dst_prev.reshape(n_tiles + 1, 1, tm))


def _moe_kernel(te_ref, nu_ref, tok0_ref, tokn_ref, dstp_ref, hn_hbm, wg_ref, wu_ref, wd_ref,
                y2_hbm, xbuf, xb, acc, ybuf, gsem, ssem, *, n_ff):
    del te_ref
    i = pl.program_id(0)
    j = pl.program_id(1)
    n_used = nu_ref[0]
    tm = xbuf.shape[0]
    fc = wg_ref.shape[2]
    rows_per_step = tm // n_ff
    spare0 = y2_hbm.shape[0] - tm
    unroll = 8

    def gather_copy(tok, u):
        return pltpu.make_async_copy(hn_hbm.at[pl.ds(tok, 1), :], xbuf.at[pl.ds(u, 1), :], gsem)

    def scatter_copy(u, row):
        return pltpu.make_async_copy(ybuf.at[pl.ds(u, 1), :], y2_hbm.at[pl.ds(row, 1), :], ssem)

    def start_gather(ref):
        def body(u, carry):
            gather_copy(ref[0, 0, u], u).start()
            return carry
        lax.fori_loop(0, tm, body, 0, unroll=unroll)

    def wait_gather():
        pltpu.make_async_copy(hn_hbm.at[pl.ds(0, tm), :], xbuf, gsem).wait()

    def start_scatter_all():
        def body(u, carry):
            scatter_copy(u, dstp_ref[0, 0, u]).start()
            return carry
        lax.fori_loop(0, tm, body, 0, unroll=unroll)

    def wait_scatter():
        pltpu.make_async_copy(ybuf, y2_hbm.at[pl.ds(0, tm), :], ssem).wait()

    @pl.when((i == 0) & (j == 0))
    def _():
        ybuf[...] = jnp.zeros_like(ybuf)
        fill = pltpu.make_async_copy(ybuf, y2_hbm.at[pl.ds(spare0, tm), :], ssem)
        fill.start()
        fill.wait()
        start_gather(tok0_ref)

    def issue_gathers(jj):
        for u in range(jj * rows_per_step, (jj + 1) * rows_per_step):
            gather_copy(tokn_ref[0, 0, u], u).start()

    def ff_step(jj):
        if jj == 0:
            wait_gather()
            xb[...] = xbuf[...].astype(BF16)
            for u in range(tm):
                scatter_copy(u, dstp_ref[0, 0, u]).start(priority=1)
            issue_gathers(0)
        if jj == n_ff - 1:
            wait_scatter()

        x = xb[...]
        part = None
        for lo in range(0, fc, MOE_SUB):
            cols = slice(lo, min(lo + MOE_SUB, fc))
            hidden = (_silu(_dot(x, wg_ref[0, :, cols])) * _dot(x, wu_ref[0, :, cols])).astype(BF16)
            d = _dot(hidden, wd_ref[0, cols, :])
            part = d if part is None else part + d

        if jj == 0:
            acc[...] = part
        elif jj < n_ff - 1:
            acc[...] += part
        else:
            ybuf[...] = acc[...] + part

    for jj in range(n_ff):
        if jj > 0:
            pl.when((i < n_used) & (j == jj))(functools.partial(issue_gathers, jj))
        pl.when((i < n_used) & (j == jj))(functools.partial(ff_step, jj))

    @pl.when((i == n_used) & (j == 0))
    def _():
        start_scatter_all()
        wait_scatter()
        wait_gather()


def _moe(hn, tile_e, n_used, tok0, tok_next, dst_prev, w_gate, w_up, w_down):
    t = hn.shape[0]
    d_ff = w_gate.shape[2]
    tm, fc = MOE_TM, MOE_FC
    n_steps = tok_next.shape[0]
    n_ff = d_ff // fc
    assert n_ff >= 2 and n_ff * fc == d_ff and tm % n_ff == 0

    def ff_step(i, j, nu):
        return jnp.where(i < nu[0], j, n_ff - 1)

    grid_spec = pltpu.PrefetchScalarGridSpec(
        num_scalar_prefetch=2,
        grid=(n_steps, n_ff),
        in_specs=[
            pl.BlockSpec((1, 1, tm), lambda i, j, te, nu: (0, 0, 0), memory_space=pltpu.SMEM),
            pl.BlockSpec((1, 1, tm), lambda i, j, te, nu: (i, 0, 0), memory_space=pltpu.SMEM),
            pl.BlockSpec((1, 1, tm), lambda i, j, te, nu: (i, 0, 0), memory_space=pltpu.SMEM),
            pl.BlockSpec(memory_space=pl.ANY),
            pl.BlockSpec((1, D_MODEL, fc), lambda i, j, te, nu: (te[i], 0, ff_step(i, j, nu))),
            pl.BlockSpec((1, D_MODEL, fc), lambda i, j, te, nu: (te[i], 0, ff_step(i, j, nu))),
            pl.BlockSpec((1, fc, D_MODEL), lambda i, j, te, nu: (te[i], ff_step(i, j, nu), 0)),
        ],
        out_specs=pl.BlockSpec(memory_space=pl.ANY),
        scratch_shapes=[
            pltpu.VMEM((tm, D_MODEL), F32),
            pltpu.VMEM((tm, D_MODEL), BF16),
            pltpu.VMEM((tm, D_MODEL), F32),
            pltpu.VMEM((tm, D_MODEL), F32),
            pltpu.SemaphoreType.DMA(()),
            pltpu.SemaphoreType.DMA(()),
        ],
    )
    return pl.pallas_call(
        functools.partial(_moe_kernel, n_ff=n_ff),
        grid_spec=grid_spec,
        out_shape=jax.ShapeDtypeStruct((TOP_K * t + tm, D_MODEL), F32),
        compiler_params=_params("arbitrary", "arbitrary"),
        name="moe",
    )(tile_e, n_used, tok0, tok_next, dst_prev, hn, w_gate, w_up, w_down)


def _final_kernel(h_ref, ya_ref, yb_ref, wt_ref, g_ref, o_ref):
    w = wt_ref[...]
    moe = ya_ref[...] * w[:, 0:1] + yb_ref[...] * w[:, 1:2]
    o_ref[...] = _rms(h_ref[...] + moe, g_ref[...])


def _final(h3, y2, top_w, g):
    t = h3.shape[0]
    tm = ROW_TM
    second = t // tm
    row = pl.BlockSpec((tm, D_MODEL), lambda i: (i, 0))
    return pl.pallas_call(
        _final_kernel,
        grid=(t // tm,),
        in_specs=[row, row, pl.BlockSpec((tm, D_MODEL), lambda i: (i + second, 0)),
                  pl.BlockSpec((tm, TOP_K), lambda i: (i, 0)), pl.BlockSpec((1, D_MODEL), lambda i: (0, 0))],
        out_specs=row,
        out_shape=jax.ShapeDtypeStruct((t, D_MODEL), F32),
        compiler_params=_params("arbitrary"),
        name="final_norm",
    )(h3, y2, y2, top_w, g)


def _rotary_tables(seq):
    half = RET_QK_DIM // 2
    inv = ROPE_BASE ** (-np.arange(half, dtype=np.float64) / half)
    ang = np.arange(seq, dtype=np.float64)[:, None] * inv[None, :]
    return jnp.asarray(np.cos(ang), F32), jnp.asarray(np.sin(ang), F32)


def _channel_dft_tables(n, scale):
    k = np.arange(n, dtype=np.int64)
    ang = ((k[:, None] * k[None, :]) % n) * (2.0 * math.pi / n)
    return (jnp.asarray(np.cos(ang) * scale, F32).astype(BF16),
            jnp.asarray(-np.sin(ang) * scale, F32).astype(BF16))


def _sequence_dft_tables(seq, scale):
    blk = FNET_SEQ_BLOCK
    n1 = seq // blk
    k1 = np.arange(n1, dtype=np.int64)[:, None, None]
    k2 = np.arange(blk, dtype=np.int64)[None, :, None]
    s2 = np.arange(blk, dtype=np.int64)[None, None, :]
    ang = (((k1 + n1 * k2) * s2) % seq) * (2.0 * math.pi / seq)
    gc = jnp.asarray((np.cos(ang) * scale).reshape(n1 * blk, blk), F32).astype(BF16)
    gs = jnp.asarray((np.sin(ang) * scale).reshape(n1 * blk, blk), F32).astype(BF16)
    return gc, gs


def kernel(x, mix_norm, ffn_norm, ret_w_in, ret_decay_logit, ret_gn_gain, ret_w_out,
           dense_w_gate, dense_w_up, dense_w_down, fnet_w_out, moe_router,
           moe_w_gate, moe_w_up, moe_w_down, final_norm):
    batch, seq, d = x.shape
    t = batch * seq
    n1 = seq // FNET_SEQ_BLOCK
    assert d == D_MODEL and seq % RET_CHUNK == 0 and seq % PROJ_TM == 0
    assert n1 * FNET_SEQ_BLOCK == seq and n1 & (n1 - 1) == 0
    x2d = x.reshape(t, d)

    cos, sin = _rotary_tables(seq)
    n_exp, _, d_ffe = moe_w_gate[0].shape
    proj, (moe_wd,) = _proj(x2d, mix_norm[0:1], cos, sin, ret_w_in[0].astype(BF16), seq,
                            [moe_w_down[0].reshape(n_exp * d_ffe, d)])
    log_gamma = jax.nn.log_sigmoid(ret_decay_logit[0].astype(F32))
    z, (w_out, d_wg, d_wu, d_wd) = _retention(
        proj, log_gamma, ret_gn_gain[0:1], batch, seq,
        [ret_w_out[0], dense_w_gate[0], dense_w_up[0], dense_w_down[0]])
    cc, sc = _channel_dft_tables(FNET_GROUP_DIM, FNET_GROUP_DIM ** -0.5)
    gc, gs = _sequence_dft_tables(seq, seq ** -0.5)
    (h2, zr, zi), (moe_wu, w_fnet) = _ffn0(
        x2d, z, w_out, ffn_norm[0:1], d_wg, d_wu, d_wd, mix_norm[1:2], cc, sc,
        [moe_w_up[0].reshape(n_exp * d, d_ffe), fnet_w_out[0]])

    y, (moe_wg,) = _fnet_sequence(zr, zi, gc, gs, batch, seq, [moe_w_gate[0].reshape(n_exp * d, d_ffe)])
    router_t = moe_router[0].T
    r_hi = router_t.astype(BF16)
    r_lo = (router_t - r_hi.astype(F32)).astype(BF16)
    h3, hn3, top_idx, top_w = _router(h2, y, w_fnet, ffn_norm[1:2], r_hi, r_lo)

    tile_e, n_used, tok0, tok_next, dst_prev = _routing_tables(top_idx, MOE_TM)
    y2 = _moe(hn3, tile_e, n_used, tok0, tok_next, dst_prev,
              moe_wg.reshape(n_exp, d, d_ffe), moe_wu.reshape(n_exp, d, d_ffe),
              moe_wd.reshape(n_exp, d_ffe, d))
    out = _final(h3, y2, top_w.T, final_norm.reshape(1, d))
    return out.reshape(batch, seq, d)
```
